```python
import math
import jax, jax.numpy as jnp
from jax import lax
import numpy as np

D_MODEL = 1024
BATCH = 8
SEQ = 2048
DEPTH = 4
DEC_BATCH = 128
DEC_SEQ = 4
PAST_LEN = 16384
PAGE_SIZE = 128

N_EVEN = (DEPTH + 1) // 2
N_ODD = DEPTH // 2
H_A = 4
DK_A = 128
DV_A = 128
CONV_W = 4
CHUNK_A = 64
G_B = 4
DG_B = 128
CHUNK_B = 128
H_C = 4
DK_C = 128
DV_C = 256
CHUNK_C = 64
D_FF = 4 * D_MODEL
EPS = 1e-6

QK_A = H_A * DK_A
V_A = H_A * DV_A
CONV_CH = 2 * QK_A + V_A
W_B = G_B * DG_B
D_IN_EVEN = CONV_CH + V_A + 2 * H_A + 2 * W_B
D_MIX_EVEN = V_A + W_B
QK_C = H_C * DK_C
V_C = H_C * DV_C
D_IN_ODD = 2 * QK_C + V_C + 2 * H_C + V_C

kernel_name = 'hybrid_gdn_sgu_mlstm_step'


def rmsnorm(x, g):
    xf = x.astype(jnp.float32)
    y = xf * lax.rsqrt(jnp.mean(xf * xf, -1, keepdims=True) + EPS)
    return (y * g.astype(jnp.float32)).astype(x.dtype)


def l2norm(x):
    xf = x.astype(jnp.float32)
    return xf * lax.rsqrt(jnp.sum(xf * xf, -1, keepdims=True) + EPS)


def split_cols(x, sizes):
    outs, start = [], 0
    for s in sizes:
        outs.append(x[..., start:start + s])
        start += s
    return outs


def causal_mask(c, strict=False):
    r = jnp.arange(c)
    return (r[:, None] > r[None, :]) if strict else (r[:, None] >= r[None, :])


def to_blocks(t, nc, c):
    b, _, h = t.shape[:3]
    t = t.astype(jnp.float32).reshape((b, nc, c, h) + t.shape[3:])
    return jnp.moveaxis(t, (1, 3), (0, 2))


def from_blocks(o):
    nc, b, h, c, d = o.shape
    return jnp.moveaxis(o, (0, 2), (1, 3)).reshape(b, nc * c, h, d)


def short_conv(x, buf, w):
    L = x.shape[1]
    xp = jnp.concatenate([buf.astype(x.dtype), x], axis=1)
    y = sum(w[j] * xp[:, j:j + L] for j in range(CONV_W))
    return jax.nn.silu(y), xp[:, L:]


def gated_delta_rule(q, k, v, g, beta, S0):
    B, L, H, DK = q.shape
    DV = v.shape[-1]
    C = math.gcd(L, CHUNK_A)
    NC = L // C
    q = to_blocks(q, NC, C) * (DK ** -0.5)
    k = to_blocks(k, NC, C)
    v = to_blocks(v, NC, C)
    beta = to_blocks(beta, NC, C)
    gc = jnp.cumsum(to_blocks(g, NC, C), -1)
    causal = causal_mask(C)
    strict = causal_mask(C, strict=True)
    gdiff = jnp.where(causal, gc[..., :, None] - gc[..., None, :], 0.0)
    decay = jnp.where(causal, jnp.exp(gdiff), 0.0)
    kb = k * beta[..., None]
    A = jnp.where(strict, jnp.einsum('nbhid,nbhjd->nbhij', kb, k) * decay, 0.0)
    rhs = jnp.concatenate([v * beta[..., None], kb * jnp.exp(gc)[..., None]], -1)
    sol = lax.linalg.triangular_solve(jnp.eye(C, dtype=jnp.float32) + A, rhs,
                                      left_side=True, lower=True, unit_diagonal=True)
    u_val, w_k = sol[..., :DV], sol[..., DV:]
    qk = jnp.where(causal, jnp.einsum('nbhid,nbhjd->nbhij', q, k) * decay, 0.0)
    q_dec = q * jnp.exp(gc)[..., None]
    k_dec = k * jnp.exp(gc[..., -1:] - gc)[..., None]
    g_last = jnp.exp(gc[..., -1])

    def step(S, xs):
        u_c, wk_c, qk_c, qd_c, kd_c, gl_c = xs
        v_new = u_c - jnp.einsum('bhik,bhkv->bhiv', wk_c, S)
        o = jnp.einsum('bhik,bhkv->bhiv', qd_c, S) + jnp.einsum('bhij,bhjv->bhiv', qk_c, v_new)
        S = S * gl_c[..., None, None] + jnp.einsum('bhik,bhiv->bhkv', kd_c, v_new)
        return S, o

    S, o = lax.scan(step, S0.astype(jnp.float32), (u_val, w_k, qk, q_dec, k_dec, g_last))
    return from_blocks(o), S


def spatial_gating(u, v, w_s, b_s):
    B, L = u.shape[:2]
    C = min(L, CHUNK_B)
    NC = L // C
    w = jnp.where(causal_mask(C), w_s[:, :C, :C], 0.0)
    vc = v.reshape(B, NC, C, G_B, DG_B)
    z = jnp.einsum('gpq,bnqgc->bnpgc', w, vc) + b_s[:, :C].T[None, None, :, :, None]
    return u * z.reshape(u.shape).astype(u.dtype)


def mlstm_chunked(q, k, v, i_pre, f_pre, C0, n0, m0):
    B, L, H, DK = q.shape
    Cs = math.gcd(L, CHUNK_C)
    NC = L // Cs
    q = to_blocks(q, NC, Cs) * (DK ** -0.5)
    k = to_blocks(k, NC, Cs)
    v = to_blocks(v, NC, Cs)
    ig = to_blocks(i_pre, NC, Cs)
    bcum = jnp.cumsum(jax.nn.log_sigmoid(to_blocks(f_pre, NC, Cs)), -1)
    causal = causal_mask(Cs)

    def step(carry, xs):
        Cm, n, m = carry
        q_c, k_c, v_c, i_c, b_c = xs
        inter = b_c + m[..., None]
        dmat = jnp.where(causal, b_c[..., :, None] - b_c[..., None, :] + i_c[..., None, :], -jnp.inf)
        m_t = jnp.maximum(inter, jnp.max(dmat, -1))
        w_intra = jnp.exp(dmat - m_t[..., None])
        w_inter = jnp.exp(inter - m_t)
        s = jnp.einsum('bhid,bhjd->bhij', q_c, k_c) * w_intra
        num = w_inter[..., None] * jnp.einsum('bhid,bhdv->bhiv', q_c, Cm) + jnp.einsum('bhij,bhjv->bhiv', s, v_c)
        den = w_inter * jnp.einsum('bhid,bhd->bhi', q_c, n) + jnp.sum(s, -1)
        h = num / jnp.maximum(jnp.abs(den), jnp.exp(-m_t))[..., None]
        b_last = b_c[..., -1]
        log_w = b_last[..., None] - b_c + i_c
        m_new = jnp.maximum(b_last + m, jnp.max(log_w, -1))
        keep = jnp.exp(b_last + m - m_new)
        wk = jnp.exp(log_w - m_new[..., None])
        Cm = keep[..., None, None] * Cm + jnp.einsum('bhj,bhjd,bhjv->bhdv', wk, k_c, v_c)
        n = keep[..., None] * n + jnp.einsum('bhj,bhjd->bhd', wk, k_c)
        return (Cm, n, m_new), h

    init = (C0.astype(jnp.float32), n0.astype(jnp.float32), m0.astype(jnp.float32))
    (Cm, n, m), h = lax.scan(step, init, (q, k, v, ig, bcum))
    return from_blocks(h), Cm, n, m


def even_mixer(h, S0, conv_buf, w_in, conv_w, a_log, dt_bias, gdn_norm_g, sgu_norm_g, sgu_w, sgu_b, w_out):
    B, L, _ = h.shape
    proj = h @ w_in
    qkv, gate, beta_pre, a_pre, u_pre, v_pre = split_cols(proj, (CONV_CH, V_A, H_A, H_A, W_B, W_B))
    qkv_c, new_buf = short_conv(qkv, conv_buf, conv_w)
    q, k, v = split_cols(qkv_c, (QK_A, QK_A, V_A))
    q = l2norm(q.reshape(B, L, H_A, DK_A))
    k = l2norm(k.reshape(B, L, H_A, DK_A))
    v = v.reshape(B, L, H_A, DV_A)
    beta = jax.nn.sigmoid(beta_pre.astype(jnp.float32))
    g = -jnp.exp(a_log.astype(jnp.float32)) * jax.nn.softplus(a_pre.astype(jnp.float32) + dt_bias.astype(jnp.float32))
    o, S = gated_delta_rule(q, k, v, g, beta, S0)
    o_a = rmsnorm(o, gdn_norm_g) * jax.nn.silu(gate.reshape(B, L, H_A, DV_A).astype(jnp.float32))
    o_a = o_a.reshape(B, L, V_A).astype(h.dtype)
    u = jax.nn.gelu(u_pre, approximate=False).reshape(B, L, G_B, DG_B)
    vb = rmsnorm(jax.nn.gelu(v_pre, approximate=False).reshape(B, L, G_B, DG_B), sgu_norm_g)
    o_b = spatial_gating(u, vb, sgu_w, sgu_b).reshape(B, L, W_B)
    out = jnp.concatenate([o_a, o_b], -1) @ w_out
    return out, S, new_buf, vb.reshape(B, L, W_B)


def odd_mixer(h, C0, n0, m0, w_in, gate_b, head_norm_g, w_out):
    B, L, _ = h.shape
    proj = h @ w_in
    q, k, v, i_pre, f_pre, o_pre = split_cols(proj, (QK_C, QK_C, V_C, H_C, H_C, V_C))
    i_pre = i_pre.astype(jnp.float32) + gate_b[:H_C]
    f_pre = f_pre.astype(jnp.float32) + gate_b[H_C:]
    hh, Cm, n, m = mlstm_chunked(q.reshape(B, L, H_C, DK_C), k.reshape(B, L, H_C, DK_C),
                                 v.reshape(B, L, H_C, DV_C), i_pre, f_pre, C0, n0, m0)
    hh = rmsnorm(hh, head_norm_g) * jax.nn.sigmoid(o_pre.reshape(B, L, H_C, DV_C).astype(jnp.float32))
    out = hh.reshape(B, L, V_C).astype(h.dtype) @ w_out
    return out, Cm, n, m


def squared_relu_mlp(h, w1, w2):
    return jnp.square(jax.nn.relu(h @ w1)) @ w2


def trunk(x, s_gdn, s_conv, s_c, s_n, s_m, norm_g, w_in_even, conv_w, a_log, dt_bias, gdn_norm_g,
          sgu_norm_g, sgu_w, sgu_b, w_out_even, w_in_odd, gate_b_odd, mlstm_norm_g, w_out_odd, w_ff1, w_ff2):
    new_gdn, new_conv, new_v, new_c, new_n, new_m = [], [], [], [], [], []
    for l in range(DEPTH):
        gl = norm_g[l]
        h = rmsnorm(x, gl[0])
        if l % 2 == 0:
            e = l // 2
            mix, S, buf, vrows = even_mixer(h, s_gdn[e], s_conv[e], w_in_even[e], conv_w[e], a_log[e], dt_bias[e],
                                            gdn_norm_g[e], sgu_norm_g[e], sgu_w[e], sgu_b[e], w_out_even[e])
            new_gdn.append(S)
            new_conv.append(buf)
            new_v.append(vrows)
        else:
            o = l // 2
            mix, Cm, n, m = odd_mixer(h, s_c[o], s_n[o], s_m[o], w_in_odd[o], gate_b_odd[o], mlstm_norm_g[o], w_out_odd[o])
            new_c.append(Cm)
            new_n.append(n)
            new_m.append(m)
        x = x + rmsnorm(mix, gl[1])
        h = rmsnorm(x, gl[2])
        x = x + rmsnorm(squared_relu_mlp(h, w_ff1[l], w_ff2[l]), gl[3])
    return (x, jnp.stack(new_gdn), jnp.stack(new_conv), jnp.stack(new_v),
            jnp.stack(new_c), jnp.stack(new_n), jnp.stack(new_m))


def setup_inputs(seed: int = 0) -> dict:
    key = jax.random.key(seed)
    ks = jax.random.split(key, 24)
    nrm = jax.random.normal
    f32 = jnp.float32
    x_prompt = nrm(ks[0], (BATCH, SEQ, D_MODEL), f32)
    x_sample = nrm(ks[1], (DEC_BATCH, DEC_SEQ, D_MODEL), f32)
    state_gdn = 0.05 * nrm(ks[2], (N_EVEN, DEC_BATCH, H_A, DK_A, DV_A), f32)
    state_gdn_conv = nrm(ks[3], (N_EVEN, DEC_BATCH, CONV_W - 1, CONV_CH), f32)
    state_mlstm_c = 0.1 * nrm(ks[4], (N_ODD, DEC_BATCH, H_C, DK_C, DV_C), f32)
    state_mlstm_n = 0.1 * nrm(ks[5], (N_ODD, DEC_BATCH, H_C, DK_C), f32)
    state_mlstm_m = nrm(ks[6], (N_ODD, DEC_BATCH, H_C), f32)
    norm_g = 1.0 + 0.05 * nrm(ks[7], (DEPTH, 4, D_MODEL), f32)
    w_in_even = nrm(ks[8], (N_EVEN, D_MODEL, D_IN_EVEN), f32) * D_MODEL ** -0.5
    conv_w = 0.5 * nrm(ks[9], (N_EVEN, CONV_W, CONV_CH), f32)
    a_log = jnp.log(jax.random.uniform(ks[10], (N_EVEN, H_A), f32, 1.0, 16.0))
    dt = jnp.exp(jax.random.uniform(ks[11], (N_EVEN, H_A), f32, math.log(1e-3), math.log(1e-1)))
    dt_bias = dt + jnp.log(-jnp.expm1(-dt))
    gdn_norm_g = 1.0 + 0.05 * nrm(ks[12], (N_EVEN, DV_A), f32)
    sgu_norm_g = 1.0 + 0.05 * nrm(ks[13], (N_EVEN, G_B, DG_B), f32)
    sgu_w = nrm(ks[14], (N_EVEN, G_B, CHUNK_B, CHUNK_B), f32) * CHUNK_B ** -0.5
    sgu_b = 1.0 + 0.1 * nrm(ks[15], (N_EVEN, G_B, CHUNK_B), f32)
    w_out_even = nrm(ks[16], (N_EVEN, D_MIX_EVEN, D_MODEL), f32) * D_MIX_EVEN ** -0.5
    w_in_odd = nrm(ks[17], (N_ODD, D_MODEL, D_IN_ODD), f32) * D_MODEL ** -0.5
    gate_b_odd = jnp.concatenate([0.1 * nrm(ks[18], (N_ODD, H_C), f32),
                                  3.0 + 0.5 * nrm(ks[19], (N_ODD, H_C), f32)], -1)
    mlstm_norm_g = 1.0 + 0.05 * nrm(ks[20], (N_ODD, DV_C), f32)
    w_out_odd = nrm(ks[21], (N_ODD, V_C, D_MODEL), f32) * V_C ** -0.5
    w_ff1 = nrm(ks[22], (DEPTH, D_MODEL, D_FF), f32) * D_MODEL ** -0.5
    w_ff2 = nrm(ks[23], (DEPTH, D_FF, D_MODEL), f32) * D_FF ** -0.5
    return {'x_prompt': x_prompt, 'x_sample': x_sample,
            'state_gdn': state_gdn, 'state_gdn_conv': state_gdn_conv,
            'state_mlstm_c': state_mlstm_c, 'state_mlstm_n': state_mlstm_n, 'state_mlstm_m': state_mlstm_m,
            'norm_g': norm_g, 'w_in_even': w_in_even, 'conv_w': conv_w, 'a_log': a_log, 'dt_bias': dt_bias,
            'gdn_norm_g': gdn_norm_g, 'sgu_norm_g': sgu_norm_g, 'sgu_w': sgu_w, 'sgu_b': sgu_b,
            'w_out_even': w_out_even, 'w_in_odd': w_in_odd, 'gate_b_odd': gate_b_odd,
            'mlstm_norm_g': mlstm_norm_g, 'w_out_odd': w_out_odd, 'w_ff1': w_ff1, 'w_ff2': w_ff2}


def reference(x_prompt, x_sample, state_gdn, state_gdn_conv, state_mlstm_c, state_mlstm_n, state_mlstm_m,
              norm_g, w_in_even, conv_w, a_log, dt_bias, gdn_norm_g, sgu_norm_g, sgu_w, sgu_b, w_out_even,
              w_in_odd, gate_b_odd, mlstm_norm_g, w_out_odd, w_ff1, w_ff2):
    f32 = jnp.float32
    bp = x_prompt.shape[0]
    y_prompt, p_gdn, p_conv, _, p_c, p_n, p_m = trunk(
        x_prompt,
        jnp.zeros((N_EVEN, bp, H_A, DK_A, DV_A), f32),
        jnp.zeros((N_EVEN, bp, CONV_W - 1, CONV_CH), x_prompt.dtype),
        jnp.zeros((N_ODD, bp, H_C, DK_C, DV_C), f32),
        jnp.zeros((N_ODD, bp, H_C, DK_C), f32),
        jnp.zeros((N_ODD, bp, H_C), f32),
        norm_g, w_in_even, conv_w, a_log, dt_bias, gdn_norm_g, sgu_norm_g, sgu_w, sgu_b, w_out_even,
        w_in_odd, gate_b_odd, mlstm_norm_g, w_out_odd, w_ff1, w_ff2)
    y_sample, s_gdn, s_conv, s_v, s_c, s_n, s_m = trunk(
        x_sample, state_gdn, state_gdn_conv, state_mlstm_c, state_mlstm_n, state_mlstm_m,
        norm_g, w_in_even, conv_w, a_log, dt_bias, gdn_norm_g, sgu_norm_g, sgu_w, sgu_b, w_out_even,
        w_in_odd, gate_b_odd, mlstm_norm_g, w_out_odd, w_ff1, w_ff2)
    return (y_prompt, y_sample, p_gdn, s_gdn, p_conv, s_conv, s_v, p_c, s_c, p_n, s_n, p_m, s_m)
```

```python
import functools
import math
from types import SimpleNamespace

import jax
import jax.numpy as jnp
from jax import lax
from jax.experimental import pallas as pl
from jax.experimental.pallas import tpu as pltpu

F32 = jnp.float32
BF16 = jnp.bfloat16
HIGHEST = lax.Precision.HIGHEST

EPS = 1e-6
N_HEADS = 4
HEAD_DIM = 128
MLSTM_DV = 256
CONV_W = 4
CHUNK_ROWS = 64
SGU_CHUNK = 128
SMALL_COLS = 128
SMALL_ROWS = 16
VMEM_LIMIT = 56 * 1024 * 1024


def _cparams(sem):
    return pltpu.CompilerParams(dimension_semantics=sem, vmem_limit_bytes=VMEM_LIMIT)


def _mm(a, b):
    return jnp.dot(a.astype(BF16), b.astype(BF16), preferred_element_type=F32)


def _mm_nt(a, b):
    return lax.dot_general(a.astype(BF16), b.astype(BF16), (((1,), (1,)), ((), ())),
                           preferred_element_type=F32)


def _mm_hi(a, b):
    return jnp.dot(a, b, preferred_element_type=F32, precision=HIGHEST)


def _rms(x, g):
    return x * lax.rsqrt(jnp.mean(x * x, axis=-1, keepdims=True) + EPS) * g


def _l2norm(x):
    return x * lax.rsqrt(jnp.sum(x * x, axis=-1, keepdims=True) + EPS)


def _softplus(x):
    return jnp.maximum(x, 0.0) + jnp.log1p(jnp.exp(-jnp.abs(x)))


def _sigmoid(x):
    return 1.0 / (1.0 + jnp.exp(-x))


def _silu(x):
    return x * _sigmoid(x)


def _gelu(x):
    return 0.5 * x * (1.0 + lax.erf(x * (2.0 ** -0.5)))


def _proj_in_kernel(x_ref, g_ref, w_ref, wst_ref, o_ref, ot_ref):
    h = _rms(x_ref[...], g_ref[...]).astype(BF16)
    o_ref[...] = jnp.dot(h, w_ref[...], preferred_element_type=F32)
    ot_ref[...] = lax.dot_general(wst_ref[...], h, (((1,), (1,)), ((), ())),
                                  preferred_element_type=F32)


def _proj_in(x2d, g, w_main, w_small_t, tm):
    t, d = x2d.shape
    n = w_main.shape[1]
    return pl.pallas_call(
        _proj_in_kernel,
        grid=(t // tm,),
        in_specs=[pl.BlockSpec((tm, d), lambda i: (i, 0)),
                  pl.BlockSpec((1, d), lambda i: (0, 0)),
                  pl.BlockSpec((d, n), lambda i: (0, 0)),
                  pl.BlockSpec((SMALL_ROWS, d), lambda i: (0, 0))],
        out_specs=[pl.BlockSpec((tm, n), lambda i: (i, 0)),
                   pl.BlockSpec((SMALL_ROWS, tm), lambda i: (0, i))],
        out_shape=[jax.ShapeDtypeStruct((t, n), F32),
                   jax.ShapeDtypeStruct((SMALL_ROWS, t), F32)],
        compiler_params=_cparams(("parallel",)),
        name="proj_in",
    )(x2d, g.reshape(1, d), w_main, w_small_t)


def _proj_out_kernel(*refs, n_act):
    x_ref, g_ref = refs[0], refs[1]
    acts = refs[2:2 + n_act]
    ws = refs[2 + n_act:2 + 2 * n_act]
    o_ref = refs[2 + 2 * n_act]
    acc = None
    for a_ref, w_ref in zip(acts, ws):
        p = jnp.dot(a_ref[...].astype(BF16), w_ref[...], preferred_element_type=F32)
        acc = p if acc is None else acc + p
    o_ref[...] = x_ref[...] + _rms(acc, g_ref[...])


def _proj_out(x2d, g, acts, ws, tm):
    t, d = x2d.shape
    n_act = len(acts)
    in_specs = [pl.BlockSpec((tm, d), lambda i: (i, 0)), pl.BlockSpec((1, d), lambda i: (0, 0))]
    in_specs += [pl.BlockSpec((tm, a.shape[1]), lambda i: (i, 0)) for a in acts]
    in_specs += [pl.BlockSpec(w.shape, lambda i: (0, 0)) for w in ws]
    return pl.pallas_call(
        functools.partial(_proj_out_kernel, n_act=n_act),
        grid=(t // tm,),
        in_specs=in_specs,
        out_specs=pl.BlockSpec((tm, d), lambda i: (i, 0)),
        out_shape=jax.ShapeDtypeStruct((t, d), F32),
        compiler_params=_cparams(("parallel",)),
        name="proj_out",
    )(x2d, g.reshape(1, d), *acts, *ws)


def _ffn_kernel(x_ref, g2_ref, g3_ref, w1_ref, w2_ref, o_ref, h_ref, acc_ref):
    k = pl.program_id(1)

    @pl.when(k == 0)
    def _():
        h_ref[...] = _rms(x_ref[...], g2_ref[...]).astype(BF16)
        acc_ref[...] = jnp.zeros_like(acc_ref)

    a = jnp.dot(h_ref[...], w1_ref[...], preferred_element_type=F32)
    a = jnp.square(jnp.maximum(a, 0.0)).astype(BF16)
    acc_ref[...] += jnp.dot(a, w2_ref[...], preferred_element_type=F32)

    @pl.when(k == pl.num_programs(1) - 1)
    def _():
        o_ref[...] = x_ref[...] + _rms(acc_ref[...], g3_ref[...])


def _ffn(x2d, g2, g3, w1, w2, tm, tf):
    t, d = x2d.shape
    ff = w1.shape[1]
    return pl.pallas_call(
        _ffn_kernel,
        grid=(t // tm, ff // tf),
        in_specs=[pl.BlockSpec((tm, d), lambda i, k: (i, 0)),
                  pl.BlockSpec((1, d), lambda i, k: (0, 0)),
                  pl.BlockSpec((1, d), lambda i, k: (0, 0)),
                  pl.BlockSpec((d, tf), lambda i, k: (0, k)),
                  pl.BlockSpec((tf, d), lambda i, k: (k, 0))],
        out_specs=pl.BlockSpec((tm, d), lambda i, k: (i, 0)),
        out_shape=jax.ShapeDtypeStruct((t, d), F32),
        scratch_shapes=[pltpu.VMEM((tm, d), BF16), pltpu.VMEM((tm, d), F32)],
        compiler_params=_cparams(("parallel", "arbitrary")),
        name="ffn",
    )(x2d, g2.reshape(1, d), g3.reshape(1, d), w1, w2)


def _ind(mask):
    return jnp.where(mask, 1.0, 0.0).astype(F32)


def _div_pow2(x, d):
    return jnp.right_shift(x, int(math.log2(d)))


def _chunk_structs(rows, ns):
    lt = rows // ns
    ri = lax.broadcasted_iota(jnp.int32, (rows, rows), 0)
    ci = lax.broadcasted_iota(jnp.int32, (rows, rows), 1)
    cs = SimpleNamespace(rows=rows, ns=ns, lt=lt)
    if ns == 1:
        cs.same = None
        cs.causal = ri >= ci
        cs.strict = ri > ci
        cs.upper = ri <= ci
    else:
        rseq = _div_pow2(ri, lt)
        cseq = _div_pow2(ci, lt)
        cs.same = rseq == cseq
        cs.causal = cs.same & (ri >= ci)
        cs.strict = cs.same & (ri > ci)
        cs.upper = cs.same & (ri <= ci)
        cs.last = _ind(ci == rseq * lt + (lt - 1))
        cs.last_t = _ind(ri == cseq * lt + (lt - 1))
        sr = _div_pow2(lax.broadcasted_iota(jnp.int32, (ns * HEAD_DIM, rows), 0), HEAD_DIM)
        sc = lax.broadcasted_iota(jnp.int32, (ns * HEAD_DIM, rows), 1)
        cs.stack_last = _ind(sc == sr * lt + (lt - 1))
        cs.stack_mask = _ind(_div_pow2(sc, lt) == sr)
        er = _div_pow2(lax.broadcasted_iota(jnp.int32, (rows, ns * HEAD_DIM), 0), lt)
        ec = _div_pow2(lax.broadcasted_iota(jnp.int32, (rows, ns * HEAD_DIM), 1), HEAD_DIM)
        cs.expand_mask = _ind(er == ec)
        kr = lax.broadcasted_iota(jnp.int32, (ns, rows), 0)
        kc = lax.broadcasted_iota(jnp.int32, (ns, rows), 1)
        cs.seq_last = _ind(kc == kr * lt + (lt - 1))
        cs.seq_sum = _ind(_div_pow2(kc, lt) == kr)
        pr = _div_pow2(lax.broadcasted_iota(jnp.int32, (rows, HEAD_DIM), 0), lt)
        pc = lax.broadcasted_iota(jnp.int32, (rows, HEAD_DIM), 1)
        cs.seq_expand = _ind(pr == pc)
    cs.lower_f = _ind(cs.causal)
    cs.upper_f = _ind(cs.upper)
    eye_r = lax.broadcasted_iota(jnp.int32, (HEAD_DIM, HEAD_DIM), 0)
    eye_c = lax.broadcasted_iota(jnp.int32, (HEAD_DIM, HEAD_DIM), 1)
    cs.eye = _ind(eye_r == eye_c)
    cs.eye_rows = _ind(ri == ci)
    return cs


def _last_rows(cs, x):
    if cs.ns == 1:
        return jnp.broadcast_to(x[cs.rows - 1:cs.rows, :], x.shape)
    return _mm_hi(cs.last, x)


def _last_lanes(cs, x):
    if cs.ns == 1:
        return jnp.broadcast_to(x[:, cs.rows - 1:cs.rows], x.shape)
    return _mm_hi(x, cs.last_t)


def _expand_lhs(cs, x):
    if cs.ns == 1:
        return x
    reps = x.shape[0] // cs.rows
    mask = cs.expand_mask if reps == 1 else jnp.concatenate([cs.expand_mask] * reps, axis=0)
    return jnp.concatenate([x] * cs.ns, axis=1) * mask


def _stack_t(cs, x):
    xt = _mm_nt(cs.eye, x)
    if cs.ns == 1:
        return xt
    return jnp.concatenate([xt] * cs.ns, axis=0) * cs.stack_mask


def _stack_scalar(cs, col):
    if cs.ns == 1:
        return col[cs.rows - 1:cs.rows, :]
    return _mm_hi(cs.stack_last, jnp.broadcast_to(col, (cs.rows, HEAD_DIM)))[:, 0:1]


def _unit_lower_inverse(cs, a):
    n = -a
    t = cs.eye_rows + n
    p = n
    for _ in range(int(math.log2(cs.lt)) - 1):
        p = _mm_hi(p, p)
        t = t + _mm_hi(t, p)
    return t


def _gdn_chunk(cs, xc, gate, sm, smt, prm_row, prm_col, gng, s_list):
    hq = N_HEADS * HEAD_DIM
    beta_all = _sigmoid(sm)
    g_c = -prm_row[0:1, :] * _softplus(sm + prm_row[1:2, :])
    g_r = -prm_col[:, 0:1] * _softplus(smt + prm_col[:, 1:2])
    gc_c = _mm_hi(cs.lower_f, g_c)
    gc_r = _mm_hi(g_r, cs.upper_f)
    gl_c = _last_rows(cs, gc_c)
    egc = jnp.exp(gc_c)
    ekd = jnp.exp(gl_c - gc_c)
    outs, s_new = [], []
    for h in range(N_HEADS):
        lo = h * HEAD_DIM
        q = _l2norm(xc[:, lo:lo + HEAD_DIM]) * (HEAD_DIM ** -0.5)
        k = _l2norm(xc[:, hq + lo:hq + lo + HEAD_DIM])
        v = xc[:, 2 * hq + lo:2 * hq + lo + HEAD_DIM]
        beta = beta_all[:, h:h + 1]
        gcol = gc_c[:, N_HEADS + h:N_HEADS + h + 1]
        grow = gc_r[N_HEADS + h:N_HEADS + h + 1, :]
        eg = egc[:, N_HEADS + h:N_HEADS + h + 1]
        decay = jnp.where(cs.causal, jnp.exp(jnp.where(cs.causal, gcol - grow, 0.0)), 0.0)
        kb = k * beta
        a = jnp.where(cs.strict, _mm_nt(kb, k) * decay, 0.0)
        tinv = _unit_lower_inverse(cs, a)
        sol = _mm_hi(tinv, jnp.concatenate([v * beta, kb * eg], axis=1))
        u_val, w_k = sol[:, :HEAD_DIM], sol[:, HEAD_DIM:]
        qk = jnp.where(cs.causal, _mm_nt(q, k) * decay, 0.0)
        q_dec = q * eg
        k_dec = k * ekd[:, N_HEADS + h:N_HEADS + h + 1]
        s_old = s_list[h]
        p = _mm(_expand_lhs(cs, jnp.concatenate([w_k, q_dec], axis=0)), s_old)
        v_new = u_val - p[:cs.rows]
        o = p[cs.rows:] + _mm(qk, v_new)
        g_last = jnp.exp(_stack_scalar(cs, gl_c[:, N_HEADS + h:N_HEADS + h + 1]))
        s_new.append(s_old * g_last + _mm(_stack_t(cs, k_dec), v_new))
        outs.append(_rms(o, gng) * _silu(gate[:, lo:lo + HEAD_DIM]))
    return outs, s_new


def _gdn_prompt_kernel(qkv_ref, gate_ref, sm_ref, smt_ref, cw_ref, prow_ref, pcol_ref, gng_ref,
                       s0_ref, buf_ref, o_ref, sout_ref, bufout_ref, s_scr, xp_scr, *, tb):
    t = pl.program_id(1)
    cs = _chunk_structs(CHUNK_ROWS, 1)

    @pl.when(t == 0)
    def _():
        s_scr[...] = s0_ref[0]
        xp_scr[0:8, :] = buf_ref[0]

    x = qkv_ref[0]
    xp_scr[8:8 + tb, :] = x
    y = (cw_ref[0:1, :] * xp_scr[5:5 + tb, :] + cw_ref[1:2, :] * xp_scr[6:6 + tb, :]
         + cw_ref[2:3, :] * xp_scr[7:7 + tb, :] + cw_ref[3:4, :] * x)
    tail = xp_scr[tb:tb + 8, :]
    xp_scr[0:8, :] = tail
    xc_all = _silu(y)
    s_list = [s_scr[h] for h in range(N_HEADS)]
    for c in range(tb // CHUNK_ROWS):
        r0 = c * CHUNK_ROWS
        outs, s_list = _gdn_chunk(cs, xc_all[r0:r0 + CHUNK_ROWS], gate_ref[0, r0:r0 + CHUNK_ROWS, :],
                                  sm_ref[0, r0:r0 + CHUNK_ROWS, :], smt_ref[:, r0:r0 + CHUNK_ROWS],
                                  prow_ref[...], pcol_ref[...], gng_ref[...], s_list)
        for h in range(N_HEADS):
            o_ref[0, r0:r0 + CHUNK_ROWS, h * HEAD_DIM:(h + 1) * HEAD_DIM] = outs[h]
    for h in range(N_HEADS):
        s_scr[h] = s_list[h]

    @pl.when(t == pl.num_programs(1) - 1)
    def _():
        sout_ref[0] = s_scr[...]
        bufout_ref[0] = tail


def _gdn_prompt(proj, smt, conv_w, prm_row, prm_col, gng, s0, buf8, bsz, seq, tb):
    hq = N_HEADS * HEAD_DIM
    cch = 3 * hq
    nt = seq // tb
    proj3 = proj.reshape(bsz, seq, proj.shape[1])
    small_blk = (cch + 3 * hq) // SMALL_COLS
    return pl.pallas_call(
        functools.partial(_gdn_prompt_kernel, tb=tb),
        grid=(bsz, nt),
        in_specs=[pl.BlockSpec((1, tb, cch), lambda b, t: (b, t, 0)),
                  pl.BlockSpec((1, tb, hq), lambda b, t: (b, t, cch // hq)),
                  pl.BlockSpec((1, tb, SMALL_COLS), lambda b, t: (b, t, small_blk)),
                  pl.BlockSpec((SMALL_ROWS, tb), lambda b, t: (0, b * nt + t)),
                  pl.BlockSpec((CONV_W, cch), lambda b, t: (0, 0)),
                  pl.BlockSpec((2, SMALL_COLS), lambda b, t: (0, 0)),
                  pl.BlockSpec((SMALL_ROWS, 2), lambda b, t: (0, 0)),
                  pl.BlockSpec((1, HEAD_DIM), lambda b, t: (0, 0)),
                  pl.BlockSpec((1, N_HEADS, HEAD_DIM, HEAD_DIM), lambda b, t: (b, 0, 0, 0)),
                  pl.BlockSpec((1, 8, cch), lambda b, t: (b, 0, 0))],
        out_specs=[pl.BlockSpec((1, tb, hq), lambda b, t: (b, t, 0)),
                   pl.BlockSpec((1, N_HEADS, HEAD_DIM, HEAD_DIM), lambda b, t: (b, 0, 0, 0)),
                   pl.BlockSpec((1, 8, cch), lambda b, t: (b, 0, 0))],
        out_shape=[jax.ShapeDtypeStruct((bsz, seq, hq), F32),
                   jax.ShapeDtypeStruct((bsz, N_HEADS, HEAD_DIM, HEAD_DIM), F32),
                   jax.ShapeDtypeStruct((bsz, 8, cch), F32)],
        scratch_shapes=[pltpu.VMEM((N_HEADS, HEAD_DIM, HEAD_DIM), F32),
                        pltpu.VMEM((tb + 8, cch), F32)],
        compiler_params=_cparams(("parallel", "arbitrary")),
        name="gdn_prompt",
    )(proj3, proj3, proj3, smt, conv_w, prm_row, prm_col, gng.reshape(1, HEAD_DIM), s0, buf8)


def _conv_sample_kernel(xcat_ref, cw_ref, o_ref, *, lt, cch):
    for t in range(lt):
        y = None
        for j in range(CONV_W):
            term = cw_ref[j:j + 1, :] * xcat_ref[:, (t + j) * cch:(t + j + 1) * cch]
            y = term if y is None else y + term
        o_ref[:, t * cch:(t + 1) * cch] = _silu(y)


def _conv_sample(xcat, conv_w, lt, cch):
    nseq = xcat.shape[0]
    return pl.pallas_call(
        functools.partial(_conv_sample_kernel, lt=lt, cch=cch),
        grid=(1,),
        in_specs=[pl.BlockSpec(xcat.shape, lambda i: (0, 0)),
                  pl.BlockSpec((CONV_W, cch), lambda i: (0, 0))],
        out_specs=pl.BlockSpec((nseq, lt * cch), lambda i: (0, 0)),
        out_shape=jax.ShapeDtypeStruct((nseq, lt * cch), F32),
        compiler_params=_cparams(("arbitrary",)),
        name="conv_sample",
    )(xcat, conv_w)


def _gdn_sample_kernel(xc_ref, gate_ref, sm_ref, smt_ref, prow_ref, pcol_ref, gng_ref, s0_ref,
                       o_ref, sout_ref, *, ns):
    cs = _chunk_structs(CHUNK_ROWS, ns)
    s_list = [s0_ref[:, h].reshape(ns * HEAD_DIM, HEAD_DIM) for h in range(N_HEADS)]
    outs, s_new = _gdn_chunk(cs, xc_ref[...], gate_ref[...], sm_ref[...], smt_ref[0],
                             prow_ref[...], pcol_ref[...], gng_ref[...], s_list)
    for h in range(N_HEADS):
        o_ref[:, h * HEAD_DIM:(h + 1) * HEAD_DIM] = outs[h]
        sout_ref[:, h] = s_new[h].reshape(ns, HEAD_DIM, HEAD_DIM)


def _gdn_sample(xc, proj, smt, prm_row, prm_col, gng, s0, lt):
    hq = N_HEADS * HEAD_DIM
    cch = 3 * hq
    ns = CHUNK_ROWS // lt
    nchunk = xc.shape[0] // CHUNK_ROWS
    small_blk = (cch + 3 * hq) // SMALL_COLS
    return pl.pallas_call(
        functools.partial(_gdn_sample_kernel, ns=ns),
        grid=(nchunk,),
        in_specs=[pl.BlockSpec((CHUNK_ROWS, cch), lambda c: (c, 0)),
                  pl.BlockSpec((CHUNK_ROWS, hq), lambda c: (c, cch // hq)),
                  pl.BlockSpec((CHUNK_ROWS, SMALL_COLS), lambda c: (c, small_blk)),
                  pl.BlockSpec((1, SMALL_ROWS, CHUNK_ROWS), lambda c: (c, 0, 0)),
                  pl.BlockSpec((2, SMALL_COLS), lambda c: (0, 0)),
                  pl.BlockSpec((SMALL_ROWS, 2), lambda c: (0, 0)),
                  pl.BlockSpec((1, HEAD_DIM), lambda c: (0, 0)),
                  pl.BlockSpec((ns, N_HEADS, HEAD_DIM, HEAD_DIM), lambda c: (c, 0, 0, 0))],
        out_specs=[pl.BlockSpec((CHUNK_ROWS, hq), lambda c: (c, 0)),
                   pl.BlockSpec((ns, N_HEADS, HEAD_DIM, HEAD_DIM), lambda c: (c, 0, 0, 0))],
        out_shape=[jax.ShapeDtypeStruct((xc.shape[0], hq), F32),
                   jax.ShapeDtypeStruct(s0.shape, F32)],
        compiler_params=_cparams(("parallel",)),
        name="gdn_sample",
    )(xc, proj, proj, smt, prm_row, prm_col, gng.reshape(1, HEAD_DIM), s0)


def _sgu_kernel(u_ref, v_ref, w_ref, b_ref, g_ref, o_ref, vb_ref, *, ns):
    rows = u_ref.shape[0]
    lt = rows // ns
    ri = lax.broadcasted_iota(jnp.int32, (rows, rows), 0)
    ci = lax.broadcasted_iota(jnp.int32, (rows, rows), 1)
    keep = ri >= ci
    if ns > 1:
        keep = keep & (_div_pow2(ri, lt) == _div_pow2(ci, lt))
    for g in range(N_HEADS):
        lo = g * HEAD_DIM
        u = _gelu(u_ref[:, lo:lo + HEAD_DIM])
        vb = _rms(_gelu(v_ref[:, lo:lo + HEAD_DIM]), g_ref[g:g + 1, :])
        w = jnp.where(keep, w_ref[g], 0.0)
        z = _mm(w, vb) + b_ref[:, g:g + 1]
        o_ref[:, lo:lo + HEAD_DIM] = u * z
        vb_ref[:, lo:lo + HEAD_DIM] = vb


def _sgu(proj, w_tiles, b_cols, norm_g, rows, ns):
    hq = N_HEADS * HEAD_DIM
    t = proj.shape[0]
    u_blk = (3 * hq + hq) // hq
    return pl.pallas_call(
        functools.partial(_sgu_kernel, ns=ns),
        grid=(t // rows,),
        in_specs=[pl.BlockSpec((rows, hq), lambda i: (i, u_blk)),
                  pl.BlockSpec((rows, hq), lambda i: (i, u_blk + 1)),
                  pl.BlockSpec((N_HEADS, rows, rows), lambda i: (0, 0, 0)),
                  pl.BlockSpec((rows, SMALL_COLS), lambda i: (0, 0)),
                  pl.BlockSpec((N_HEADS, HEAD_DIM), lambda i: (0, 0))],
        out_specs=[pl.BlockSpec((rows, hq), lambda i: (i, 0)),
                   pl.BlockSpec((rows, hq), lambda i: (i, 0))],
        out_shape=[jax.ShapeDtypeStruct((t, hq), F32), jax.ShapeDtypeStruct((t, hq), F32)],
        compiler_params=_cparams(("parallel",)),
        name="sgu",
    )(proj, proj, w_tiles, b_cols, norm_g)


def _logsigmoid(x):
    return jnp.minimum(x, 0.0) - jnp.log1p(jnp.exp(-jnp.abs(x)))


def _mlstm_chunk(cs, q_all, k_all, v_all, opre, sm, smt, gb_row, gb_col, mng, c_list, n_rows, m_rows):
    neg_inf = -jnp.inf
    pre_c = sm + gb_row
    pre_r = smt + gb_col
    b_c = _mm_hi(cs.lower_f, _logsigmoid(pre_c))
    b_r = _mm_hi(_logsigmoid(pre_r), cs.upper_f)
    bl_c = _last_rows(cs, b_c)
    bl_r = _last_lanes(cs, b_r)
    res = []
    for h in range(N_HEADS):
        lo = h * HEAD_DIM
        vo = h * MLSTM_DV
        q = q_all[:, lo:lo + HEAD_DIM] * (HEAD_DIM ** -0.5)
        k = k_all[:, lo:lo + HEAD_DIM]
        v = v_all[:, vo:vo + MLSTM_DV]
        bcol = b_c[:, N_HEADS + h:N_HEADS + h + 1]
        brow = b_r[N_HEADS + h:N_HEADS + h + 1, :]
        icol = pre_c[:, h:h + 1]
        irow = pre_r[h:h + 1, :]
        blcol = bl_c[:, N_HEADS + h:N_HEADS + h + 1]
        blrow = bl_r[N_HEADS + h:N_HEADS + h + 1, :]
        mrow = m_rows[:, h:h + 1]
        inter = bcol + mrow
        dmat = jnp.where(cs.causal, bcol - brow + irow, neg_inf)
        m_t = jnp.maximum(inter, jnp.max(dmat, axis=-1, keepdims=True))
        w_intra = jnp.exp(dmat - m_t)
        w_inter = jnp.exp(inter - m_t)
        s = _mm_nt(q, k) * w_intra
        c_old = c_list[h]
        num = w_inter * _mm(_expand_lhs(cs, q), c_old) + _mm(s, v)
        qn = jnp.sum(q * n_rows[:, lo:lo + HEAD_DIM], axis=-1, keepdims=True)
        den = w_inter * qn + jnp.sum(s, axis=-1, keepdims=True)
        hval = num / jnp.maximum(jnp.abs(den), jnp.exp(-m_t))
        out = _rms(hval, mng) * _sigmoid(opre[:, vo:vo + MLSTM_DV])
        logw_row = blrow - brow + irow
        if cs.ns == 1:
            seq_max = jnp.max(logw_row, axis=-1, keepdims=True)
        else:
            seq_max = jnp.max(jnp.where(cs.same, logw_row, neg_inf), axis=-1, keepdims=True)
        m_new = jnp.maximum(blcol + mrow, seq_max)
        keep = jnp.exp(blcol + mrow - m_new)
        wk = jnp.exp(blcol - bcol + icol - m_new)
        kw = k * wk
        c_new = c_old * _stack_scalar(cs, keep) + _mm(_stack_t(cs, kw), v)
        res.append((out, c_new, keep, m_new, kw))
    return res


def _mlstm_prompt_kernel(q_ref, k_ref, v_ref, op_ref, sm_ref, smt_ref, gbr_ref, gbc_ref, mng_ref,
                         c0_ref, n0_ref, m0_ref, o_ref, cout_ref, nout_ref, mout_ref,
                         c_scr, n_scr, m_scr, *, tb):
    t = pl.program_id(1)
    cs = _chunk_structs(CHUNK_ROWS, 1)
    hq = N_HEADS * HEAD_DIM
    lane = lax.broadcasted_iota(jnp.int32, (1, SMALL_COLS), 1)

    @pl.when(t == 0)
    def _():
        c_scr[...] = c0_ref[0]
        n_scr[...] = n0_ref[0]
        m_scr[...] = m0_ref[0]

    c_list = [c_scr[h] for h in range(N_HEADS)]
    n_cur = n_scr[...]
    m_cur = m_scr[...]
    for c in range(tb // CHUNK_ROWS):
        r0 = c * CHUNK_ROWS
        rs = slice(r0, r0 + CHUNK_ROWS)
        res = _mlstm_chunk(cs, q_ref[0, rs, :], k_ref[0, rs, :], v_ref[0, rs, :], op_ref[0, rs, :],
                           sm_ref[0, rs, :], smt_ref[:, rs], gbr_ref[...], gbc_ref[...], mng_ref[...],
                           c_list, jnp.broadcast_to(n_cur, (CHUNK_ROWS, hq)),
                           jnp.broadcast_to(m_cur, (CHUNK_ROWS, SMALL_COLS)))
        n_parts = []
        for h, (out, c_new, keep, m_new, kw) in enumerate(res):
            o_ref[0, rs, h * MLSTM_DV:(h + 1) * MLSTM_DV] = out
            c_list[h] = c_new
            keep1 = keep[CHUNK_ROWS - 1:CHUNK_ROWS, :]
            n_parts.append(keep1 * n_cur[:, h * HEAD_DIM:(h + 1) * HEAD_DIM]
                           + jnp.sum(kw, axis=0, keepdims=True))
            m_cur = jnp.where(lane == h, m_new[CHUNK_ROWS - 1:CHUNK_ROWS, :], m_cur)
        n_cur = jnp.concatenate(n_parts, axis=1)
    for h in range(N_HEADS):
        c_scr[h] = c_list[h]
    n_scr[...] = n_cur
    m_scr[...] = m_cur

    @pl.when(t == pl.num_programs(1) - 1)
    def _():
        cout_ref[0] = c_scr[...]
        nout_ref[0] = n_scr[...]
        mout_ref[0] = m_scr[...]


def _mlstm_prompt(proj, smt, gb_row, gb_col, mng, c0, n0, m0, bsz, seq, tb):
    hq = N_HEADS * HEAD_DIM
    hv = N_HEADS * MLSTM_DV
    nt = seq // tb
    proj3 = proj.reshape(bsz, seq, proj.shape[1])
    small_blk = (2 * hq + 2 * hv) // SMALL_COLS
    state_c = pl.BlockSpec((1, N_HEADS, HEAD_DIM, MLSTM_DV), lambda b, t: (b, 0, 0, 0))
    state_n = pl.BlockSpec((1, 1, hq), lambda b, t: (b, 0, 0))
    state_m = pl.BlockSpec((1, 1, SMALL_COLS), lambda b, t: (b, 0, 0))
    return pl.pallas_call(
        functools.partial(_mlstm_prompt_kernel, tb=tb),
        grid=(bsz, nt),
        in_specs=[pl.BlockSpec((1, tb, hq), lambda b, t: (b, t, 0)),
                  pl.BlockSpec((1, tb, hq), lambda b, t: (b, t, 1)),
                  pl.BlockSpec((1, tb, hv), lambda b, t: (b, t, 2 * hq // hv)),
                  pl.BlockSpec((1, tb, hv), lambda b, t: (b, t, 2 * hq // hv + 1)),
                  pl.BlockSpec((1, tb, SMALL_COLS), lambda b, t: (b, t, small_blk)),
                  pl.BlockSpec((SMALL_ROWS, tb), lambda b, t: (0, b * nt + t)),
                  pl.BlockSpec((1, SMALL_COLS), lambda b, t: (0, 0)),
                  pl.BlockSpec((SMALL_ROWS, 1), lambda b, t: (0, 0)),
                  pl.BlockSpec((1, MLSTM_DV), lambda b, t: (0, 0)),
                  state_c, state_n, state_m],
        out_specs=[pl.BlockSpec((1, tb, hv), lambda b, t: (b, t, 0)), state_c, state_n, state_m],
        out_shape=[jax.ShapeDtypeStruct((bsz, seq, hv), F32),
                   jax.ShapeDtypeStruct(c0.shape, F32),
                   jax.ShapeDtypeStruct(n0.shape, F32),
                   jax.ShapeDtypeStruct(m0.shape, F32)],
        scratch_shapes=[pltpu.VMEM((N_HEADS, HEAD_DIM, MLSTM_DV), F32),
                        pltpu.VMEM((1, hq), F32),
                        pltpu.VMEM((1, SMALL_COLS), F32)],
        compiler_params=_cparams(("parallel", "arbitrary")),
        name="mlstm_prompt",
    )(proj3, proj3, proj3, proj3, proj3, smt, gb_row, gb_col, mng.reshape(1, MLSTM_DV), c0, n0, m0)


def _mlstm_sample_kernel(q_ref, k_ref, v_ref, op_ref, sm_ref, smt_ref, gbr_ref, gbc_ref, mng_ref,
                         c0_ref, n0_ref, m0_ref, o_ref, cout_ref, nout_ref, mout_ref, *, ns):
    cs = _chunk_structs(CHUNK_ROWS, ns)
    lane = lax.broadcasted_iota(jnp.int32, (1, SMALL_COLS), 1)
    c_list = [c0_ref[:, h].reshape(ns * HEAD_DIM, MLSTM_DV) for h in range(N_HEADS)]
    n0 = n0_ref[...]
    m0 = m0_ref[...]
    n_rows = _mm_hi(cs.seq_expand, jnp.concatenate(
        [n0, jnp.zeros((HEAD_DIM - ns, n0.shape[1]), F32)], axis=0))
    m_rows = _mm_hi(cs.seq_expand, jnp.concatenate(
        [m0, jnp.zeros((HEAD_DIM - ns, m0.shape[1]), F32)], axis=0))
    res = _mlstm_chunk(cs, q_ref[...], k_ref[...], v_ref[...], op_ref[...], sm_ref[...], smt_ref[0],
                       gbr_ref[...], gbc_ref[...], mng_ref[...], c_list, n_rows, m_rows)
    m_all = jnp.zeros((CHUNK_ROWS, SMALL_COLS), F32)
    for h, (out, c_new, keep, m_new, kw) in enumerate(res):
        o_ref[:, h * MLSTM_DV:(h + 1) * MLSTM_DV] = out
        cout_ref[:, h] = c_new.reshape(ns, HEAD_DIM, MLSTM_DV)
        keep_seq = _mm_hi(cs.seq_last, jnp.broadcast_to(keep, (CHUNK_ROWS, HEAD_DIM)))
        nout_ref[:, h * HEAD_DIM:(h + 1) * HEAD_DIM] = (
            keep_seq * n0[:, h * HEAD_DIM:(h + 1) * HEAD_DIM] + _mm_hi(cs.seq_sum, kw))
        m_all = jnp.where(lane == h, m_new, m_all)
    mout_ref[...] = _mm_hi(cs.seq_last, m_all)


def _mlstm_sample(proj, smt, gb_row, gb_col, mng, c0, n0, m0, lt):
    hq = N_HEADS * HEAD_DIM
    hv = N_HEADS * MLSTM_DV
    ns = CHUNK_ROWS // lt
    t = proj.shape[0]
    small_blk = (2 * hq + 2 * hv) // SMALL_COLS
    state_c = pl.BlockSpec((ns, N_HEADS, HEAD_DIM, MLSTM_DV), lambda c: (c, 0, 0, 0))
    state_n = pl.BlockSpec((ns, hq), lambda c: (c, 0))
    state_m = pl.BlockSpec((ns, SMALL_COLS), lambda c: (c, 0))
    return pl.pallas_call(
        functools.partial(_mlstm_sample_kernel, ns=ns),
        grid=(t // CHUNK_ROWS,),
        in_specs=[pl.BlockSpec((CHUNK_ROWS, hq), lambda c: (c, 0)),
                  pl.BlockSpec((CHUNK_ROWS, hq), lambda c: (c, 1)),
                  pl.BlockSpec((CHUNK_ROWS, hv), lambda c: (c, 2 * hq // hv)),
                  pl.BlockSpec((CHUNK_ROWS, hv), lambda c: (c, 2 * hq // hv + 1)),
                  pl.BlockSpec((CHUNK_ROWS, SMALL_COLS), lambda c: (c, small_blk)),
                  pl.BlockSpec((1, SMALL_ROWS, CHUNK_ROWS), lambda c: (c, 0, 0)),
                  pl.BlockSpec((1, SMALL_COLS), lambda c: (0, 0)),
                  pl.BlockSpec((SMALL_ROWS, 1), lambda c: (0, 0)),
                  pl.BlockSpec((1, MLSTM_DV), lambda c: (0, 0)),
                  state_c, state_n, state_m],
        out_specs=[pl.BlockSpec((CHUNK_ROWS, hv), lambda c: (c, 0)), state_c, state_n, state_m],
        out_shape=[jax.ShapeDtypeStruct((t, hv), F32),
                   jax.ShapeDtypeStruct(c0.shape, F32),
                   jax.ShapeDtypeStruct(n0.shape, F32),
                   jax.ShapeDtypeStruct(m0.shape, F32)],
        compiler_params=_cparams(("parallel",)),
        name="mlstm_sample",
    )(proj, proj, proj, proj, proj, smt, gb_row, gb_col, mng.reshape(1, MLSTM_DV), c0, n0, m0)


def _pad_cols(w, n):
    return jnp.pad(w, ((0, 0), (0, n - w.shape[1])))


def _split_in_weight(w, n_big, n_small):
    n_a = n_big
    small = w[:, n_a:n_a + n_small]
    main = jnp.concatenate([w[:, :n_a], w[:, n_a + n_small:], _pad_cols(small, SMALL_COLS)], axis=1)
    small_t = jnp.pad(small.T, ((0, SMALL_ROWS - n_small), (0, 0)))
    return main.astype(BF16), small_t.astype(BF16)


def _chunk_lanes(smt):
    rows, t = smt.shape
    return smt.reshape(rows, t // CHUNK_ROWS, CHUNK_ROWS).transpose(1, 0, 2)


def _row_col_params(vals):
    row = jnp.stack([jnp.pad(v, (0, SMALL_COLS - v.shape[0])) for v in vals]).astype(F32)
    col = jnp.stack([jnp.pad(v, (0, SMALL_ROWS - v.shape[0])) for v in vals], axis=1).astype(F32)
    return row, col


def _trunk(x, is_prompt, s_gdn, s_conv, s_c, s_n, s_m, norm_g, w_in_even, conv_w, a_log, dt_bias,
           gdn_norm_g, sgu_norm_g, sgu_w, sgu_b, w_out_even, w_in_odd, gate_b_odd, mlstm_norm_g,
           w_out_odd, w_ff1, w_ff2):
    bsz, seq, d = x.shape
    t = bsz * seq
    depth = norm_g.shape[0]
    hq = N_HEADS * HEAD_DIM
    cch = 3 * hq
    tm = 1024 if t % 1024 == 0 else 512
    tb = 256
    zeros_h = jnp.zeros((N_HEADS,), F32)
    x2 = x.reshape(t, d)
    new_gdn, new_conv, new_v, new_c, new_n, new_m = [], [], [], [], [], []
    for l in range(depth):
        gl = norm_g[l]
        if l % 2 == 0:
            e = l // 2
            w_main, w_small_t = _split_in_weight(w_in_even[e], cch + hq, 2 * N_HEADS)
            proj, smt = _proj_in(x2, gl[0], w_main, w_small_t, 512)
            prm_row, prm_col = _row_col_params([jnp.concatenate([zeros_h, jnp.exp(a_log[e])]),
                                                jnp.concatenate([zeros_h, dt_bias[e]])])
            if is_prompt:
                buf8 = jnp.pad(s_conv[e], ((0, 0), (8 - (CONV_W - 1), 0), (0, 0)))
                o_a, s_new, tail = _gdn_prompt(proj, smt, conv_w[e], prm_row, prm_col, gdn_norm_g[e],
                                               s_gdn[e], buf8, bsz, seq, tb)
                o_a = o_a.reshape(t, hq)
                buf_new = tail[:, 8 - (CONV_W - 1):, :]
                w_tiles = sgu_w[e]
                b_cols = _pad_cols(sgu_b[e].T, SMALL_COLS)
                o_b, vb = _sgu(proj, w_tiles, b_cols, sgu_norm_g[e], SGU_CHUNK, 1)
            else:
                qkv = proj[:, :cch].reshape(bsz, seq, cch)
                xcat = jnp.concatenate([s_conv[e], qkv], axis=1)
                buf_new = xcat[:, seq:, :]
                xc = _conv_sample(xcat.reshape(bsz, (CONV_W - 1 + seq) * cch), conv_w[e], seq, cch)
                o_a, s_new = _gdn_sample(xc.reshape(t, cch), proj, _chunk_lanes(smt), prm_row, prm_col,
                                         gdn_norm_g[e], s_gdn[e], seq)
                ns = CHUNK_ROWS // seq
                w_tiles = jnp.tile(sgu_w[e][:, :seq, :seq], (1, ns, ns))
                b_cols = _pad_cols(jnp.tile(sgu_b[e][:, :seq].T, (ns, 1)), SMALL_COLS)
                o_b, vb = _sgu(proj, w_tiles, b_cols, sgu_norm_g[e], CHUNK_ROWS, ns)
            new_gdn.append(s_new)
            new_conv.append(buf_new)
            new_v.append(vb.reshape(bsz, seq, hq))
            w_out = w_out_even[e].astype(BF16)
            x2 = _proj_out(x2, gl[1], [o_a, o_b], [w_out[:hq], w_out[hq:]], tm)
        else:
            o = l // 2
            w_main, w_small_t = _split_in_weight(w_in_odd[o], 2 * hq + N_HEADS * MLSTM_DV, 2 * N_HEADS)
            proj, smt = _proj_in(x2, gl[0], w_main, w_small_t, 512)
            gb_row, gb_col = _row_col_params([gate_b_odd[o]])
            if is_prompt:
                n0 = s_n[o].reshape(bsz, 1, hq)
                m0 = _pad_cols(s_m[o], SMALL_COLS).reshape(bsz, 1, SMALL_COLS)
                hh, c_new, n_new, m_new = _mlstm_prompt(proj, smt, gb_row, gb_col, mlstm_norm_g[o],
                                                        s_c[o], n0, m0, bsz, seq, tb)
                hh = hh.reshape(t, N_HEADS * MLSTM_DV)
                m_new = m_new.reshape(bsz, SMALL_COLS)
            else:
                n0 = s_n[o].reshape(bsz, hq)
                m0 = _pad_cols(s_m[o], SMALL_COLS)
                hh, c_new, n_new, m_new = _mlstm_sample(proj, _chunk_lanes(smt), gb_row, gb_col,
                                                        mlstm_norm_g[o],
                                                        s_c[o], n0, m0, seq)
            new_c.append(c_new)
            new_n.append(n_new.reshape(bsz, N_HEADS, HEAD_DIM))
            new_m.append(m_new[:, :N_HEADS])
            x2 = _proj_out(x2, gl[1], [hh], [w_out_odd[o].astype(BF16)], tm)
        x2 = _ffn(x2, gl[2], gl[3], w_ff1[l].astype(BF16), w_ff2[l].astype(BF16), tm, 1024)
    return (x2.reshape(bsz, seq, d), jnp.stack(new_gdn), jnp.stack(new_conv), jnp.stack(new_v),
            jnp.stack(new_c), jnp.stack(new_n), jnp.stack(new_m))


def kernel(x_prompt, x_sample, state_gdn, state_gdn_conv, state_mlstm_c, state_mlstm_n, state_mlstm_m,
           norm_g, w_in_even, conv_w, a_log, dt_bias, gdn_norm_g, sgu_norm_g, sgu_w, sgu_b, w_out_even,
           w_in_odd, gate_b_odd, mlstm_norm_g, w_out_odd, w_ff1, w_ff2):
    weights = (norm_g, w_in_even, conv_w, a_log, dt_bias, gdn_norm_g, sgu_norm_g, sgu_w, sgu_b,
               w_out_even, w_in_odd, gate_b_odd, mlstm_norm_g, w_out_odd, w_ff1, w_ff2)
    bp = x_prompt.shape[0]
    n_even, n_odd = state_gdn.shape[0], state_mlstm_c.shape[0]
    y_prompt, p_gdn, p_conv, _, p_c, p_n, p_m = _trunk(
        x_prompt, True,
        jnp.zeros((n_even, bp) + state_gdn.shape[2:], F32),
        jnp.zeros((n_even, bp) + state_gdn_conv.shape[2:], x_prompt.dtype),
        jnp.zeros((n_odd, bp) + state_mlstm_c.shape[2:], F32),
        jnp.zeros((n_odd, bp) + state_mlstm_n.shape[2:], F32),
        jnp.zeros((n_odd, bp) + state_mlstm_m.shape[2:], F32),
        *weights)
    y_sample, s_gdn, s_conv, s_v, s_c, s_n, s_m = _trunk(
        x_sample, False, state_gdn, state_gdn_conv, state_mlstm_c, state_mlstm_n, state_mlstm_m,
        *weights)
    return (y_prompt, y_sample, p_gdn, s_gdn, p_conv, s_conv, s_v, p_c, s_c, p_n, s_n, p_m, s_m)
```

```python
import functools
import math
from types import SimpleNamespace

import jax
import jax.numpy as jnp
from jax import lax
from jax.experimental import pallas as pl
from jax.experimental.pallas import tpu as pltpu

F32 = jnp.float32
BF16 = jnp.bfloat16
HIGHEST = lax.Precision.HIGHEST

EPS = 1e-6
N_HEADS = 4
HEAD_DIM = 128
MLSTM_DV = 256
CONV_W = 4
CHUNK_ROWS = 64
SGU_CHUNK = 128
SMALL_COLS = 128
SMALL_ROWS = 16
VMEM_LIMIT = 56 * 1024 * 1024


def _cparams(sem):
    return pltpu.CompilerParams(dimension_semantics=sem, vmem_limit_bytes=VMEM_LIMIT)


def _mm(a, b):
    return jnp.dot(a.astype(BF16), b.astype(BF16), preferred_element_type=F32)


def _mm_nt(a, b):
    return lax.dot_general(a.astype(BF16), b.astype(BF16), (((1,), (1,)), ((), ())),
                           preferred_element_type=F32)


def _mm_hi(a, b):
    return jnp.dot(a, b, preferred_element_type=F32, precision=HIGHEST)


def _rms(x, g):
    return x * lax.rsqrt(jnp.mean(x * x, axis=-1, keepdims=True) + EPS) * g


def _l2norm(x):
    return x * lax.rsqrt(jnp.sum(x * x, axis=-1, keepdims=True) + EPS)


def _softplus(x):
    return jnp.maximum(x, 0.0) + jnp.log1p(jnp.exp(-jnp.abs(x)))


def _sigmoid(x):
    return 1.0 / (1.0 + jnp.exp(-x))


def _silu(x):
    return x * _sigmoid(x)


def _gelu(x):
    return 0.5 * x * (1.0 + lax.erf(x * (2.0 ** -0.5)))


def _proj_in_kernel(x_ref, g_ref, w_ref, wst_ref, o_ref, ot_ref):
    h = _rms(x_ref[...], g_ref[...]).astype(BF16)
    o_ref[...] = jnp.dot(h, w_ref[...], preferred_element_type=F32)
    ot_ref[...] = lax.dot_general(wst_ref[...], h, (((1,), (1,)), ((), ())),
                                  preferred_element_type=F32)


def _proj_in(x2d, g, w_main, w_small_t, tm):
    t, d = x2d.shape
    n = w_main.shape[1]
    return pl.pallas_call(
        _proj_in_kernel,
        grid=(t // tm,),
        in_specs=[pl.BlockSpec((tm, d), lambda i: (i, 0)),
                  pl.BlockSpec((1, d), lambda i: (0, 0)),
                  pl.BlockSpec((d, n), lambda i: (0, 0)),
                  pl.BlockSpec((SMALL_ROWS, d), lambda i: (0, 0))],
        out_specs=[pl.BlockSpec((tm, n), lambda i: (i, 0)),
                   pl.BlockSpec((SMALL_ROWS, tm), lambda i: (0, i))],
        out_shape=[jax.ShapeDtypeStruct((t, n), F32),
                   jax.ShapeDtypeStruct((SMALL_ROWS, t), F32)],
        compiler_params=_cparams(("parallel",)),
        name="proj_in",
    )(x2d, g.reshape(1, d), w_main, w_small_t)


def _proj_out_kernel(*refs, n_act):
    x_ref, g_ref = refs[0], refs[1]
    acts = refs[2:2 + n_act]
    ws = refs[2 + n_act:2 + 2 * n_act]
    o_ref = refs[2 + 2 * n_act]
    acc = None
    for a_ref, w_ref in zip(acts, ws):
        p = jnp.dot(a_ref[...].astype(BF16), w_ref[...], preferred_element_type=F32)
        acc = p if acc is None else acc + p
    o_ref[...] = x_ref[...] + _rms(acc, g_ref[...])


def _proj_out(x2d, g, acts, ws, tm):
    t, d = x2d.shape
    n_act = len(acts)
    in_specs = [pl.BlockSpec((tm, d), lambda i: (i, 0)), pl.BlockSpec((1, d), lambda i: (0, 0))]
    in_specs += [pl.BlockSpec((tm, a.shape[1]), lambda i: (i, 0)) for a in acts]
    in_specs += [pl.BlockSpec(w.shape, lambda i: (0, 0)) for w in ws]
    return pl.pallas_call(
        functools.partial(_proj_out_kernel, n_act=n_act),
        grid=(t // tm,),
        in_specs=in_specs,
        out_specs=pl.BlockSpec((tm, d), lambda i: (i, 0)),
        out_shape=jax.ShapeDtypeStruct((t, d), F32),
        compiler_params=_cparams(("parallel",)),
        name="proj_out",
    )(x2d, g.reshape(1, d), *acts, *ws)


def _ffn_kernel(x_ref, g2_ref, g3_ref, w1_ref, w2_ref, o_ref, h_ref, acc_ref):
    k = pl.program_id(1)

    @pl.when(k == 0)
    def _():
        h_ref[...] = _rms(x_ref[...], g2_ref[...]).astype(BF16)
        acc_ref[...] = jnp.zeros_like(acc_ref)

    a = jnp.dot(h_ref[...], w1_ref[...], preferred_element_type=F32)
    a = jnp.square(jnp.maximum(a, 0.0)).astype(BF16)
    acc_ref[...] += jnp.dot(a, w2_ref[...], preferred_element_type=F32)

    @pl.when(k == pl.num_programs(1) - 1)
    def _():
        o_ref[...] = x_ref[...] + _rms(acc_ref[...], g3_ref[...])


def _ffn(x2d, g2, g3, w1, w2, tm, tf):
    t, d = x2d.shape
    ff = w1.shape[1]
    return pl.pallas_call(
        _ffn_kernel,
        grid=(t // tm, ff // tf),
        in_specs=[pl.BlockSpec((tm, d), lambda i, k: (i, 0)),
                  pl.BlockSpec((1, d), lambda i, k: (0, 0)),
                  pl.BlockSpec((1, d), lambda i, k: (0, 0)),
                  pl.BlockSpec((d, tf), lambda i, k: (0, k)),
                  pl.BlockSpec((tf, d), lambda i, k: (k, 0))],
        out_specs=pl.BlockSpec((tm, d), lambda i, k: (i, 0)),
        out_shape=jax.ShapeDtypeStruct((t, d), F32),
        scratch_shapes=[pltpu.VMEM((tm, d), BF16), pltpu.VMEM((tm, d), F32)],
        compiler_params=_cparams(("parallel", "arbitrary")),
        name="ffn",
    )(x2d, g2.reshape(1, d), g3.reshape(1, d), w1, w2)


def _ind(mask):
    return jnp.where(mask, 1.0, 0.0).astype(F32)


def _div_pow2(x, d):
    return jnp.right_shift(x, int(math.log2(d)))


def _chunk_structs(rows, ns):
    lt = rows // ns
    ri = lax.broadcasted_iota(jnp.int32, (rows, rows), 0)
    ci = lax.broadcasted_iota(jnp.int32, (rows, rows), 1)
    cs = SimpleNamespace(rows=rows, ns=ns, lt=lt)
    if ns == 1:
        cs.same = None
        cs.causal = ri >= ci
        cs.strict = ri > ci
        cs.upper = ri <= ci
    else:
        rseq = _div_pow2(ri, lt)
        cseq = _div_pow2(ci, lt)
        cs.same = rseq == cseq
        cs.causal = cs.same & (ri >= ci)
        cs.strict = cs.same & (ri > ci)
        cs.upper = cs.same & (ri <= ci)
        cs.last = _ind(ci == rseq * lt + (lt - 1))
        cs.last_t = _ind(ri == cseq * lt + (lt - 1))
        sr = _div_pow2(lax.broadcasted_iota(jnp.int32, (ns * HEAD_DIM, rows), 0), HEAD_DIM)
        sc = lax.broadcasted_iota(jnp.int32, (ns * HEAD_DIM, rows), 1)
        cs.stack_last = _ind(sc == sr * lt + (lt - 1))
        cs.stack_mask = _ind(_div_pow2(sc, lt) == sr)
        er = _div_pow2(lax.broadcasted_iota(jnp.int32, (rows, ns * HEAD_DIM), 0), lt)
        ec = _div_pow2(lax.broadcasted_iota(jnp.int32, (rows, ns * HEAD_DIM), 1), HEAD_DIM)
        cs.expand_mask = _ind(er == ec)
        kr = lax.broadcasted_iota(jnp.int32, (ns, rows), 0)
        kc = lax.broadcasted_iota(jnp.int32, (ns, rows), 1)
        cs.seq_last = _ind(kc == kr * lt + (lt - 1))
        cs.seq_sum = _ind(_div_pow2(kc, lt) == kr)
        pr = _div_pow2(lax.broadcasted_iota(jnp.int32, (rows, HEAD_DIM), 0), lt)
        pc = lax.broadcasted_iota(jnp.int32, (rows, HEAD_DIM), 1)
        cs.seq_expand = _ind(pr == pc)
    cs.lower_f = _ind(cs.causal)
    cs.upper_f = _ind(cs.upper)
    eye_r = lax.broadcasted_iota(jnp.int32, (HEAD_DIM, HEAD_DIM), 0)
    eye_c = lax.broadcasted_iota(jnp.int32, (HEAD_DIM, HEAD_DIM), 1)
    cs.eye = _ind(eye_r == eye_c)
    cs.eye_rows = _ind(ri == ci)
    return cs


def _last_rows(cs, x):
    if cs.ns == 1:
        return jnp.broadcast_to(x[cs.rows - 1:cs.rows, :], x.shape)
    return _mm_hi(cs.last, x)


def _last_lanes(cs, x):
    if cs.ns == 1:
        return jnp.broadcast_to(x[:, cs.rows - 1:cs.rows], x.shape)
    return _mm_hi(x, cs.last_t)


def _expand_lhs(cs, x):
    if cs.ns == 1:
        return x
    reps = x.shape[-2] // cs.rows
    mask = cs.expand_mask if reps == 1 else jnp.concatenate([cs.expand_mask] * reps, axis=0)
    return jnp.concatenate([x] * cs.ns, axis=-1) * mask


def _stack_t(cs, x):
    xt = _mm_nt(cs.eye, x)
    if cs.ns == 1:
        return xt
    return jnp.concatenate([xt] * cs.ns, axis=0) * cs.stack_mask


def _stack_scalar(cs, col):
    if cs.ns == 1:
        return col[cs.rows - 1:cs.rows, :]
    return _mm_hi(cs.stack_last, jnp.broadcast_to(col, (cs.rows, HEAD_DIM)))[:, 0:1]


_BATCH_NN = (((2,), (1,)), ((0,), (0,)))
_BATCH_NT = (((2,), (2,)), ((0,), (0,)))
_BATCH_TN = (((1,), (1,)), ((0,), (0,)))


def _bmm(a, b, dims=_BATCH_NN):
    return lax.dot_general(a.astype(BF16), b.astype(BF16), dims, preferred_element_type=F32)


def _split2(a):
    hi = a.astype(BF16)
    return hi, (a - hi.astype(F32)).astype(BF16)


def _bmm_split(ap, bp):
    ah, al = ap
    bh, bl = bp
    f = lambda x, y: lax.dot_general(x, y, _BATCH_NN, preferred_element_type=F32)
    return f(ah, bh) + (f(ah, bl) + f(al, bh))


def _unit_lower_inverse_off(cs, a):
    toff = -a
    p_parts = _split2(toff)
    for _ in range(int(math.log2(cs.lt)) - 1):
        p = _bmm_split(p_parts, p_parts)
        p_parts = _split2(p)
        toff = toff + p + _bmm_split(_split2(toff), p_parts)
    return toff


def _gdn_block(cs, nchunks, xc, gate, sm, smt, prm_row, prm_col, gng, s_heads, write_out):
    rows = cs.rows
    hq = N_HEADS * HEAD_DIM
    pairs = [(c, h) for c in range(nchunks) for h in range(N_HEADS)]
    rs = lambda c: slice(c * rows, (c + 1) * rows)

    def heads(x, col0):
        return jnp.stack([x[rs(c), col0 + h * HEAD_DIM:col0 + (h + 1) * HEAD_DIM] for c, h in pairs])

    def cols(x, lane0):
        return jnp.stack([x[rs(c), lane0 + h:lane0 + h + 1] for c, h in pairs])

    beta_all = _sigmoid(sm)
    g_c = -prm_row[0:1, :] * _softplus(sm + prm_row[1:2, :])
    g_r = -prm_col[:, 0:1] * _softplus(smt + prm_col[:, 1:2])
    gc_c = jnp.concatenate([_mm_hi(cs.lower_f, g_c[rs(c)]) for c in range(nchunks)], axis=0)
    gc_r = jnp.concatenate([_mm_hi(g_r[:, rs(c)], cs.upper_f) for c in range(nchunks)], axis=1)
    gl_c = jnp.concatenate([_last_rows(cs, gc_c[rs(c)]) for c in range(nchunks)], axis=0)
    egc = jnp.exp(gc_c)
    ekd = jnp.exp(gl_c - gc_c)

    q = _l2norm(heads(xc, 0)) * (HEAD_DIM ** -0.5)
    k = _l2norm(heads(xc, hq))
    v = heads(xc, 2 * hq)
    beta = cols(beta_all, 0)
    gcol = cols(gc_c, N_HEADS)
    eg = cols(egc, N_HEADS)
    grow = jnp.stack([gc_r[N_HEADS + h:N_HEADS + h + 1, rs(c)] for c, h in pairs])
    decay = jnp.where(cs.causal, jnp.exp(jnp.where(cs.causal, gcol - grow, 0.0)), 0.0)
    kb = k * beta
    a = jnp.where(cs.strict, _bmm(kb, k, _BATCH_NT) * decay, 0.0)
    toff = _unit_lower_inverse_off(cs, a)
    rhs = jnp.concatenate([v * beta, kb * eg], axis=-1)
    sol = rhs + _bmm_split(_split2(toff), _split2(rhs))
    u_val, w_k = sol[..., :HEAD_DIM], sol[..., HEAD_DIM:]
    qk = jnp.where(cs.causal, _bmm(q, k, _BATCH_NT) * decay, 0.0)
    q_dec = q * eg
    k_dec = k * cols(ekd, N_HEADS)

    s = s_heads
    for c in range(nchunks):
        hs = slice(c * N_HEADS, (c + 1) * N_HEADS)
        p = _bmm(_expand_lhs(cs, jnp.concatenate([w_k[hs], q_dec[hs]], axis=1)), s)
        v_new = u_val[hs] - p[:, :rows]
        o = p[:, rows:] + _bmm(qk[hs], v_new)
        gl_cols = [gl_c[rs(c), N_HEADS + h:N_HEADS + h + 1] for h in range(N_HEADS)]
        g_last = jnp.exp(jnp.stack([_stack_scalar(cs, col) for col in gl_cols]))
        if cs.ns == 1:
            upd = _bmm(k_dec[hs], v_new, _BATCH_TN)
        else:
            kt = _bmm(jnp.broadcast_to(cs.eye, (N_HEADS, HEAD_DIM, HEAD_DIM)), k_dec[hs], _BATCH_NT)
            upd = _bmm(jnp.concatenate([kt] * cs.ns, axis=1) * cs.stack_mask, v_new)
        s = s * g_last + upd
        for h in range(N_HEADS):
            write_out(c, h, _rms(o[h], gng) * _silu(gate[rs(c), h * HEAD_DIM:(h + 1) * HEAD_DIM]))
    return s


def _gdn_prompt_kernel(qkv_ref, gate_ref, sm_ref, smt_ref, cw_ref, prow_ref, pcol_ref, gng_ref,
                       s0_ref, buf_ref, o_ref, sout_ref, bufout_ref, s_scr, xp_scr, *, tb):
    t = pl.program_id(1)
    cs = _chunk_structs(CHUNK_ROWS, 1)

    @pl.when(t == 0)
    def _():
        s_scr[...] = s0_ref[0]
        xp_scr[0:8, :] = buf_ref[0]

    x = qkv_ref[0]
    xp_scr[8:8 + tb, :] = x
    y = (cw_ref[0:1, :] * xp_scr[5:5 + tb, :] + cw_ref[1:2, :] * xp_scr[6:6 + tb, :]
         + cw_ref[2:3, :] * xp_scr[7:7 + tb, :] + cw_ref[3:4, :] * x)
    tail = xp_scr[tb:tb + 8, :]
    xp_scr[0:8, :] = tail

    def write_out(c, h, o):
        o_ref[0, c * CHUNK_ROWS:(c + 1) * CHUNK_ROWS, h * HEAD_DIM:(h + 1) * HEAD_DIM] = o

    s_scr[...] = _gdn_block(cs, tb // CHUNK_ROWS, _silu(y), gate_ref[0], sm_ref[0], smt_ref[...],
                            prow_ref[...], pcol_ref[...], gng_ref[...], s_scr[...], write_out)

    @pl.when(t == pl.num_programs(1) - 1)
    def _():
        sout_ref[0] = s_scr[...]
        bufout_ref[0] = tail


def _gdn_prompt(proj, smt, conv_w, prm_row, prm_col, gng, s0, buf8, bsz, seq, tb):
    hq = N_HEADS * HEAD_DIM
    cch = 3 * hq
    nt = seq // tb
    proj3 = proj.reshape(bsz, seq, proj.shape[1])
    small_blk = (cch + 3 * hq) // SMALL_COLS
    return pl.pallas_call(
        functools.partial(_gdn_prompt_kernel, tb=tb),
        grid=(bsz, nt),
        in_specs=[pl.BlockSpec((1, tb, cch), lambda b, t: (b, t, 0)),
                  pl.BlockSpec((1, tb, hq), lambda b, t: (b, t, cch // hq)),
                  pl.BlockSpec((1, tb, SMALL_COLS), lambda b, t: (b, t, small_blk)),
                  pl.BlockSpec((SMALL_ROWS, tb), lambda b, t: (0, b * nt + t)),
                  pl.BlockSpec((CONV_W, cch), lambda b, t: (0, 0)),
                  pl.BlockSpec((2, SMALL_COLS), lambda b, t: (0, 0)),
                  pl.BlockSpec((SMALL_ROWS, 2), lambda b, t: (0, 0)),
                  pl.BlockSpec((1, HEAD_DIM), lambda b, t: (0, 0)),
                  pl.BlockSpec((1, N_HEADS, HEAD_DIM, HEAD_DIM), lambda b, t: (b, 0, 0, 0)),
                  pl.BlockSpec((1, 8, cch), lambda b, t: (b, 0, 0))],
        out_specs=[pl.BlockSpec((1, tb, hq), lambda b, t: (b, t, 0)),
                   pl.BlockSpec((1, N_HEADS, HEAD_DIM, HEAD_DIM), lambda b, t: (b, 0, 0, 0)),
                   pl.BlockSpec((1, 8, cch), lambda b, t: (b, 0, 0))],
        out_shape=[jax.ShapeDtypeStruct((bsz, seq, hq), F32),
                   jax.ShapeDtypeStruct((bsz, N_HEADS, HEAD_DIM, HEAD_DIM), F32),
                   jax.ShapeDtypeStruct((bsz, 8, cch), F32)],
        scratch_shapes=[pltpu.VMEM((N_HEADS, HEAD_DIM, HEAD_DIM), F32),
                        pltpu.VMEM((tb + 8, cch), F32)],
        compiler_params=_cparams(("parallel", "arbitrary")),
        name="gdn_prompt",
    )(proj3, proj3, proj3, smt, conv_w, prm_row, prm_col, gng.reshape(1, HEAD_DIM), s0, buf8)


def _conv_sample_kernel(xcat_ref, cw_ref, o_ref, *, lt, cch):
    for t in range(lt):
        y = None
        for j in range(CONV_W):
            term = cw_ref[j:j + 1, :] * xcat_ref[:, (t + j) * cch:(t + j + 1) * cch]
            y = term if y is None else y + term
        o_ref[:, t * cch:(t + 1) * cch] = _silu(y)


def _conv_sample(xcat, conv_w, lt, cch):
    nseq = xcat.shape[0]
    return pl.pallas_call(
        functools.partial(_conv_sample_kernel, lt=lt, cch=cch),
        grid=(1,),
        in_specs=[pl.BlockSpec(xcat.shape, lambda i: (0, 0)),
                  pl.BlockSpec((CONV_W, cch), lambda i: (0, 0))],
        out_specs=pl.BlockSpec((nseq, lt * cch), lambda i: (0, 0)),
        out_shape=jax.ShapeDtypeStruct((nseq, lt * cch), F32),
        compiler_params=_cparams(("arbitrary",)),
        name="conv_sample",
    )(xcat, conv_w)


def _gdn_sample_kernel(xc_ref, gate_ref, sm_ref, smt_ref, prow_ref, pcol_ref, gng_ref, s0_ref,
                       o_ref, sout_ref, *, ns):
    cs = _chunk_structs(CHUNK_ROWS, ns)
    s_heads = jnp.stack([s0_ref[:, h].reshape(ns * HEAD_DIM, HEAD_DIM) for h in range(N_HEADS)])

    def write_out(c, h, o):
        o_ref[:, h * HEAD_DIM:(h + 1) * HEAD_DIM] = o

    s_new = _gdn_block(cs, 1, xc_ref[...], gate_ref[...], sm_ref[...], smt_ref[0],
                       prow_ref[...], pcol_ref[...], gng_ref[...], s_heads, write_out)
    for h in range(N_HEADS):
        sout_ref[:, h] = s_new[h].reshape(ns, HEAD_DIM, HEAD_DIM)


def _gdn_sample(xc, proj, smt, prm_row, prm_col, gng, s0, lt):
    hq = N_HEADS * HEAD_DIM
    cch = 3 * hq
    ns = CHUNK_ROWS // lt
    nchunk = xc.shape[0] // CHUNK_ROWS
    small_blk = (cch + 3 * hq) // SMALL_COLS
    return pl.pallas_call(
        functools.partial(_gdn_sample_kernel, ns=ns),
        grid=(nchunk,),
        in_specs=[pl.BlockSpec((CHUNK_ROWS, cch), lambda c: (c, 0)),
                  pl.BlockSpec((CHUNK_ROWS, hq), lambda c: (c, cch // hq)),
                  pl.BlockSpec((CHUNK_ROWS, SMALL_COLS), lambda c: (c, small_blk)),
                  pl.BlockSpec((1, SMALL_ROWS, CHUNK_ROWS), lambda c: (c, 0, 0)),
                  pl.BlockSpec((2, SMALL_COLS), lambda c: (0, 0)),
                  pl.BlockSpec((SMALL_ROWS, 2), lambda c: (0, 0)),
                  pl.BlockSpec((1, HEAD_DIM), lambda c: (0, 0)),
                  pl.BlockSpec((ns, N_HEADS, HEAD_DIM, HEAD_DIM), lambda c: (c, 0, 0, 0))],
        out_specs=[pl.BlockSpec((CHUNK_ROWS, hq), lambda c: (c, 0)),
                   pl.BlockSpec((ns, N_HEADS, HEAD_DIM, HEAD_DIM), lambda c: (c, 0, 0, 0))],
        out_shape=[jax.ShapeDtypeStruct((xc.shape[0], hq), F32),
                   jax.ShapeDtypeStruct(s0.shape, F32)],
        compiler_params=_cparams(("parallel",)),
        name="gdn_sample",
    )(xc, proj, proj, smt, prm_row, prm_col, gng.reshape(1, HEAD_DIM), s0)


def _sgu_kernel(u_ref, v_ref, w_ref, b_ref, g_ref, o_ref, vb_ref, *, ns):
    rows = u_ref.shape[0]
    lt = rows // ns
    ri = lax.broadcasted_iota(jnp.int32, (rows, rows), 0)
    ci = lax.broadcasted_iota(jnp.int32, (rows, rows), 1)
    keep = ri >= ci
    if ns > 1:
        keep = keep & (_div_pow2(ri, lt) == _div_pow2(ci, lt))
    for g in range(N_HEADS):
        lo = g * HEAD_DIM
        u = _gelu(u_ref[:, lo:lo + HEAD_DIM])
        vb = _rms(_gelu(v_ref[:, lo:lo + HEAD_DIM]), g_ref[g:g + 1, :])
        w = jnp.where(keep, w_ref[g], 0.0)
        z = _mm(w, vb) + b_ref[:, g:g + 1]
        o_ref[:, lo:lo + HEAD_DIM] = u * z
        vb_ref[:, lo:lo + HEAD_DIM] = vb


def _sgu(proj, w_tiles, b_cols, norm_g, rows, ns):
    hq = N_HEADS * HEAD_DIM
    t = proj.shape[0]
    u_blk = (3 * hq + hq) // hq
    return pl.pallas_call(
        functools.partial(_sgu_kernel, ns=ns),
        grid=(t // rows,),
        in_specs=[pl.BlockSpec((rows, hq), lambda i: (i, u_blk)),
                  pl.BlockSpec((rows, hq), lambda i: (i, u_blk + 1)),
                  pl.BlockSpec((N_HEADS, rows, rows), lambda i: (0, 0, 0)),
                  pl.BlockSpec((rows, SMALL_COLS), lambda i: (0, 0)),
                  pl.BlockSpec((N_HEADS, HEAD_DIM), lambda i: (0, 0))],
        out_specs=[pl.BlockSpec((rows, hq), lambda i: (i, 0)),
                   pl.BlockSpec((rows, hq), lambda i: (i, 0))],
        out_shape=[jax.ShapeDtypeStruct((t, hq), F32), jax.ShapeDtypeStruct((t, hq), F32)],
        compiler_params=_cparams(("parallel",)),
        name="sgu",
    )(proj, proj, w_tiles, b_cols, norm_g)


def _logsigmoid(x):
    return jnp.minimum(x, 0.0) - jnp.log1p(jnp.exp(-jnp.abs(x)))


def _mlstm_chunk(cs, q_all, k_all, v_all, opre, sm, smt, gb_row, gb_col, mng, c_list, n_rows, m_rows):
    neg_inf = -jnp.inf
    pre_c = sm + gb_row
    pre_r = smt + gb_col
    b_c = _mm_hi(cs.lower_f, _logsigmoid(pre_c))
    b_r = _mm_hi(_logsigmoid(pre_r), cs.upper_f)
    bl_c = _last_rows(cs, b_c)
    bl_r = _last_lanes(cs, b_r)
    res = []
    for h in range(N_HEADS):
        lo = h * HEAD_DIM
        vo = h * MLSTM_DV
        q = q_all[:, lo:lo + HEAD_DIM] * (HEAD_DIM ** -0.5)
        k = k_all[:, lo:lo + HEAD_DIM]
        v = v_all[:, vo:vo + MLSTM_DV]
        bcol = b_c[:, N_HEADS + h:N_HEADS + h + 1]
        brow = b_r[N_HEADS + h:N_HEADS + h + 1, :]
        icol = pre_c[:, h:h + 1]
        irow = pre_r[h:h + 1, :]
        blcol = bl_c[:, N_HEADS + h:N_HEADS + h + 1]
        blrow = bl_r[N_HEADS + h:N_HEADS + h + 1, :]
        mrow = m_rows[:, h:h + 1]
        inter = bcol + mrow
        dmat = jnp.where(cs.causal, bcol - brow + irow, neg_inf)
        m_t = jnp.maximum(inter, jnp.max(dmat, axis=-1, keepdims=True))
        w_intra = jnp.exp(dmat - m_t)
        w_inter = jnp.exp(inter - m_t)
        s = _mm_nt(q, k) * w_intra
        c_old = c_list[h]
        num = w_inter * _mm(_expand_lhs(cs, q), c_old) + _mm(s, v)
        qn = jnp.sum(q * n_rows[:, lo:lo + HEAD_DIM], axis=-1, keepdims=True)
        den = w_inter * qn + jnp.sum(s, axis=-1, keepdims=True)
        hval = num / jnp.maximum(jnp.abs(den), jnp.exp(-m_t))
        out = _rms(hval, mng) * _sigmoid(opre[:, vo:vo + MLSTM_DV])
        logw_row = blrow - brow + irow
        if cs.ns == 1:
            seq_max = jnp.max(logw_row, axis=-1, keepdims=True)
        else:
            seq_max = jnp.max(jnp.where(cs.same, logw_row, neg_inf), axis=-1, keepdims=True)
        m_new = jnp.maximum(blcol + mrow, seq_max)
        keep = jnp.exp(blcol + mrow - m_new)
        wk = jnp.exp(blcol - bcol + icol - m_new)
        kw = k * wk
        c_new = c_old * _stack_scalar(cs, keep) + _mm(_stack_t(cs, kw), v)
        res.append((out, c_new, keep, m_new, kw))
    return res


def _mlstm_prompt_block(cs, nchunks, q_all, k_all, v_all, opre, sm, smt, gb_row, gb_col, mng,
                        c_heads, n_heads, m_heads, write_out):
    rows = cs.rows
    pairs = [(c, h) for c in range(nchunks) for h in range(N_HEADS)]
    rs = lambda c: slice(c * rows, (c + 1) * rows)

    def heads(x, width):
        return jnp.stack([x[rs(c), h * width:(h + 1) * width] for c, h in pairs])

    def cols(x, lane0):
        return jnp.stack([x[rs(c), lane0 + h:lane0 + h + 1] for c, h in pairs])

    def lanes(x, row0):
        return jnp.stack([x[row0 + h:row0 + h + 1, rs(c)] for c, h in pairs])

    pre_c = sm + gb_row
    pre_r = smt + gb_col
    lf_c = _logsigmoid(pre_c)
    lf_r = _logsigmoid(pre_r)
    b_c = jnp.concatenate([_mm_hi(cs.lower_f, lf_c[rs(c)]) for c in range(nchunks)], axis=0)
    b_r = jnp.concatenate([_mm_hi(lf_r[:, rs(c)], cs.upper_f) for c in range(nchunks)], axis=1)

    q = heads(q_all, HEAD_DIM) * (HEAD_DIM ** -0.5)
    k = heads(k_all, HEAD_DIM)
    v = heads(v_all, MLSTM_DV)
    bcol = cols(b_c, N_HEADS)
    icol = cols(pre_c, 0)
    brow = lanes(b_r, N_HEADS)
    irow = lanes(pre_r, 0)
    blast = bcol[:, rows - 1:rows, :]
    dmat = jnp.where(cs.causal, bcol - brow + irow, -jnp.inf)
    rmax = jnp.max(dmat, axis=-1, keepdims=True)
    sp = _bmm(q, k, _BATCH_NT) * jnp.exp(dmat - rmax)
    sv = _bmm(sp, v)
    ssum = jnp.sum(sp, axis=-1, keepdims=True)
    lmax = jnp.max(blast - brow + irow, axis=-1, keepdims=True)
    kwp = k * jnp.exp(blast - bcol + icol - lmax)
    upd = _bmm(kwp, v, _BATCH_TN)
    nsum = jnp.sum(kwp, axis=1, keepdims=True)

    cst, n, m = c_heads, n_heads, m_heads
    for c in range(nchunks):
        hs = slice(c * N_HEADS, (c + 1) * N_HEADS)
        inter = bcol[hs] + m
        m_t = jnp.maximum(inter, rmax[hs])
        w_inter = jnp.exp(inter - m_t)
        w_intra = jnp.exp(rmax[hs] - m_t)
        num = w_inter * _bmm(q[hs], cst) + w_intra * sv[hs]
        den = (w_inter * jnp.sum(q[hs] * n, axis=-1, keepdims=True) + w_intra * ssum[hs])
        hval = num / jnp.maximum(jnp.abs(den), jnp.exp(-m_t))
        for h in range(N_HEADS):
            write_out(c, h, _rms(hval[h], mng) * _sigmoid(opre[rs(c), h * MLSTM_DV:(h + 1) * MLSTM_DV]))
        m_new = jnp.maximum(blast[hs] + m, lmax[hs])
        keep = jnp.exp(blast[hs] + m - m_new)
        fresh = jnp.exp(lmax[hs] - m_new)
        cst = keep * cst + fresh * upd[hs]
        n = keep * n + fresh * nsum[hs]
        m = m_new
    return cst, n, m


def _mlstm_prompt_kernel(q_ref, k_ref, v_ref, op_ref, sm_ref, smt_ref, gbr_ref, gbc_ref, mng_ref,
                         c0_ref, n0_ref, m0_ref, o_ref, cout_ref, nout_ref, mout_ref,
                         c_scr, n_scr, m_scr, *, tb):
    t = pl.program_id(1)
    cs = _chunk_structs(CHUNK_ROWS, 1)
    hq = N_HEADS * HEAD_DIM
    lane = lax.broadcasted_iota(jnp.int32, (1, SMALL_COLS), 1)

    @pl.when(t == 0)
    def _():
        c_scr[...] = c0_ref[0]
        n_scr[...] = n0_ref[0]
        m_scr[...] = m0_ref[0]

    n_cur = n_scr[...]
    m_cur = m_scr[...]
    n_heads = jnp.stack([n_cur[:, h * HEAD_DIM:(h + 1) * HEAD_DIM] for h in range(N_HEADS)])
    m_heads = jnp.stack([m_cur[:, h:h + 1] for h in range(N_HEADS)])

    def write_out(c, h, o):
        o_ref[0, c * CHUNK_ROWS:(c + 1) * CHUNK_ROWS, h * MLSTM_DV:(h + 1) * MLSTM_DV] = o

    c_new, n_heads, m_heads = _mlstm_prompt_block(
        cs, tb // CHUNK_ROWS, q_ref[0], k_ref[0], v_ref[0], op_ref[0], sm_ref[0], smt_ref[...],
        gbr_ref[...], gbc_ref[...], mng_ref[...], c_scr[...], n_heads, m_heads, write_out)
    c_scr[...] = c_new
    n_scr[...] = jnp.concatenate([n_heads[h] for h in range(N_HEADS)], axis=1)
    for h in range(N_HEADS):
        m_cur = jnp.where(lane == h, m_heads[h], m_cur)
    m_scr[...] = m_cur

    @pl.when(t == pl.num_programs(1) - 1)
    def _():
        cout_ref[0] = c_scr[...]
        nout_ref[0] = n_scr[...]
        mout_ref[0] = m_scr[...]


def _mlstm_prompt(proj, smt, gb_row, gb_col, mng, c0, n0, m0, bsz, seq, tb):
    hq = N_HEADS * HEAD_DIM
    hv = N_HEADS * MLSTM_DV
    nt = seq // tb
    proj3 = proj.reshape(bsz, seq, proj.shape[1])
    small_blk = (2 * hq + 2 * hv) // SMALL_COLS
    state_c = pl.BlockSpec((1, N_HEADS, HEAD_DIM, MLSTM_DV), lambda b, t: (b, 0, 0, 0))
    state_n = pl.BlockSpec((1, 1, hq), lambda b, t: (b, 0, 0))
    state_m = pl.BlockSpec((1, 1, SMALL_COLS), lambda b, t: (b, 0, 0))
    return pl.pallas_call(
        functools.partial(_mlstm_prompt_kernel, tb=tb),
        grid=(bsz, nt),
        in_specs=[pl.BlockSpec((1, tb, hq), lambda b, t: (b, t, 0)),
                  pl.BlockSpec((1, tb, hq), lambda b, t: (b, t, 1)),
                  pl.BlockSpec((1, tb, hv), lambda b, t: (b, t, 2 * hq // hv)),
                  pl.BlockSpec((1, tb, hv), lambda b, t: (b, t, 2 * hq // hv + 1)),
                  pl.BlockSpec((1, tb, SMALL_COLS), lambda b, t: (b, t, small_blk)),
                  pl.BlockSpec((SMALL_ROWS, tb), lambda b, t: (0, b * nt + t)),
                  pl.BlockSpec((1, SMALL_COLS), lambda b, t: (0, 0)),
                  pl.BlockSpec((SMALL_ROWS, 1), lambda b, t: (0, 0)),
                  pl.BlockSpec((1, MLSTM_DV), lambda b, t: (0, 0)),
                  state_c, state_n, state_m],
        out_specs=[pl.BlockSpec((1, tb, hv), lambda b, t: (b, t, 0)), state_c, state_n, state_m],
        out_shape=[jax.ShapeDtypeStruct((bsz, seq, hv), F32),
                   jax.ShapeDtypeStruct(c0.shape, F32),
                   jax.ShapeDtypeStruct(n0.shape, F32),
                   jax.ShapeDtypeStruct(m0.shape, F32)],
        scratch_shapes=[pltpu.VMEM((N_HEADS, HEAD_DIM, MLSTM_DV), F32),
                        pltpu.VMEM((1, hq), F32),
                        pltpu.VMEM((1, SMALL_COLS), F32)],
        compiler_params=_cparams(("parallel", "arbitrary")),
        name="mlstm_prompt",
    )(proj3, proj3, proj3, proj3, proj3, smt, gb_row, gb_col, mng.reshape(1, MLSTM_DV), c0, n0, m0)


def _mlstm_sample_kernel(q_ref, k_ref, v_ref, op_ref, sm_ref, smt_ref, gbr_ref, gbc_ref, mng_ref,
                         c0_ref, n0_ref, m0_ref, o_ref, cout_ref, nout_ref, mout_ref, *, ns):
    cs = _chunk_structs(CHUNK_ROWS, ns)
    lane = lax.broadcasted_iota(jnp.int32, (1, SMALL_COLS), 1)
    c_list = [c0_ref[:, h].reshape(ns * HEAD_DIM, MLSTM_DV) for h in range(N_HEADS)]
    n0 = n0_ref[...]
    m0 = m0_ref[...]
    n_rows = _mm_hi(cs.seq_expand, jnp.concatenate(
        [n0, jnp.zeros((HEAD_DIM - ns, n0.shape[1]), F32)], axis=0))
    m_rows = _mm_hi(cs.seq_expand, jnp.concatenate(
        [m0, jnp.zeros((HEAD_DIM - ns, m0.shape[1]), F32)], axis=0))
    res = _mlstm_chunk(cs, q_ref[...], k_ref[...], v_ref[...], op_ref[...], sm_ref[...], smt_ref[0],
                       gbr_ref[...], gbc_ref[...], mng_ref[...], c_list, n_rows, m_rows)
    m_all = jnp.zeros((CHUNK_ROWS, SMALL_COLS), F32)
    for h, (out, c_new, keep, m_new, kw) in enumerate(res):
        o_ref[:, h * MLSTM_DV:(h + 1) * MLSTM_DV] = out
        cout_ref[:, h] = c_new.reshape(ns, HEAD_DIM, MLSTM_DV)
        keep_seq = _mm_hi(cs.seq_last, jnp.broadcast_to(keep, (CHUNK_ROWS, HEAD_DIM)))
        nout_ref[:, h * HEAD_DIM:(h + 1) * HEAD_DIM] = (
            keep_seq * n0[:, h * HEAD_DIM:(h + 1) * HEAD_DIM] + _mm_hi(cs.seq_sum, kw))
        m_all = jnp.where(lane == h, m_new, m_all)
    mout_ref[...] = _mm_hi(cs.seq_last, m_all)


def _mlstm_sample(proj, smt, gb_row, gb_col, mng, c0, n0, m0, lt):
    hq = N_HEADS * HEAD_DIM
    hv = N_HEADS * MLSTM_DV
    ns = CHUNK_ROWS // lt
    t = proj.shape[0]
    small_blk = (2 * hq + 2 * hv) // SMALL_COLS
    state_c = pl.BlockSpec((ns, N_HEADS, HEAD_DIM, MLSTM_DV), lambda c: (c, 0, 0, 0))
    state_n = pl.BlockSpec((ns, hq), lambda c: (c, 0))
    state_m = pl.BlockSpec((ns, SMALL_COLS), lambda c: (c, 0))
    return pl.pallas_call(
        functools.partial(_mlstm_sample_kernel, ns=ns),
        grid=(t // CHUNK_ROWS,),
        in_specs=[pl.BlockSpec((CHUNK_ROWS, hq), lambda c: (c, 0)),
                  pl.BlockSpec((CHUNK_ROWS, hq), lambda c: (c, 1)),
                  pl.BlockSpec((CHUNK_ROWS, hv), lambda c: (c, 2 * hq // hv)),
                  pl.BlockSpec((CHUNK_ROWS, hv), lambda c: (c, 2 * hq // hv + 1)),
                  pl.BlockSpec((CHUNK_ROWS, SMALL_COLS), lambda c: (c, small_blk)),
                  pl.BlockSpec((1, SMALL_ROWS, CHUNK_ROWS), lambda c: (c, 0, 0)),
                  pl.BlockSpec((1, SMALL_COLS), lambda c: (0, 0)),
                  pl.BlockSpec((SMALL_ROWS, 1), lambda c: (0, 0)),
                  pl.BlockSpec((1, MLSTM_DV), lambda c: (0, 0)),
                  state_c, state_n, state_m],
        out_specs=[pl.BlockSpec((CHUNK_ROWS, hv), lambda c: (c, 0)), state_c, state_n, state_m],
        out_shape=[jax.ShapeDtypeStruct((t, hv), F32),
                   jax.ShapeDtypeStruct(c0.shape, F32),
                   jax.ShapeDtypeStruct(n0.shape, F32),
                   jax.ShapeDtypeStruct(m0.shape, F32)],
        compiler_params=_cparams(("parallel",)),
        name="mlstm_sample",
    )(proj, proj, proj, proj, proj, smt, gb_row, gb_col, mng.reshape(1, MLSTM_DV), c0, n0, m0)


def _pad_cols(w, n):
    return jnp.pad(w, ((0, 0), (0, n - w.shape[1])))


def _split_in_weight(w, n_big, n_small):
    n_a = n_big
    small = w[:, n_a:n_a + n_small]
    main = jnp.concatenate([w[:, :n_a], w[:, n_a + n_small:], _pad_cols(small, SMALL_COLS)], axis=1)
    small_t = jnp.pad(small.T, ((0, SMALL_ROWS - n_small), (0, 0)))
    return main.astype(BF16), small_t.astype(BF16)


def _chunk_lanes(smt):
    rows, t = smt.shape
    return smt.reshape(rows, t // CHUNK_ROWS, CHUNK_ROWS).transpose(1, 0, 2)


def _row_col_params(vals):
    row = jnp.stack([jnp.pad(v, (0, SMALL_COLS - v.shape[0])) for v in vals]).astype(F32)
    col = jnp.stack([jnp.pad(v, (0, SMALL_ROWS - v.shape[0])) for v in vals], axis=1).astype(F32)
    return row, col


def _trunk(x, is_prompt, s_gdn, s_conv, s_c, s_n, s_m, norm_g, w_in_even, conv_w, a_log, dt_bias,
           gdn_norm_g, sgu_norm_g, sgu_w, sgu_b, w_out_even, w_in_odd, gate_b_odd, mlstm_norm_g,
           w_out_odd, w_ff1, w_ff2):
    bsz, seq, d = x.shape
    t = bsz * seq
    depth = norm_g.shape[0]
    hq = N_HEADS * HEAD_DIM
    cch = 3 * hq
    tm = 1024 if t % 1024 == 0 else 512
    tb = 256
    zeros_h = jnp.zeros((N_HEADS,), F32)
    x2 = x.reshape(t, d)
    new_gdn, new_conv, new_v, new_c, new_n, new_m = [], [], [], [], [], []
    for l in range(depth):
        gl = norm_g[l]
        if l % 2 == 0:
            e = l // 2
            w_main, w_small_t = _split_in_weight(w_in_even[e], cch + hq, 2 * N_HEADS)
            proj, smt = _proj_in(x2, gl[0], w_main, w_small_t, 512)
            prm_row, prm_col = _row_col_params([jnp.concatenate([zeros_h, jnp.exp(a_log[e])]),
                                                jnp.concatenate([zeros_h, dt_bias[e]])])
            if is_prompt:
                buf8 = jnp.pad(s_conv[e], ((0, 0), (8 - (CONV_W - 1), 0), (0, 0)))
                o_a, s_new, tail = _gdn_prompt(proj, smt, conv_w[e], prm_row, prm_col, gdn_norm_g[e],
                                               s_gdn[e], buf8, bsz, seq, tb)
                o_a = o_a.reshape(t, hq)
                buf_new = tail[:, 8 - (CONV_W - 1):, :]
                w_tiles = sgu_w[e]
                b_cols = _pad_cols(sgu_b[e].T, SMALL_COLS)
                o_b, vb = _sgu(proj, w_tiles, b_cols, sgu_norm_g[e], SGU_CHUNK, 1)
            else:
                qkv = proj[:, :cch].reshape(bsz, seq, cch)
                xcat = jnp.concatenate([s_conv[e], qkv], axis=1)
                buf_new = xcat[:, seq:, :]
                xc = _conv_sample(xcat.reshape(bsz, (CONV_W - 1 + seq) * cch), conv_w[e], seq, cch)
                o_a, s_new = _gdn_sample(xc.reshape(t, cch), proj, _chunk_lanes(smt), prm_row, prm_col,
                                         gdn_norm_g[e], s_gdn[e], seq)
                ns = CHUNK_ROWS // seq
                w_tiles = jnp.tile(sgu_w[e][:, :seq, :seq], (1, ns, ns))
                b_cols = _pad_cols(jnp.tile(sgu_b[e][:, :seq].T, (ns, 1)), SMALL_COLS)
                o_b, vb = _sgu(proj, w_tiles, b_cols, sgu_norm_g[e], CHUNK_ROWS, ns)
            new_gdn.append(s_new)
            new_conv.append(buf_new)
            new_v.append(vb.reshape(bsz, seq, hq))
            w_out = w_out_even[e].astype(BF16)
            x2 = _proj_out(x2, gl[1], [o_a, o_b], [w_out[:hq], w_out[hq:]], tm)
        else:
            o = l // 2
            w_main, w_small_t = _split_in_weight(w_in_odd[o], 2 * hq + N_HEADS * MLSTM_DV, 2 * N_HEADS)
            proj, smt = _proj_in(x2, gl[0], w_main, w_small_t, 512)
            gb_row, gb_col = _row_col_params([gate_b_odd[o]])
            if is_prompt:
                n0 = s_n[o].reshape(bsz, 1, hq)
                m0 = _pad_cols(s_m[o], SMALL_COLS).reshape(bsz, 1, SMALL_COLS)
                hh, c_new, n_new, m_new = _mlstm_prompt(proj, smt, gb_row, gb_col, mlstm_norm_g[o],
                                                        s_c[o], n0, m0, bsz, seq, tb)
                hh = hh.reshape(t, N_HEADS * MLSTM_DV)
                m_new = m_new.reshape(bsz, SMALL_COLS)
            else:
                n0 = s_n[o].reshape(bsz, hq)
                m0 = _pad_cols(s_m[o], SMALL_COLS)
                hh, c_new, n_new, m_new = _mlstm_sample(proj, _chunk_lanes(smt), gb_row, gb_col,
                                                        mlstm_norm_g[o],
                                                        s_c[o], n0, m0, seq)
            new_c.append(c_new)
            new_n.append(n_new.reshape(bsz, N_HEADS, HEAD_DIM))
            new_m.append(m_new[:, :N_HEADS])
            x2 = _proj_out(x2, gl[1], [hh], [w_out_odd[o].astype(BF16)], tm)
        x2 = _ffn(x2, gl[2], gl[3], w_ff1[l].astype(BF16), w_ff2[l].astype(BF16), tm, 1024)
    return (x2.reshape(bsz, seq, d), jnp.stack(new_gdn), jnp.stack(new_conv), jnp.stack(new_v),
            jnp.stack(new_c), jnp.stack(new_n), jnp.stack(new_m))


def kernel(x_prompt, x_sample, state_gdn, state_gdn_conv, state_mlstm_c, state_mlstm_n, state_mlstm_m,
           norm_g, w_in_even, conv_w, a_log, dt_bias, gdn_norm_g, sgu_norm_g, sgu_w, sgu_b, w_out_even,
           w_in_odd, gate_b_odd, mlstm_norm_g, w_out_odd, w_ff1, w_ff2):
    weights = (norm_g, w_in_even, conv_w, a_log, dt_bias, gdn_norm_g, sgu_norm_g, sgu_w, sgu_b,
               w_out_even, w_in_odd, gate_b_odd, mlstm_norm_g, w_out_odd, w_ff1, w_ff2)
    bp = x_prompt.shape[0]
    n_even, n_odd = state_gdn.shape[0], state_mlstm_c.shape[0]
    y_prompt, p_gdn, p_conv, _, p_c, p_n, p_m = _trunk(
        x_prompt, True,
        jnp.zeros((n_even, bp) + state_gdn.shape[2:], F32),
        jnp.zeros((n_even, bp) + state_gdn_conv.shape[2:], x_prompt.dtype),
        jnp.zeros((n_odd, bp) + state_mlstm_c.shape[2:], F32),
        jnp.zeros((n_odd, bp) + state_mlstm_n.shape[2:], F32),
        jnp.zeros((n_odd, bp) + state_mlstm_m.shape[2:], F32),
        *weights)
    y_sample, s_gdn, s_conv, s_v, s_c, s_n, s_m = _trunk(
        x_sample, False, state_gdn, state_gdn_conv, state_mlstm_c, state_mlstm_n, state_mlstm_m,
        *weights)
    return (y_prompt, y_sample, p_gdn, s_gdn, p_conv, s_conv, s_v, p_c, s_c, p_n, s_n, p_m, s_m)
```

```python
import functools
import math
from types import SimpleNamespace

import jax
import jax.numpy as jnp
from jax import lax
from jax.experimental import pallas as pl
from jax.experimental.pallas import tpu as pltpu

F32 = jnp.float32
BF16 = jnp.bfloat16
HIGHEST = lax.Precision.HIGHEST

EPS = 1e-6
N_HEADS = 4
HEAD_DIM = 128
MLSTM_DV = 256
CONV_W = 4
CHUNK_ROWS = 64
SGU_CHUNK = 128
SMALL_COLS = 128
SMALL_ROWS = 16
VMEM_LIMIT = 56 * 1024 * 1024


def _cparams(sem):
    return pltpu.CompilerParams(dimension_semantics=sem, vmem_limit_bytes=VMEM_LIMIT)


def _mm(a, b):
    return jnp.dot(a.astype(BF16), b.astype(BF16), preferred_element_type=F32)


def _mm_nt(a, b):
    return lax.dot_general(a.astype(BF16), b.astype(BF16), (((1,), (1,)), ((), ())),
                           preferred_element_type=F32)


def _mm_hi(a, b):
    return jnp.dot(a, b, preferred_element_type=F32, precision=HIGHEST)


def _rms(x, g):
    return x * lax.rsqrt(jnp.mean(x * x, axis=-1, keepdims=True) + EPS) * g


def _l2norm(x):
    return x * lax.rsqrt(jnp.sum(x * x, axis=-1, keepdims=True) + EPS)


def _softplus(x):
    return jnp.maximum(x, 0.0) + jnp.log1p(jnp.exp(-jnp.abs(x)))


def _sigmoid(x):
    return 0.5 * jnp.tanh(0.5 * x) + 0.5


def _silu(x):
    return x * _sigmoid(x)


def _gelu(x):
    return 0.5 * x * (1.0 + lax.erf(x * (2.0 ** -0.5)))


def _proj_in_kernel(x_ref, g_ref, w_ref, wst_ref, o_ref, ot_ref):
    h = _rms(x_ref[...], g_ref[...]).astype(BF16)
    o_ref[...] = jnp.dot(h, w_ref[...], preferred_element_type=F32)
    ot_ref[...] = lax.dot_general(wst_ref[...], h, (((1,), (1,)), ((), ())),
                                  preferred_element_type=F32)


def _proj_in(x2d, g, w_main, w_small_t, tm):
    t, d = x2d.shape
    n = w_main.shape[1]
    return pl.pallas_call(
        _proj_in_kernel,
        grid=(t // tm,),
        in_specs=[pl.BlockSpec((tm, d), lambda i: (i, 0)),
                  pl.BlockSpec((1, d), lambda i: (0, 0)),
                  pl.BlockSpec((d, n), lambda i: (0, 0)),
                  pl.BlockSpec((SMALL_ROWS, d), lambda i: (0, 0))],
        out_specs=[pl.BlockSpec((tm, n), lambda i: (i, 0)),
                   pl.BlockSpec((SMALL_ROWS, tm), lambda i: (0, i))],
        out_shape=[jax.ShapeDtypeStruct((t, n), F32),
                   jax.ShapeDtypeStruct((SMALL_ROWS, t), F32)],
        compiler_params=_cparams(("parallel",)),
        name="proj_in",
    )(x2d, g.reshape(1, d), w_main, w_small_t)


def _mix_ffn_kernel(*refs, n_act):
    x_ref, g_ref = refs[0], refs[1]
    acts = refs[2:2 + n_act]
    wos = refs[2 + n_act:2 + 2 * n_act]
    w1_ref, w2_ref, o_ref, h_ref, acc_ref = refs[2 + 2 * n_act:]
    k = pl.program_id(1)

    @pl.when(k == 0)
    def _():
        mix = None
        for a_ref, w_ref in zip(acts, wos):
            p = jnp.dot(a_ref[...].astype(BF16), w_ref[0], preferred_element_type=F32)
            mix = p if mix is None else mix + p
        x1 = x_ref[...] + _rms(mix, g_ref[0:1, :])
        o_ref[...] = x1
        h_ref[...] = _rms(x1, g_ref[1:2, :]).astype(BF16)
        acc_ref[...] = jnp.zeros_like(acc_ref)

    a = jnp.dot(h_ref[...], w1_ref[0], preferred_element_type=F32)
    a = jnp.square(jnp.maximum(a, 0.0)).astype(BF16)
    acc_ref[...] += jnp.dot(a, w2_ref[0], preferred_element_type=F32)

    @pl.when(k == pl.num_programs(1) - 1)
    def _():
        o_ref[...] = o_ref[...] + _rms(acc_ref[...], g_ref[2:3, :])


def _mix_ffn(x2d, g3rows, acts, w_out_all, w_out_idx, w1_all, w2_all, layer, tm, tf):
    t, d = x2d.shape
    ff = w1_all.shape[2]
    n_act = len(acts)
    in_specs = [pl.BlockSpec((tm, d), lambda i, k: (i, 0)), pl.BlockSpec((3, d), lambda i, k: (0, 0))]
    in_specs += [pl.BlockSpec((tm, a.shape[1]), lambda i, k: (i, 0)) for a in acts]
    in_specs += [pl.BlockSpec((1, a.shape[1], d), lambda i, k, li=li, ri=ri: (li, ri, 0))
                 for a, (li, ri) in zip(acts, w_out_idx)]
    in_specs += [pl.BlockSpec((1, d, tf), lambda i, k: (layer, 0, k)),
                 pl.BlockSpec((1, tf, d), lambda i, k: (layer, k, 0))]
    return pl.pallas_call(
        functools.partial(_mix_ffn_kernel, n_act=n_act),
        grid=(t // tm, ff // tf),
        in_specs=in_specs,
        out_specs=pl.BlockSpec((tm, d), lambda i, k: (i, 0)),
        out_shape=jax.ShapeDtypeStruct((t, d), F32),
        scratch_shapes=[pltpu.VMEM((tm, d), BF16), pltpu.VMEM((tm, d), F32)],
        compiler_params=_cparams(("parallel", "arbitrary")),
        name="mix_ffn",
    )(x2d, g3rows, *acts, *([w_out_all] * n_act), w1_all, w2_all)


def _ind(mask):
    return jnp.where(mask, 1.0, 0.0).astype(F32)


def _div_pow2(x, d):
    return jnp.right_shift(x, int(math.log2(d)))


def _chunk_structs(rows, ns):
    lt = rows // ns
    ri = lax.broadcasted_iota(jnp.int32, (rows, rows), 0)
    ci = lax.broadcasted_iota(jnp.int32, (rows, rows), 1)
    cs = SimpleNamespace(rows=rows, ns=ns, lt=lt)
    if ns == 1:
        cs.same = None
        cs.causal = ri >= ci
        cs.strict = ri > ci
        cs.upper = ri <= ci
    else:
        rseq = _div_pow2(ri, lt)
        cseq = _div_pow2(ci, lt)
        cs.same = rseq == cseq
        cs.causal = cs.same & (ri >= ci)
        cs.strict = cs.same & (ri > ci)
        cs.upper = cs.same & (ri <= ci)
        cs.last = _ind(ci == rseq * lt + (lt - 1))
        cs.last_t = _ind(ri == cseq * lt + (lt - 1))
        sr = _div_pow2(lax.broadcasted_iota(jnp.int32, (ns * HEAD_DIM, rows), 0), HEAD_DIM)
        sc = lax.broadcasted_iota(jnp.int32, (ns * HEAD_DIM, rows), 1)
        cs.stack_last = _ind(sc == sr * lt + (lt - 1))
        cs.stack_mask = _ind(_div_pow2(sc, lt) == sr)
        er = _div_pow2(lax.broadcasted_iota(jnp.int32, (rows, ns * HEAD_DIM), 0), lt)
        ec = _div_pow2(lax.broadcasted_iota(jnp.int32, (rows, ns * HEAD_DIM), 1), HEAD_DIM)
        cs.expand_mask = _ind(er == ec)
        kr = lax.broadcasted_iota(jnp.int32, (ns, rows), 0)
        kc = lax.broadcasted_iota(jnp.int32, (ns, rows), 1)
        cs.seq_last = _ind(kc == kr * lt + (lt - 1))
        cs.seq_sum = _ind(_div_pow2(kc, lt) == kr)
        pr = _div_pow2(lax.broadcasted_iota(jnp.int32, (rows, HEAD_DIM), 0), lt)
        pc = lax.broadcasted_iota(jnp.int32, (rows, HEAD_DIM), 1)
        cs.seq_expand = _ind(pr == pc)
    cs.lower_f = _ind(cs.causal)
    cs.upper_f = _ind(cs.upper)
    eye_r = lax.broadcasted_iota(jnp.int32, (HEAD_DIM, HEAD_DIM), 0)
    eye_c = lax.broadcasted_iota(jnp.int32, (HEAD_DIM, HEAD_DIM), 1)
    cs.eye = _ind(eye_r == eye_c)
    cs.eye_rows = _ind(ri == ci)
    return cs


def _last_rows(cs, x):
    if cs.ns == 1:
        return jnp.broadcast_to(x[cs.rows - 1:cs.rows, :], x.shape)
    return _mm_hi(cs.last, x)


def _last_lanes(cs, x):
    if cs.ns == 1:
        return jnp.broadcast_to(x[:, cs.rows - 1:cs.rows], x.shape)
    return _mm_hi(x, cs.last_t)


def _expand_lhs(cs, x):
    if cs.ns == 1:
        return x
    reps = x.shape[-2] // cs.rows
    mask = cs.expand_mask if reps == 1 else jnp.concatenate([cs.expand_mask] * reps, axis=0)
    return jnp.concatenate([x] * cs.ns, axis=-1) * mask


def _stack_t(cs, x):
    xt = _mm_nt(cs.eye, x)
    if cs.ns == 1:
        return xt
    return jnp.concatenate([xt] * cs.ns, axis=0) * cs.stack_mask


def _stack_scalar(cs, col):
    if cs.ns == 1:
        return col[cs.rows - 1:cs.rows, :]
    return _mm_hi(cs.stack_last, jnp.broadcast_to(col, (cs.rows, HEAD_DIM)))[:, 0:1]


_BATCH_NN = (((2,), (1,)), ((0,), (0,)))
_BATCH_NT = (((2,), (2,)), ((0,), (0,)))
_BATCH_TN = (((1,), (1,)), ((0,), (0,)))


def _bmm(a, b, dims=_BATCH_NN):
    return lax.dot_general(a.astype(BF16), b.astype(BF16), dims, preferred_element_type=F32)


def _split2(a):
    hi = a.astype(BF16)
    return hi, (a - hi.astype(F32)).astype(BF16)


def _bmm_split(ap, bp):
    ah, al = ap
    bh, bl = bp
    f = lambda x, y: lax.dot_general(x, y, _BATCH_NN, preferred_element_type=F32)
    return f(ah, bh) + (f(ah, bl) + f(al, bh))


def _unit_lower_inverse_off(cs, a):
    toff = -a
    p_parts = _split2(toff)
    for _ in range(int(math.log2(cs.lt)) - 1):
        p = _bmm_split(p_parts, p_parts)
        p_parts = _split2(p)
        toff = toff + p + _bmm_split(_split2(toff), p_parts)
    return toff


def _gdn_block(cs, nchunks, xc, gate, sm, smt, prm_row, prm_col, gng, s_heads, write_out):
    rows = cs.rows
    hq = N_HEADS * HEAD_DIM
    pairs = [(c, h) for c in range(nchunks) for h in range(N_HEADS)]
    rs = lambda c: slice(c * rows, (c + 1) * rows)

    def heads(x, col0):
        return jnp.stack([x[rs(c), col0 + h * HEAD_DIM:col0 + (h + 1) * HEAD_DIM] for c, h in pairs])

    def cols(x, lane0):
        return jnp.stack([x[rs(c), lane0 + h:lane0 + h + 1] for c, h in pairs])

    beta_all = _sigmoid(sm)
    g_c = -prm_row[0:1, :] * _softplus(sm + prm_row[1:2, :])
    g_r = -prm_col[:, 0:1] * _softplus(smt + prm_col[:, 1:2])
    gc_c = jnp.concatenate([_mm_hi(cs.lower_f, g_c[rs(c)]) for c in range(nchunks)], axis=0)
    gc_r = jnp.concatenate([_mm_hi(g_r[:, rs(c)], cs.upper_f) for c in range(nchunks)], axis=1)
    gl_c = jnp.concatenate([_last_rows(cs, gc_c[rs(c)]) for c in range(nchunks)], axis=0)
    egc = jnp.exp(gc_c)
    ekd = jnp.exp(gl_c - gc_c)

    q = _l2norm(heads(xc, 0)) * (HEAD_DIM ** -0.5)
    k = _l2norm(heads(xc, hq))
    v = heads(xc, 2 * hq)
    beta = cols(beta_all, 0)
    gcol = cols(gc_c, N_HEADS)
    eg = cols(egc, N_HEADS)
    grow = jnp.stack([gc_r[N_HEADS + h:N_HEADS + h + 1, rs(c)] for c, h in pairs])
    decay = jnp.where(cs.causal, jnp.exp(jnp.where(cs.causal, gcol - grow, 0.0)), 0.0)
    kb = k * beta
    a = jnp.where(cs.strict, _bmm(kb, k, _BATCH_NT) * decay, 0.0)
    toff = _unit_lower_inverse_off(cs, a)
    rhs = jnp.concatenate([v * beta, kb * eg], axis=-1)
    sol = rhs + _bmm_split(_split2(toff), _split2(rhs))
    u_val, w_k = sol[..., :HEAD_DIM], sol[..., HEAD_DIM:]
    qk = jnp.where(cs.causal, _bmm(q, k, _BATCH_NT) * decay, 0.0)
    q_dec = q * eg
    k_dec = k * cols(ekd, N_HEADS)

    s = s_heads
    for c in range(nchunks):
        hs = slice(c * N_HEADS, (c + 1) * N_HEADS)
        p = _bmm(_expand_lhs(cs, jnp.concatenate([w_k[hs], q_dec[hs]], axis=1)), s)
        v_new = u_val[hs] - p[:, :rows]
        o = p[:, rows:] + _bmm(qk[hs], v_new)
        gl_cols = [gl_c[rs(c), N_HEADS + h:N_HEADS + h + 1] for h in range(N_HEADS)]
        g_last = jnp.exp(jnp.stack([_stack_scalar(cs, col) for col in gl_cols]))
        if cs.ns == 1:
            upd = _bmm(k_dec[hs], v_new, _BATCH_TN)
        else:
            kt = _bmm(jnp.broadcast_to(cs.eye, (N_HEADS, HEAD_DIM, HEAD_DIM)), k_dec[hs], _BATCH_NT)
            upd = _bmm(jnp.concatenate([kt] * cs.ns, axis=1) * cs.stack_mask, v_new)
        s = s * g_last + upd
        for h in range(N_HEADS):
            write_out(c, h, _rms(o[h], gng) * _silu(gate[rs(c), h * HEAD_DIM:(h + 1) * HEAD_DIM]))
    return s


def _gdn_prompt_kernel(qkv_ref, gate_ref, sm_ref, smt_ref, cw_ref, prow_ref, pcol_ref, gng_ref,
                       s0_ref, buf_ref, o_ref, sout_ref, bufout_ref, s_scr, xp_scr, *, tb):
    t = pl.program_id(1)
    cs = _chunk_structs(CHUNK_ROWS, 1)

    @pl.when(t == 0)
    def _():
        s_scr[...] = s0_ref[0]
        xp_scr[...] = buf_ref[0]

    x = qkv_ref[0]
    xp = jnp.concatenate([xp_scr[...], x], axis=0)
    y = cw_ref[CONV_W - 1:CONV_W, :] * x
    for back in range(1, CONV_W):
        y = y + cw_ref[CONV_W - 1 - back:CONV_W - back, :] * pltpu.roll(xp, back, axis=0)[8:]
    tail = x[tb - 8:]
    xp_scr[...] = tail

    def write_out(c, h, o):
        o_ref[0, c * CHUNK_ROWS:(c + 1) * CHUNK_ROWS, h * HEAD_DIM:(h + 1) * HEAD_DIM] = o

    s_scr[...] = _gdn_block(cs, tb // CHUNK_ROWS, _silu(y), gate_ref[0], sm_ref[0], smt_ref[...],
                            prow_ref[...], pcol_ref[...], gng_ref[...], s_scr[...], write_out)

    @pl.when(t == pl.num_programs(1) - 1)
    def _():
        sout_ref[0] = s_scr[...]
        bufout_ref[0] = tail


def _gdn_prompt(proj, smt, conv_w, prm_row, prm_col, gng, s0, buf8, bsz, seq, tb):
    hq = N_HEADS * HEAD_DIM
    cch = 3 * hq
    nt = seq // tb
    proj3 = proj.reshape(bsz, seq, proj.shape[1])
    small_blk = (cch + 3 * hq) // SMALL_COLS
    return pl.pallas_call(
        functools.partial(_gdn_prompt_kernel, tb=tb),
        grid=(bsz, nt),
        in_specs=[pl.BlockSpec((1, tb, cch), lambda b, t: (b, t, 0)),
                  pl.BlockSpec((1, tb, hq), lambda b, t: (b, t, cch // hq)),
                  pl.BlockSpec((1, tb, SMALL_COLS), lambda b, t: (b, t, small_blk)),
                  pl.BlockSpec((SMALL_ROWS, tb), lambda b, t: (0, b * nt + t)),
                  pl.BlockSpec((CONV_W, cch), lambda b, t: (0, 0)),
                  pl.BlockSpec((2, SMALL_COLS), lambda b, t: (0, 0)),
                  pl.BlockSpec((SMALL_ROWS, 2), lambda b, t: (0, 0)),
                  pl.BlockSpec((1, HEAD_DIM), lambda b, t: (0, 0)),
                  pl.BlockSpec((1, N_HEADS, HEAD_DIM, HEAD_DIM), lambda b, t: (b, 0, 0, 0)),
                  pl.BlockSpec((1, 8, cch), lambda b, t: (b, 0, 0))],
        out_specs=[pl.BlockSpec((1, tb, hq), lambda b, t: (b, t, 0)),
                   pl.BlockSpec((1, N_HEADS, HEAD_DIM, HEAD_DIM), lambda b, t: (b, 0, 0, 0)),
                   pl.BlockSpec((1, 8, cch), lambda b, t: (b, 0, 0))],
        out_shape=[jax.ShapeDtypeStruct((bsz, seq, hq), F32),
                   jax.ShapeDtypeStruct((bsz, N_HEADS, HEAD_DIM, HEAD_DIM), F32),
                   jax.ShapeDtypeStruct((bsz, 8, cch), F32)],
        scratch_shapes=[pltpu.VMEM((N_HEADS, HEAD_DIM, HEAD_DIM), F32),
                        pltpu.VMEM((8, cch), F32)],
        compiler_params=_cparams(("parallel", "arbitrary")),
        name="gdn_prompt",
    )(proj3, proj3, proj3, smt, conv_w, prm_row, prm_col, gng.reshape(1, HEAD_DIM), s0, buf8)


def _conv_sample_kernel(xcat_ref, cw_ref, o_ref, *, lt, cch):
    for t in range(lt):
        y = None
        for j in range(CONV_W):
            term = cw_ref[j:j + 1, :] * xcat_ref[:, (t + j) * cch:(t + j + 1) * cch]
            y = term if y is None else y + term
        o_ref[:, t * cch:(t + 1) * cch] = _silu(y)


def _conv_sample(xcat, conv_w, lt, cch):
    nseq = xcat.shape[0]
    return pl.pallas_call(
        functools.partial(_conv_sample_kernel, lt=lt, cch=cch),
        grid=(1,),
        in_specs=[pl.BlockSpec(xcat.shape, lambda i: (0, 0)),
                  pl.BlockSpec((CONV_W, cch), lambda i: (0, 0))],
        out_specs=pl.BlockSpec((nseq, lt * cch), lambda i: (0, 0)),
        out_shape=jax.ShapeDtypeStruct((nseq, lt * cch), F32),
        compiler_params=_cparams(("arbitrary",)),
        name="conv_sample",
    )(xcat, conv_w)


def _gdn_sample_kernel(xc_ref, gate_ref, sm_ref, smt_ref, prow_ref, pcol_ref, gng_ref, s0_ref,
                       *rest, ns):
    o_ref, sout_ref = rest[-2:]
    cs = _chunk_structs(CHUNK_ROWS, ns)
    s_heads = jnp.stack([s0_ref[0, :, h].reshape(ns * HEAD_DIM, HEAD_DIM) for h in range(N_HEADS)])

    def write_out(c, h, o):
        o_ref[:, h * HEAD_DIM:(h + 1) * HEAD_DIM] = o

    s_new = _gdn_block(cs, 1, xc_ref[...], gate_ref[...], sm_ref[...], smt_ref[0],
                       prow_ref[...], pcol_ref[...], gng_ref[...], s_heads, write_out)
    for h in range(N_HEADS):
        sout_ref[0, :, h] = s_new[h].reshape(ns, HEAD_DIM, HEAD_DIM)


def _gdn_sample(xc, proj, smt, prm_row, prm_col, gng, s_all, layer, s_out_prev, lt):
    hq = N_HEADS * HEAD_DIM
    cch = 3 * hq
    ns = CHUNK_ROWS // lt
    nchunk = xc.shape[0] // CHUNK_ROWS
    small_blk = (cch + 3 * hq) // SMALL_COLS
    state_spec = pl.BlockSpec((1, ns, N_HEADS, HEAD_DIM, HEAD_DIM), lambda c: (layer, c, 0, 0, 0))
    in_specs = [pl.BlockSpec((CHUNK_ROWS, cch), lambda c: (c, 0)),
                pl.BlockSpec((CHUNK_ROWS, hq), lambda c: (c, cch // hq)),
                pl.BlockSpec((CHUNK_ROWS, SMALL_COLS), lambda c: (c, small_blk)),
                pl.BlockSpec((1, SMALL_ROWS, CHUNK_ROWS), lambda c: (c, 0, 0)),
                pl.BlockSpec((2, SMALL_COLS), lambda c: (0, 0)),
                pl.BlockSpec((SMALL_ROWS, 2), lambda c: (0, 0)),
                pl.BlockSpec((1, HEAD_DIM), lambda c: (0, 0)),
                state_spec]
    args = [xc, proj, proj, smt, prm_row, prm_col, gng.reshape(1, HEAD_DIM), s_all]
    aliases = {}
    if s_out_prev is not None:
        aliases = {len(args): 1}
        in_specs.append(pl.BlockSpec(memory_space=pl.ANY))
        args.append(s_out_prev)
    return pl.pallas_call(
        functools.partial(_gdn_sample_kernel, ns=ns),
        grid=(nchunk,),
        in_specs=in_specs,
        out_specs=[pl.BlockSpec((CHUNK_ROWS, hq), lambda c: (c, 0)), state_spec],
        out_shape=[jax.ShapeDtypeStruct((xc.shape[0], hq), F32),
                   jax.ShapeDtypeStruct(s_all.shape, F32)],
        input_output_aliases=aliases,
        compiler_params=_cparams(("parallel",)),
        name="gdn_sample",
    )(*args)


def _sgu_kernel(u_ref, v_ref, w_ref, b_ref, g_ref, o_ref, vb_ref, *, rows, ns):
    lt = rows // ns
    ri = lax.broadcasted_iota(jnp.int32, (rows, rows), 0)
    ci = lax.broadcasted_iota(jnp.int32, (rows, rows), 1)
    keep = ri >= ci
    if ns > 1:
        keep = keep & (_div_pow2(ri, lt) == _div_pow2(ci, lt))
    for g in range(N_HEADS):
        lo = g * HEAD_DIM
        w = jnp.where(keep, w_ref[g], 0.0).astype(BF16)
        u = _gelu(u_ref[:, lo:lo + HEAD_DIM])
        vb = _rms(_gelu(v_ref[:, lo:lo + HEAD_DIM]), g_ref[g:g + 1, :])
        vb_ref[:, lo:lo + HEAD_DIM] = vb
        for c in range(u_ref.shape[0] // rows):
            rs = slice(c * rows, (c + 1) * rows)
            z = _mm(w, vb[rs]) + b_ref[:, g:g + 1]
            o_ref[rs, lo:lo + HEAD_DIM] = u[rs] * z


def _sgu(proj, w_tiles, b_cols, norm_g, rows, ns, nchunks):
    hq = N_HEADS * HEAD_DIM
    t = proj.shape[0]
    u_blk = (3 * hq + hq) // hq
    blk = rows * nchunks
    return pl.pallas_call(
        functools.partial(_sgu_kernel, rows=rows, ns=ns),
        grid=(t // blk,),
        in_specs=[pl.BlockSpec((blk, hq), lambda i: (i, u_blk)),
                  pl.BlockSpec((blk, hq), lambda i: (i, u_blk + 1)),
                  pl.BlockSpec((N_HEADS, rows, rows), lambda i: (0, 0, 0)),
                  pl.BlockSpec((rows, SMALL_COLS), lambda i: (0, 0)),
                  pl.BlockSpec((N_HEADS, HEAD_DIM), lambda i: (0, 0))],
        out_specs=[pl.BlockSpec((blk, hq), lambda i: (i, 0)),
                   pl.BlockSpec((blk, hq), lambda i: (i, 0))],
        out_shape=[jax.ShapeDtypeStruct((t, hq), F32), jax.ShapeDtypeStruct((t, hq), F32)],
        compiler_params=_cparams(("parallel",)),
        name="sgu",
    )(proj, proj, w_tiles, b_cols, norm_g)


def _logsigmoid(x):
    return jnp.minimum(x, 0.0) - jnp.log1p(jnp.exp(-jnp.abs(x)))


def _mlstm_chunk(cs, q_all, k_all, v_all, opre, sm, smt, gb_row, gb_col, mng, c_list, n_rows, m_rows):
    neg_inf = -jnp.inf
    pre_c = sm + gb_row
    pre_r = smt + gb_col
    b_c = _mm_hi(cs.lower_f, _logsigmoid(pre_c))
    b_r = _mm_hi(_logsigmoid(pre_r), cs.upper_f)
    bl_c = _last_rows(cs, b_c)
    bl_r = _last_lanes(cs, b_r)
    res = []
    for h in range(N_HEADS):
        lo = h * HEAD_DIM
        vo = h * MLSTM_DV
        q = q_all[:, lo:lo + HEAD_DIM] * (HEAD_DIM ** -0.5)
        k = k_all[:, lo:lo + HEAD_DIM]
        v = v_all[:, vo:vo + MLSTM_DV]
        bcol = b_c[:, N_HEADS + h:N_HEADS + h + 1]
        brow = b_r[N_HEADS + h:N_HEADS + h + 1, :]
        icol = pre_c[:, h:h + 1]
        irow = pre_r[h:h + 1, :]
        blcol = bl_c[:, N_HEADS + h:N_HEADS + h + 1]
        blrow = bl_r[N_HEADS + h:N_HEADS + h + 1, :]
        mrow = m_rows[:, h:h + 1]
        inter = bcol + mrow
        dmat = jnp.where(cs.causal, bcol - brow + irow, neg_inf)
        m_t = jnp.maximum(inter, jnp.max(dmat, axis=-1, keepdims=True))
        w_intra = jnp.exp(dmat - m_t)
        w_inter = jnp.exp(inter - m_t)
        s = _mm_nt(q, k) * w_intra
        c_old = c_list[h]
        num = w_inter * _mm(_expand_lhs(cs, q), c_old) + _mm(s, v)
        qn = jnp.sum(q * n_rows[:, lo:lo + HEAD_DIM], axis=-1, keepdims=True)
        den = w_inter * qn + jnp.sum(s, axis=-1, keepdims=True)
        hval = num / jnp.maximum(jnp.abs(den), jnp.exp(-m_t))
        out = _rms(hval, mng) * _sigmoid(opre[:, vo:vo + MLSTM_DV])
        logw_row = blrow - brow + irow
        if cs.ns == 1:
            seq_max = jnp.max(logw_row, axis=-1, keepdims=True)
        else:
            seq_max = jnp.max(jnp.where(cs.same, logw_row, neg_inf), axis=-1, keepdims=True)
        m_new = jnp.maximum(blcol + mrow, seq_max)
        keep = jnp.exp(blcol + mrow - m_new)
        wk = jnp.exp(blcol - bcol + icol - m_new)
        kw = k * wk
        c_new = c_old * _stack_scalar(cs, keep) + _mm(_stack_t(cs, kw), v)
        res.append((out, c_new, keep, m_new, kw))
    return res


def _mlstm_prompt_block(cs, nchunks, q_all, k_all, v_all, opre, sm, smt, gb_row, gb_col, mng,
                        c_heads, n_heads, m_heads, write_out):
    rows = cs.rows
    pairs = [(c, h) for c in range(nchunks) for h in range(N_HEADS)]
    rs = lambda c: slice(c * rows, (c + 1) * rows)

    def heads(x, width):
        return jnp.stack([x[rs(c), h * width:(h + 1) * width] for c, h in pairs])

    def cols(x, lane0):
        return jnp.stack([x[rs(c), lane0 + h:lane0 + h + 1] for c, h in pairs])

    def lanes(x, row0):
        return jnp.stack([x[row0 + h:row0 + h + 1, rs(c)] for c, h in pairs])

    pre_c = sm + gb_row
    pre_r = smt + gb_col
    lf_c = _logsigmoid(pre_c)
    lf_r = _logsigmoid(pre_r)
    b_c = jnp.concatenate([_mm_hi(cs.lower_f, lf_c[rs(c)]) for c in range(nchunks)], axis=0)
    b_r = jnp.concatenate([_mm_hi(lf_r[:, rs(c)], cs.upper_f) for c in range(nchunks)], axis=1)

    q = heads(q_all, HEAD_DIM) * (HEAD_DIM ** -0.5)
    k = heads(k_all, HEAD_DIM)
    v = heads(v_all, MLSTM_DV)
    bcol = cols(b_c, N_HEADS)
    icol = cols(pre_c, 0)
    brow = lanes(b_r, N_HEADS)
    irow = lanes(pre_r, 0)
    blast = bcol[:, rows - 1:rows, :]
    dmat = jnp.where(cs.causal, bcol - brow + irow, -jnp.inf)
    rmax = jnp.max(dmat, axis=-1, keepdims=True)
    sp = _bmm(q, k, _BATCH_NT) * jnp.exp(dmat - rmax)
    sv = _bmm(sp, v)
    ssum = jnp.sum(sp, axis=-1, keepdims=True)
    lmax = jnp.max(blast - brow + irow, axis=-1, keepdims=True)
    kwp = k * jnp.exp(blast - bcol + icol - lmax)
    upd = _bmm(kwp, v, _BATCH_TN)
    nsum = jnp.sum(kwp, axis=1, keepdims=True)

    cst, n, m = c_heads, n_heads, m_heads
    for c in range(nchunks):
        hs = slice(c * N_HEADS, (c + 1) * N_HEADS)
        inter = bcol[hs] + m
        m_t = jnp.maximum(inter, rmax[hs])
        w_inter = jnp.exp(inter - m_t)
        w_intra = jnp.exp(rmax[hs] - m_t)
        num = w_inter * _bmm(q[hs], cst) + w_intra * sv[hs]
        den = (w_inter * jnp.sum(q[hs] * n, axis=-1, keepdims=True) + w_intra * ssum[hs])
        hval = num / jnp.maximum(jnp.abs(den), jnp.exp(-m_t))
        for h in range(N_HEADS):
            write_out(c, h, _rms(hval[h], mng) * _sigmoid(opre[rs(c), h * MLSTM_DV:(h + 1) * MLSTM_DV]))
        m_new = jnp.maximum(blast[hs] + m, lmax[hs])
        keep = jnp.exp(blast[hs] + m - m_new)
        fresh = jnp.exp(lmax[hs] - m_new)
        cst = keep * cst + fresh * upd[hs]
        n = keep * n + fresh * nsum[hs]
        m = m_new
    return cst, n, m


def _mlstm_prompt_kernel(q_ref, k_ref, v_ref, op_ref, sm_ref, smt_ref, gbr_ref, gbc_ref, mng_ref,
                         c0_ref, n0_ref, m0_ref, o_ref, cout_ref, nout_ref, mout_ref,
                         c_scr, n_scr, m_scr, *, tb):
    t = pl.program_id(1)
    cs = _chunk_structs(CHUNK_ROWS, 1)
    hq = N_HEADS * HEAD_DIM
    lane = lax.broadcasted_iota(jnp.int32, (1, SMALL_COLS), 1)

    @pl.when(t == 0)
    def _():
        c_scr[...] = c0_ref[0]
        n_scr[...] = n0_ref[0]
        m_scr[...] = m0_ref[0]

    n_cur = n_scr[...]
    m_cur = m_scr[...]
    n_heads = jnp.stack([n_cur[:, h * HEAD_DIM:(h + 1) * HEAD_DIM] for h in range(N_HEADS)])
    m_heads = jnp.stack([m_cur[:, h:h + 1] for h in range(N_HEADS)])

    def write_out(c, h, o):
        o_ref[0, c * CHUNK_ROWS:(c + 1) * CHUNK_ROWS, h * MLSTM_DV:(h + 1) * MLSTM_DV] = o

    c_new, n_heads, m_heads = _mlstm_prompt_block(
        cs, tb // CHUNK_ROWS, q_ref[0], k_ref[0], v_ref[0], op_ref[0], sm_ref[0], smt_ref[...],
        gbr_ref[...], gbc_ref[...], mng_ref[...], c_scr[...], n_heads, m_heads, write_out)
    c_scr[...] = c_new
    n_scr[...] = jnp.concatenate([n_heads[h] for h in range(N_HEADS)], axis=1)
    for h in range(N_HEADS):
        m_cur = jnp.where(lane == h, m_heads[h], m_cur)
    m_scr[...] = m_cur

    @pl.when(t == pl.num_programs(1) - 1)
    def _():
        cout_ref[0] = c_scr[...]
        nout_ref[0] = n_scr[...]
        mout_ref[0] = m_scr[...]


def _mlstm_prompt(proj, smt, gb_row, gb_col, mng, c0, n0, m0, bsz, seq, tb):
    hq = N_HEADS * HEAD_DIM
    hv = N_HEADS * MLSTM_DV
    nt = seq // tb
    proj3 = proj.reshape(bsz, seq, proj.shape[1])
    small_blk = (2 * hq + 2 * hv) // SMALL_COLS
    state_c = pl.BlockSpec((1, N_HEADS, HEAD_DIM, MLSTM_DV), lambda b, t: (b, 0, 0, 0))
    state_n = pl.BlockSpec((1, 1, hq), lambda b, t: (b, 0, 0))
    state_m = pl.BlockSpec((1, 1, SMALL_COLS), lambda b, t: (b, 0, 0))
    return pl.pallas_call(
        functools.partial(_mlstm_prompt_kernel, tb=tb),
        grid=(bsz, nt),
        in_specs=[pl.BlockSpec((1, tb, hq), lambda b, t: (b, t, 0)),
                  pl.BlockSpec((1, tb, hq), lambda b, t: (b, t, 1)),
                  pl.BlockSpec((1, tb, hv), lambda b, t: (b, t, 2 * hq // hv)),
                  pl.BlockSpec((1, tb, hv), lambda b, t: (b, t, 2 * hq // hv + 1)),
                  pl.BlockSpec((1, tb, SMALL_COLS), lambda b, t: (b, t, small_blk)),
                  pl.BlockSpec((SMALL_ROWS, tb), lambda b, t: (0, b * nt + t)),
                  pl.BlockSpec((1, SMALL_COLS), lambda b, t: (0, 0)),
                  pl.BlockSpec((SMALL_ROWS, 1), lambda b, t: (0, 0)),
                  pl.BlockSpec((1, MLSTM_DV), lambda b, t: (0, 0)),
                  state_c, state_n, state_m],
        out_specs=[pl.BlockSpec((1, tb, hv), lambda b, t: (b, t, 0)), state_c, state_n, state_m],
        out_shape=[jax.ShapeDtypeStruct((bsz, seq, hv), F32),
                   jax.ShapeDtypeStruct(c0.shape, F32),
                   jax.ShapeDtypeStruct(n0.shape, F32),
                   jax.ShapeDtypeStruct(m0.shape, F32)],
        scratch_shapes=[pltpu.VMEM((N_HEADS, HEAD_DIM, MLSTM_DV), F32),
                        pltpu.VMEM((1, hq), F32),
                        pltpu.VMEM((1, SMALL_COLS), F32)],
        compiler_params=_cparams(("parallel", "arbitrary")),
        name="mlstm_prompt",
    )(proj3, proj3, proj3, proj3, proj3, smt, gb_row, gb_col, mng.reshape(1, MLSTM_DV), c0, n0, m0)


def _mlstm_sample_kernel(q_ref, k_ref, v_ref, op_ref, sm_ref, smt_ref, gbr_ref, gbc_ref, mng_ref,
                         c0_ref, n0_ref, m0_ref, *rest, ns):
    o_ref, cout_ref, nout_ref, mout_ref = rest[-4:]
    cs = _chunk_structs(CHUNK_ROWS, ns)
    lane = lax.broadcasted_iota(jnp.int32, (1, SMALL_COLS), 1)
    c_list = [c0_ref[0, :, h].reshape(ns * HEAD_DIM, MLSTM_DV) for h in range(N_HEADS)]
    n0 = n0_ref[...]
    m0 = m0_ref[...]
    n_rows = _mm_hi(cs.seq_expand, jnp.concatenate(
        [n0, jnp.zeros((HEAD_DIM - ns, n0.shape[1]), F32)], axis=0))
    m_rows = _mm_hi(cs.seq_expand, jnp.concatenate(
        [m0, jnp.zeros((HEAD_DIM - ns, m0.shape[1]), F32)], axis=0))
    res = _mlstm_chunk(cs, q_ref[...], k_ref[...], v_ref[...], op_ref[...], sm_ref[...], smt_ref[0],
                       gbr_ref[...], gbc_ref[...], mng_ref[...], c_list, n_rows, m_rows)
    m_all = jnp.zeros((CHUNK_ROWS, SMALL_COLS), F32)
    for h, (out, c_new, keep, m_new, kw) in enumerate(res):
        o_ref[:, h * MLSTM_DV:(h + 1) * MLSTM_DV] = out
        cout_ref[0, :, h] = c_new.reshape(ns, HEAD_DIM, MLSTM_DV)
        keep_seq = _mm_hi(cs.seq_last, jnp.broadcast_to(keep, (CHUNK_ROWS, HEAD_DIM)))
        nout_ref[:, h * HEAD_DIM:(h + 1) * HEAD_DIM] = (
            keep_seq * n0[:, h * HEAD_DIM:(h + 1) * HEAD_DIM] + _mm_hi(cs.seq_sum, kw))
        m_all = jnp.where(lane == h, m_new, m_all)
    mout_ref[...] = _mm_hi(cs.seq_last, m_all)


def _mlstm_sample(proj, smt, gb_row, gb_col, mng, c_all, layer, c_out_prev, n0, m0, lt):
    hq = N_HEADS * HEAD_DIM
    hv = N_HEADS * MLSTM_DV
    ns = CHUNK_ROWS // lt
    t = proj.shape[0]
    small_blk = (2 * hq + 2 * hv) // SMALL_COLS
    state_c = pl.BlockSpec((1, ns, N_HEADS, HEAD_DIM, MLSTM_DV), lambda c: (layer, c, 0, 0, 0))
    state_n = pl.BlockSpec((ns, hq), lambda c: (c, 0))
    state_m = pl.BlockSpec((ns, SMALL_COLS), lambda c: (c, 0))
    in_specs = [pl.BlockSpec((CHUNK_ROWS, hq), lambda c: (c, 0)),
                pl.BlockSpec((CHUNK_ROWS, hq), lambda c: (c, 1)),
                pl.BlockSpec((CHUNK_ROWS, hv), lambda c: (c, 2 * hq // hv)),
                pl.BlockSpec((CHUNK_ROWS, hv), lambda c: (c, 2 * hq // hv + 1)),
                pl.BlockSpec((CHUNK_ROWS, SMALL_COLS), lambda c: (c, small_blk)),
                pl.BlockSpec((1, SMALL_ROWS, CHUNK_ROWS), lambda c: (c, 0, 0)),
                pl.BlockSpec((1, SMALL_COLS), lambda c: (0, 0)),
                pl.BlockSpec((SMALL_ROWS, 1), lambda c: (0, 0)),
                pl.BlockSpec((1, MLSTM_DV), lambda c: (0, 0)),
                state_c, state_n, state_m]
    args = [proj, proj, proj, proj, proj, smt, gb_row, gb_col, mng.reshape(1, MLSTM_DV), c_all, n0, m0]
    aliases = {}
    if c_out_prev is not None:
        aliases = {len(args): 1}
        in_specs.append(pl.BlockSpec(memory_space=pl.ANY))
        args.append(c_out_prev)
    return pl.pallas_call(
        functools.partial(_mlstm_sample_kernel, ns=ns),
        grid=(t // CHUNK_ROWS,),
        in_specs=in_specs,
        out_specs=[pl.BlockSpec((CHUNK_ROWS, hv), lambda c: (c, 0)), state_c, state_n, state_m],
        out_shape=[jax.ShapeDtypeStruct((t, hv), F32),
                   jax.ShapeDtypeStruct(c_all.shape, F32),
                   jax.ShapeDtypeStruct(n0.shape, F32),
                   jax.ShapeDtypeStruct(m0.shape, F32)],
        input_output_aliases=aliases,
        compiler_params=_cparams(("parallel",)),
        name="mlstm_sample",
    )(*args)


def _pad_cols(w, n):
    return jnp.pad(w, ((0, 0), (0, n - w.shape[1])))


def _split_in_weight(w, n_big, n_small):
    n_a = n_big
    small = w[:, n_a:n_a + n_small]
    main = jnp.concatenate([w[:, :n_a], w[:, n_a + n_small:], _pad_cols(small, SMALL_COLS)], axis=1)
    small_t = jnp.pad(small.T, ((0, SMALL_ROWS - n_small), (0, 0)))
    return main.astype(BF16), small_t.astype(BF16)


def _chunk_lanes(smt):
    rows, t = smt.shape
    return smt.reshape(rows, t // CHUNK_ROWS, CHUNK_ROWS).transpose(1, 0, 2)


def _row_col_params(vals):
    row = jnp.stack([jnp.pad(v, (0, SMALL_COLS - v.shape[0])) for v in vals]).astype(F32)
    col = jnp.stack([jnp.pad(v, (0, SMALL_ROWS - v.shape[0])) for v in vals], axis=1).astype(F32)
    return row, col


def _trunk(x, is_prompt, s_gdn, s_conv, s_c, s_n, s_m, norm_g, w_in_even, conv_w, a_log, dt_bias,
           gdn_norm_g, sgu_norm_g, sgu_w, sgu_b, w_out_even, w_in_odd, gate_b_odd, mlstm_norm_g,
           w_out_odd, w_ff1, w_ff2):
    bsz, seq, d = x.shape
    t = bsz * seq
    depth = norm_g.shape[0]
    hq = N_HEADS * HEAD_DIM
    cch = 3 * hq
    tm = 1024 if t % 1024 == 0 else 512
    tb = min(512, seq)
    zeros_h = jnp.zeros((N_HEADS,), F32)
    x2 = x.reshape(t, d)
    new_gdn, new_conv, new_v, new_c, new_n, new_m = [], [], [], [], [], []
    gdn_out = c_out = None
    for l in range(depth):
        gl = norm_g[l]
        if l % 2 == 0:
            e = l // 2
            w_main, w_small_t = _split_in_weight(w_in_even[e], cch + hq, 2 * N_HEADS)
            proj, smt = _proj_in(x2, gl[0], w_main, w_small_t, 512)
            prm_row, prm_col = _row_col_params([jnp.concatenate([zeros_h, jnp.exp(a_log[e])]),
                                                jnp.concatenate([zeros_h, dt_bias[e]])])
            if is_prompt:
                buf8 = jnp.pad(s_conv[e], ((0, 0), (8 - (CONV_W - 1), 0), (0, 0)))
                o_a, s_new, tail = _gdn_prompt(proj, smt, conv_w[e], prm_row, prm_col, gdn_norm_g[e],
                                               s_gdn[e], buf8, bsz, seq, tb)
                o_a = o_a.reshape(t, hq)
                buf_new = tail[:, 8 - (CONV_W - 1):, :]
                w_tiles = sgu_w[e]
                b_cols = _pad_cols(sgu_b[e].T, SMALL_COLS)
                o_b, vb = _sgu(proj, w_tiles, b_cols, sgu_norm_g[e], SGU_CHUNK, 1, 4)
                new_gdn.append(s_new)
            else:
                qkv = proj[:, :cch].reshape(bsz, seq, cch)
                xcat = jnp.concatenate([s_conv[e], qkv], axis=1)
                buf_new = xcat[:, seq:, :]
                xc = _conv_sample(xcat.reshape(bsz, (CONV_W - 1 + seq) * cch), conv_w[e], seq, cch)
                o_a, gdn_out = _gdn_sample(xc.reshape(t, cch), proj, _chunk_lanes(smt), prm_row, prm_col,
                                           gdn_norm_g[e], s_gdn, e, gdn_out, seq)
                ns = CHUNK_ROWS // seq
                w_tiles = jnp.tile(sgu_w[e][:, :seq, :seq], (1, ns, ns))
                b_cols = _pad_cols(jnp.tile(sgu_b[e][:, :seq].T, (ns, 1)), SMALL_COLS)
                o_b, vb = _sgu(proj, w_tiles, b_cols, sgu_norm_g[e], CHUNK_ROWS, ns, 1)
            new_conv.append(buf_new)
            new_v.append(vb.reshape(bsz, seq, hq))
            mix_acts, mix_idx, w_out_all = [o_a, o_b], [(e, 0), (e, 1)], w_out_even
        else:
            o = l // 2
            w_main, w_small_t = _split_in_weight(w_in_odd[o], 2 * hq + N_HEADS * MLSTM_DV, 2 * N_HEADS)
            proj, smt = _proj_in(x2, gl[0], w_main, w_small_t, 512)
            gb_row, gb_col = _row_col_params([gate_b_odd[o]])
            if is_prompt:
                n0 = s_n[o].reshape(bsz, 1, hq)
                m0 = _pad_cols(s_m[o], SMALL_COLS).reshape(bsz, 1, SMALL_COLS)
                hh, c_new, n_new, m_new = _mlstm_prompt(proj, smt, gb_row, gb_col, mlstm_norm_g[o],
                                                        s_c[o], n0, m0, bsz, seq, tb)
                hh = hh.reshape(t, N_HEADS * MLSTM_DV)
                m_new = m_new.reshape(bsz, SMALL_COLS)
                new_c.append(c_new)
            else:
                n0 = s_n[o].reshape(bsz, hq)
                m0 = _pad_cols(s_m[o], SMALL_COLS)
                hh, c_out, n_new, m_new = _mlstm_sample(proj, _chunk_lanes(smt), gb_row, gb_col,
                                                        mlstm_norm_g[o], s_c, o, c_out, n0, m0, seq)
            new_n.append(n_new.reshape(bsz, N_HEADS, HEAD_DIM))
            new_m.append(m_new[:, :N_HEADS])
            mix_acts, mix_idx, w_out_all = [hh], [(o, 0)], w_out_odd
        x2 = _mix_ffn(x2, gl[1:4], mix_acts, w_out_all, mix_idx, w_ff1, w_ff2, l, tm, 1024)
    return (x2.reshape(bsz, seq, d),
            jnp.stack(new_gdn) if is_prompt else gdn_out, jnp.stack(new_conv), jnp.stack(new_v),
            jnp.stack(new_c) if is_prompt else c_out, jnp.stack(new_n), jnp.stack(new_m))


def kernel(x_prompt, x_sample, state_gdn, state_gdn_conv, state_mlstm_c, state_mlstm_n, state_mlstm_m,
           norm_g, w_in_even, conv_w, a_log, dt_bias, gdn_norm_g, sgu_norm_g, sgu_w, sgu_b, w_out_even,
           w_in_odd, gate_b_odd, mlstm_norm_g, w_out_odd, w_ff1, w_ff2):
    weights = (norm_g, w_in_even, conv_w, a_log, dt_bias, gdn_norm_g, sgu_norm_g, sgu_w, sgu_b,
               w_out_even.astype(BF16), w_in_odd, gate_b_odd, mlstm_norm_g, w_out_odd.astype(BF16),
               w_ff1.astype(BF16), w_ff2.astype(BF16))
    bp = x_prompt.shape[0]
    n_even, n_odd = state_gdn.shape[0], state_mlstm_c.shape[0]
    y_prompt, p_gdn, p_conv, _, p_c, p_n, p_m = _trunk(
        x_prompt, True,
        jnp.zeros((n_even, bp) + state_gdn.shape[2:], F32),
        jnp.zeros((n_even, bp) + state_gdn_conv.shape[2:], x_prompt.dtype),
        jnp.zeros((n_odd, bp) + state_mlstm_c.shape[2:], F32),
        jnp.zeros((n_odd, bp) + state_mlstm_n.shape[2:], F32),
        jnp.zeros((n_odd, bp) + state_mlstm_m.shape[2:], F32),
        *weights)
    y_sample, s_gdn, s_conv, s_v, s_c, s_n, s_m = _trunk(
        x_sample, False, state_gdn, state_gdn_conv, state_mlstm_c, state_mlstm_n, state_mlstm_m,
        *weights)
    return (y_prompt, y_sample, p_gdn, s_gdn, p_conv, s_conv, s_v, p_c, s_c, p_n, s_n, p_m, s_m)
```

```python
import functools
import math
from types import SimpleNamespace

import jax
import jax.numpy as jnp
from jax import lax
from jax.experimental import pallas as pl
from jax.experimental.pallas import tpu as pltpu

F32 = jnp.float32
BF16 = jnp.bfloat16
HIGHEST = lax.Precision.HIGHEST

EPS = 1e-6
N_HEADS = 4
HEAD_DIM = 128
MLSTM_DV = 256
CONV_W = 4
CHUNK_ROWS = 64
SGU_CHUNK = 128
SMALL_COLS = 128
SMALL_ROWS = 16
VMEM_LIMIT = 56 * 1024 * 1024


def _cparams(sem):
    return pltpu.CompilerParams(dimension_semantics=sem, vmem_limit_bytes=VMEM_LIMIT)


def _mm(a, b):
    return jnp.dot(a.astype(BF16), b.astype(BF16), preferred_element_type=F32)


def _mm_nt(a, b):
    return lax.dot_general(a.astype(BF16), b.astype(BF16), (((1,), (1,)), ((), ())),
                           preferred_element_type=F32)


def _mm_hi(a, b):
    return jnp.dot(a, b, preferred_element_type=F32, precision=HIGHEST)


def _rms(x, g):
    return x * lax.rsqrt(jnp.mean(x * x, axis=-1, keepdims=True) + EPS) * g


def _l2norm(x):
    return x * lax.rsqrt(jnp.sum(x * x, axis=-1, keepdims=True) + EPS)


def _softplus(x):
    return jnp.maximum(x, 0.0) + jnp.log1p(jnp.exp(-jnp.abs(x)))


def _sigmoid(x):
    return 0.5 * jnp.tanh(0.5 * x) + 0.5


def _silu(x):
    return x * _sigmoid(x)


def _gelu(x):
    return 0.5 * x * (1.0 + lax.erf(x * (2.0 ** -0.5)))


def _proj_in_kernel(x_ref, g_ref, w_ref, wst_ref, o_ref, ot_ref):
    _project_block(x_ref, g_ref, w_ref, wst_ref, o_ref, ot_ref)


def _proj_in(x2d, g, w_main_all, w_small_t_all, layer, tm):
    t, d = x2d.shape
    n = w_main_all.shape[2]
    return pl.pallas_call(
        _proj_in_kernel,
        grid=(t // tm,),
        in_specs=[pl.BlockSpec((tm, d), lambda i: (i, 0)),
                  pl.BlockSpec((1, d), lambda i: (0, 0)),
                  pl.BlockSpec((1, d, n), lambda i: (layer, 0, 0)),
                  pl.BlockSpec((1, SMALL_ROWS, d), lambda i: (layer, 0, 0))],
        out_specs=[pl.BlockSpec((tm, n), lambda i: (i, 0)),
                   pl.BlockSpec((SMALL_ROWS, tm), lambda i: (0, i))],
        out_shape=[jax.ShapeDtypeStruct((t, n), F32),
                   jax.ShapeDtypeStruct((SMALL_ROWS, t), F32)],
        compiler_params=_cparams(("parallel",)),
        name="proj_in",
    )(x2d, g.reshape(1, d), w_main_all, w_small_t_all)


def _mix_ffn_kernel(*refs, n_act):
    x_ref, g_ref = refs[0], refs[1]
    acts = refs[2:2 + n_act]
    wos = refs[2 + n_act:2 + 2 * n_act]
    w1_ref, w2_ref, o_ref, h_ref, acc_ref = refs[2 + 2 * n_act:]
    k = pl.program_id(1)

    @pl.when(k == 0)
    def _():
        mix = None
        for a_ref, w_ref in zip(acts, wos):
            p = jnp.dot(a_ref[...].astype(BF16), w_ref[0], preferred_element_type=F32)
            mix = p if mix is None else mix + p
        x1 = x_ref[...] + _rms(mix, g_ref[0:1, :])
        o_ref[...] = x1
        h_ref[...] = _rms(x1, g_ref[1:2, :]).astype(BF16)
        acc_ref[...] = jnp.zeros_like(acc_ref)

    a = jnp.dot(h_ref[...], w1_ref[0], preferred_element_type=F32)
    a = jnp.square(jnp.maximum(a, 0.0)).astype(BF16)
    acc_ref[...] += jnp.dot(a, w2_ref[0], preferred_element_type=F32)

    @pl.when(k == pl.num_programs(1) - 1)
    def _():
        o_ref[...] = o_ref[...] + _rms(acc_ref[...], g_ref[2:3, :])


def _mix_ffn(x2d, g3rows, acts, w_out_all, w_out_idx, w1_all, w2_all, layer, tm, tf):
    t, d = x2d.shape
    ff = w1_all.shape[2]
    n_act = len(acts)
    in_specs = [pl.BlockSpec((tm, d), lambda i, k: (i, 0)), pl.BlockSpec((3, d), lambda i, k: (0, 0))]
    in_specs += [pl.BlockSpec((tm, a.shape[1]), lambda i, k: (i, 0)) for a in acts]
    in_specs += [pl.BlockSpec((1, a.shape[1], d), lambda i, k, li=li, ri=ri: (li, ri, 0))
                 for a, (li, ri) in zip(acts, w_out_idx)]
    in_specs += [pl.BlockSpec((1, d, tf), lambda i, k: (layer, 0, k)),
                 pl.BlockSpec((1, tf, d), lambda i, k: (layer, k, 0))]
    return pl.pallas_call(
        functools.partial(_mix_ffn_kernel, n_act=n_act),
        grid=(t // tm, ff // tf),
        in_specs=in_specs,
        out_specs=pl.BlockSpec((tm, d), lambda i, k: (i, 0)),
        out_shape=jax.ShapeDtypeStruct((t, d), F32),
        scratch_shapes=[pltpu.VMEM((tm, d), BF16), pltpu.VMEM((tm, d), F32)],
        compiler_params=_cparams(("parallel", "arbitrary")),
        name="mix_ffn",
    )(x2d, g3rows, *acts, *([w_out_all] * n_act), w1_all, w2_all)


def _ind(mask):
    return jnp.where(mask, 1.0, 0.0).astype(F32)


def _div_pow2(x, d):
    return jnp.right_shift(x, int(math.log2(d)))


def _chunk_structs(rows, ns):
    lt = rows // ns
    ri = lax.broadcasted_iota(jnp.int32, (rows, rows), 0)
    ci = lax.broadcasted_iota(jnp.int32, (rows, rows), 1)
    cs = SimpleNamespace(rows=rows, ns=ns, lt=lt)
    if ns == 1:
        cs.same = None
        cs.causal = ri >= ci
        cs.strict = ri > ci
        cs.upper = ri <= ci
    else:
        rseq = _div_pow2(ri, lt)
        cseq = _div_pow2(ci, lt)
        cs.same = rseq == cseq
        cs.causal = cs.same & (ri >= ci)
        cs.strict = cs.same & (ri > ci)
        cs.upper = cs.same & (ri <= ci)
        cs.last = _ind(ci == rseq * lt + (lt - 1))
        cs.last_t = _ind(ri == cseq * lt + (lt - 1))
        sr = _div_pow2(lax.broadcasted_iota(jnp.int32, (ns * HEAD_DIM, rows), 0), HEAD_DIM)
        sc = lax.broadcasted_iota(jnp.int32, (ns * HEAD_DIM, rows), 1)
        cs.stack_last = _ind(sc == sr * lt + (lt - 1))
        cs.stack_mask = _ind(_div_pow2(sc, lt) == sr)
        er = _div_pow2(lax.broadcasted_iota(jnp.int32, (rows, ns * HEAD_DIM), 0), lt)
        ec = _div_pow2(lax.broadcasted_iota(jnp.int32, (rows, ns * HEAD_DIM), 1), HEAD_DIM)
        cs.expand_mask = _ind(er == ec)
        kr = lax.broadcasted_iota(jnp.int32, (ns, rows), 0)
        kc = lax.broadcasted_iota(jnp.int32, (ns, rows), 1)
        cs.seq_last = _ind(kc == kr * lt + (lt - 1))
        cs.seq_sum = _ind(_div_pow2(kc, lt) == kr)
        pr = _div_pow2(lax.broadcasted_iota(jnp.int32, (rows, HEAD_DIM), 0), lt)
        pc = lax.broadcasted_iota(jnp.int32, (rows, HEAD_DIM), 1)
        cs.seq_expand = _ind(pr == pc)
    cs.lower_f = _ind(cs.causal)
    cs.upper_f = _ind(cs.upper)
    eye_r = lax.broadcasted_iota(jnp.int32, (HEAD_DIM, HEAD_DIM), 0)
    eye_c = lax.broadcasted_iota(jnp.int32, (HEAD_DIM, HEAD_DIM), 1)
    cs.eye = _ind(eye_r == eye_c)
    cs.eye_rows = _ind(ri == ci)
    return cs


def _last_rows(cs, x):
    if cs.ns == 1:
        return jnp.broadcast_to(x[cs.rows - 1:cs.rows, :], x.shape)
    return _mm_hi(cs.last, x)


def _last_lanes(cs, x):
    if cs.ns == 1:
        return jnp.broadcast_to(x[:, cs.rows - 1:cs.rows], x.shape)
    return _mm_hi(x, cs.last_t)


def _expand_lhs(cs, x):
    if cs.ns == 1:
        return x
    reps = x.shape[-2] // cs.rows
    mask = cs.expand_mask if reps == 1 else jnp.concatenate([cs.expand_mask] * reps, axis=0)
    return jnp.concatenate([x] * cs.ns, axis=-1) * mask


def _stack_t(cs, x):
    xt = _mm_nt(cs.eye, x)
    if cs.ns == 1:
        return xt
    return jnp.concatenate([xt] * cs.ns, axis=0) * cs.stack_mask


def _stack_scalar(cs, col):
    if cs.ns == 1:
        return col[cs.rows - 1:cs.rows, :]
    return _mm_hi(cs.stack_last, jnp.broadcast_to(col, (cs.rows, HEAD_DIM)))[:, 0:1]


_BATCH_NN = (((2,), (1,)), ((0,), (0,)))
_BATCH_NT = (((2,), (2,)), ((0,), (0,)))
_BATCH_TN = (((1,), (1,)), ((0,), (0,)))


def _bmm(a, b, dims=_BATCH_NN):
    return lax.dot_general(a.astype(BF16), b.astype(BF16), dims, preferred_element_type=F32)


def _split2(a):
    hi = a.astype(BF16)
    return hi, (a - hi.astype(F32)).astype(BF16)


def _bmm_split(ap, bp):
    ah, al = ap
    bh, bl = bp
    f = lambda x, y: lax.dot_general(x, y, _BATCH_NN, preferred_element_type=F32)
    return f(ah, bh) + (f(ah, bl) + f(al, bh))


def _unit_lower_inverse_off(cs, a):
    toff = -a
    p_parts = _split2(toff)
    for _ in range(int(math.log2(cs.lt)) - 1):
        p = _bmm_split(p_parts, p_parts)
        p_parts = _split2(p)
        toff = toff + p + _bmm_split(_split2(toff), p_parts)
    return toff


def _gdn_block(cs, nchunks, xc, gate, sm, smt, prm_row, prm_col, gng, s_heads, write_out):
    rows = cs.rows
    hq = N_HEADS * HEAD_DIM
    pairs = [(c, h) for c in range(nchunks) for h in range(N_HEADS)]
    rs = lambda c: slice(c * rows, (c + 1) * rows)

    def heads(x, col0):
        return jnp.stack([x[rs(c), col0 + h * HEAD_DIM:col0 + (h + 1) * HEAD_DIM] for c, h in pairs])

    def cols(x, lane0):
        return jnp.stack([x[rs(c), lane0 + h:lane0 + h + 1] for c, h in pairs])

    beta_all = _sigmoid(sm)
    g_c = -prm_row[0:1, :] * _softplus(sm + prm_row[1:2, :])
    g_r = -prm_col[:, 0:1] * _softplus(smt + prm_col[:, 1:2])
    gc_c = jnp.concatenate([_mm_hi(cs.lower_f, g_c[rs(c)]) for c in range(nchunks)], axis=0)
    gc_r = jnp.concatenate([_mm_hi(g_r[:, rs(c)], cs.upper_f) for c in range(nchunks)], axis=1)
    gl_c = jnp.concatenate([_last_rows(cs, gc_c[rs(c)]) for c in range(nchunks)], axis=0)
    egc = jnp.exp(gc_c)
    ekd = jnp.exp(gl_c - gc_c)

    q = _l2norm(heads(xc, 0)) * (HEAD_DIM ** -0.5)
    k = _l2norm(heads(xc, hq))
    v = heads(xc, 2 * hq)
    beta = cols(beta_all, 0)
    gcol = cols(gc_c, N_HEADS)
    eg = cols(egc, N_HEADS)
    grow = jnp.stack([gc_r[N_HEADS + h:N_HEADS + h + 1, rs(c)] for c, h in pairs])
    decay = jnp.where(cs.causal, jnp.exp(jnp.where(cs.causal, gcol - grow, 0.0)), 0.0)
    kb = k * beta
    a = jnp.where(cs.strict, _bmm(kb, k, _BATCH_NT) * decay, 0.0)
    toff = _unit_lower_inverse_off(cs, a)
    rhs = jnp.concatenate([v * beta, kb * eg], axis=-1)
    sol = rhs + _bmm_split(_split2(toff), _split2(rhs))
    u_val, w_k = sol[..., :HEAD_DIM], sol[..., HEAD_DIM:]
    qk = jnp.where(cs.causal, _bmm(q, k, _BATCH_NT) * decay, 0.0)
    q_dec = q * eg
    k_dec = k * cols(ekd, N_HEADS)

    s = s_heads
    for c in range(nchunks):
        hs = slice(c * N_HEADS, (c + 1) * N_HEADS)
        p = _bmm(_expand_lhs(cs, jnp.concatenate([w_k[hs], q_dec[hs]], axis=1)), s)
        v_new = u_val[hs] - p[:, :rows]
        o = p[:, rows:] + _bmm(qk[hs], v_new)
        gl_cols = [gl_c[rs(c), N_HEADS + h:N_HEADS + h + 1] for h in range(N_HEADS)]
        g_last = jnp.exp(jnp.stack([_stack_scalar(cs, col) for col in gl_cols]))
        if cs.ns == 1:
            upd = _bmm(k_dec[hs], v_new, _BATCH_TN)
        else:
            kt = _bmm(jnp.broadcast_to(cs.eye, (N_HEADS, HEAD_DIM, HEAD_DIM)), k_dec[hs], _BATCH_NT)
            upd = _bmm(jnp.concatenate([kt] * cs.ns, axis=1) * cs.stack_mask, v_new)
        s = s * g_last + upd
        for h in range(N_HEADS):
            write_out(c, h, _rms(o[h], gng) * _silu(gate[rs(c), h * HEAD_DIM:(h + 1) * HEAD_DIM]))
    return s


def _project_block(x_ref, g_ref, w_ref, wst_ref, proj_w, smt_w):
    hn = _rms(x_ref[...], g_ref[...]).astype(BF16)
    proj_w[...] = jnp.dot(hn, w_ref[0], preferred_element_type=F32)
    smt_w[...] = lax.dot_general(wst_ref[0], hn, (((1,), (1,)), ((), ())), preferred_element_type=F32)


def _pipelined_steps(step):
    i = pl.program_id(0)
    for parity in range(2):
        @pl.when(lax.rem(i, 2) == parity)
        def _():
            step(parity)


def _even_prompt_kernel(x_ref, g_ref, w_ref, wst_ref, cw_ref, prow_ref, pcol_ref, gng_ref, s0_ref, buf_ref,
                        sw_ref, sb_ref, sg_ref, o_ref, sout_ref, bufout_ref,
                        proj_a, proj_b, smt_a, smt_b, s_scr, xp_scr, *, tb, nt):
    i = pl.program_id(0)
    t = lax.rem(jnp.maximum(i - 1, 0), nt)
    hq = N_HEADS * HEAD_DIM
    cch = 3 * hq
    cs = _chunk_structs(CHUNK_ROWS, 1)

    @pl.when(i == 0)
    def _():
        proj_b[...] = jnp.zeros_like(proj_b)
        smt_b[...] = jnp.zeros_like(smt_b)

    @pl.when(t == 0)
    def _():
        s_scr[...] = s0_ref[0]
        xp_scr[...] = buf_ref[0]

    def write_delta(c, h, o):
        o_ref[c * CHUNK_ROWS:(c + 1) * CHUNK_ROWS, h * HEAD_DIM:(h + 1) * HEAD_DIM] = o

    def write_gating(rs, g, val):
        o_ref[rs, hq + g * HEAD_DIM:hq + (g + 1) * HEAD_DIM] = val

    def step(parity):
        proj_w, smt_w, proj_r, smt_r = ((proj_a, smt_a, proj_b, smt_b) if parity == 0
                                        else (proj_b, smt_b, proj_a, smt_a))
        _project_block(x_ref, g_ref, w_ref, wst_ref, proj_w, smt_w)
        x = proj_r[:, 0:cch]
        xp = jnp.concatenate([xp_scr[...], x], axis=0)
        y = cw_ref[CONV_W - 1:CONV_W, :] * x
        for back in range(1, CONV_W):
            y = y + cw_ref[CONV_W - 1 - back:CONV_W - back, :] * pltpu.roll(xp, back, axis=0)[8:]
        xp_scr[...] = x[tb - 8:]
        s_scr[...] = _gdn_block(cs, tb // CHUNK_ROWS, _silu(y), proj_r[:, cch:cch + hq],
                                proj_r[:, cch + 3 * hq:cch + 3 * hq + SMALL_COLS], smt_r[...],
                                prow_ref[...], pcol_ref[...], gng_ref[...], s_scr[...], write_delta)
        _sgu_apply(proj_r[:, cch + hq:cch + 2 * hq], proj_r[:, cch + 2 * hq:cch + 3 * hq],
                   sw_ref, sb_ref[...], sg_ref[...], SGU_CHUNK, 1, write_gating, None)

    _pipelined_steps(step)

    @pl.when(t == nt - 1)
    def _():
        sout_ref[0] = s_scr[...]
        bufout_ref[0] = xp_scr[...]


def _even_prompt(x2, g, w_main_all, w_small_t_all, e, conv_w, prm_row, prm_col, gng, s0, buf8,
                 sgu_w, sgu_b_cols, sgu_g, bsz, seq, tb):
    t, d = x2.shape
    hq = N_HEADS * HEAD_DIM
    cch = 3 * hq
    n_proj = w_main_all.shape[2]
    n = t // tb
    nt = seq // tb
    blk = lambda i: jnp.minimum(i, n - 1)
    seq_of = lambda i: jnp.maximum(i - 1, 0) // nt
    const2 = lambda i: (0, 0)
    return pl.pallas_call(
        functools.partial(_even_prompt_kernel, tb=tb, nt=nt),
        grid=(n + 1,),
        in_specs=[pl.BlockSpec((tb, d), lambda i: (blk(i), 0)),
                  pl.BlockSpec((1, d), const2),
                  pl.BlockSpec((1, d, n_proj), lambda i: (e, 0, 0)),
                  pl.BlockSpec((1, SMALL_ROWS, d), lambda i: (e, 0, 0)),
                  pl.BlockSpec((CONV_W, cch), const2),
                  pl.BlockSpec((2, SMALL_COLS), const2),
                  pl.BlockSpec((SMALL_ROWS, 2), const2),
                  pl.BlockSpec((1, HEAD_DIM), const2),
                  pl.BlockSpec((1, N_HEADS, HEAD_DIM, HEAD_DIM), lambda i: (seq_of(i), 0, 0, 0)),
                  pl.BlockSpec((1, 8, cch), lambda i: (seq_of(i), 0, 0)),
                  pl.BlockSpec((N_HEADS, SGU_CHUNK, SGU_CHUNK), lambda i: (0, 0, 0)),
                  pl.BlockSpec((SGU_CHUNK, SMALL_COLS), const2),
                  pl.BlockSpec((N_HEADS, HEAD_DIM), const2)],
        out_specs=[pl.BlockSpec((tb, 2 * hq), lambda i: (jnp.maximum(i - 1, 0), 0)),
                   pl.BlockSpec((1, N_HEADS, HEAD_DIM, HEAD_DIM), lambda i: (seq_of(i), 0, 0, 0)),
                   pl.BlockSpec((1, 8, cch), lambda i: (seq_of(i), 0, 0))],
        out_shape=[jax.ShapeDtypeStruct((t, 2 * hq), F32),
                   jax.ShapeDtypeStruct((bsz, N_HEADS, HEAD_DIM, HEAD_DIM), F32),
                   jax.ShapeDtypeStruct((bsz, 8, cch), F32)],
        scratch_shapes=[pltpu.VMEM((tb, n_proj), F32), pltpu.VMEM((tb, n_proj), F32),
                        pltpu.VMEM((SMALL_ROWS, tb), F32), pltpu.VMEM((SMALL_ROWS, tb), F32),
                        pltpu.VMEM((N_HEADS, HEAD_DIM, HEAD_DIM), F32),
                        pltpu.VMEM((8, cch), F32)],
        compiler_params=_cparams(("arbitrary",)),
        name="even_prompt",
    )(x2, g.reshape(1, d), w_main_all, w_small_t_all, conv_w, prm_row, prm_col,
      gng.reshape(1, HEAD_DIM), s0, buf8, sgu_w, sgu_b_cols, sgu_g)


def _conv_sample_kernel(xcat_ref, cw_ref, o_ref, *, lt, cch):
    for t in range(lt):
        y = None
        for j in range(CONV_W):
            term = cw_ref[j:j + 1, :] * xcat_ref[:, (t + j) * cch:(t + j + 1) * cch]
            y = term if y is None else y + term
        o_ref[:, t * cch:(t + 1) * cch] = _silu(y)


def _conv_sample(xcat, conv_w, lt, cch):
    nseq = xcat.shape[0]
    return pl.pallas_call(
        functools.partial(_conv_sample_kernel, lt=lt, cch=cch),
        grid=(1,),
        in_specs=[pl.BlockSpec(xcat.shape, lambda i: (0, 0)),
                  pl.BlockSpec((CONV_W, cch), lambda i: (0, 0))],
        out_specs=pl.BlockSpec((nseq, lt * cch), lambda i: (0, 0)),
        out_shape=jax.ShapeDtypeStruct((nseq, lt * cch), F32),
        compiler_params=_cparams(("arbitrary",)),
        name="conv_sample",
    )(xcat, conv_w)


def _gdn_sample_kernel(xc_ref, gate_ref, sm_ref, smt_ref, prow_ref, pcol_ref, gng_ref, s0_ref,
                       *rest, ns):
    o_ref, sout_ref = rest[-2:]
    cs = _chunk_structs(CHUNK_ROWS, ns)
    s_heads = jnp.stack([s0_ref[0, :, h].reshape(ns * HEAD_DIM, HEAD_DIM) for h in range(N_HEADS)])

    def write_out(c, h, o):
        o_ref[:, h * HEAD_DIM:(h + 1) * HEAD_DIM] = o

    s_new = _gdn_block(cs, 1, xc_ref[...], gate_ref[...], sm_ref[...], smt_ref[0],
                       prow_ref[...], pcol_ref[...], gng_ref[...], s_heads, write_out)
    for h in range(N_HEADS):
        sout_ref[0, :, h] = s_new[h].reshape(ns, HEAD_DIM, HEAD_DIM)


def _gdn_sample(xc, proj, smt, prm_row, prm_col, gng, s_all, layer, s_out_prev, lt):
    hq = N_HEADS * HEAD_DIM
    cch = 3 * hq
    ns = CHUNK_ROWS // lt
    nchunk = xc.shape[0] // CHUNK_ROWS
    small_blk = (cch + 3 * hq) // SMALL_COLS
    state_spec = pl.BlockSpec((1, ns, N_HEADS, HEAD_DIM, HEAD_DIM), lambda c: (layer, c, 0, 0, 0))
    in_specs = [pl.BlockSpec((CHUNK_ROWS, cch), lambda c: (c, 0)),
                pl.BlockSpec((CHUNK_ROWS, hq), lambda c: (c, cch // hq)),
                pl.BlockSpec((CHUNK_ROWS, SMALL_COLS), lambda c: (c, small_blk)),
                pl.BlockSpec((1, SMALL_ROWS, CHUNK_ROWS), lambda c: (c, 0, 0)),
                pl.BlockSpec((2, SMALL_COLS), lambda c: (0, 0)),
                pl.BlockSpec((SMALL_ROWS, 2), lambda c: (0, 0)),
                pl.BlockSpec((1, HEAD_DIM), lambda c: (0, 0)),
                state_spec]
    args = [xc, proj, proj, smt, prm_row, prm_col, gng.reshape(1, HEAD_DIM), s_all]
    aliases = {}
    if s_out_prev is not None:
        aliases = {len(args): 1}
        in_specs.append(pl.BlockSpec(memory_space=pl.ANY))
        args.append(s_out_prev)
    return pl.pallas_call(
        functools.partial(_gdn_sample_kernel, ns=ns),
        grid=(nchunk,),
        in_specs=in_specs,
        out_specs=[pl.BlockSpec((CHUNK_ROWS, hq), lambda c: (c, 0)), state_spec],
        out_shape=[jax.ShapeDtypeStruct((xc.shape[0], hq), F32),
                   jax.ShapeDtypeStruct(s_all.shape, F32)],
        input_output_aliases=aliases,
        compiler_params=_cparams(("parallel",)),
        name="gdn_sample",
    )(*args)


def _sgu_apply(u_pre, v_pre, w_ref, b_cols, g_rows, rows, ns, write_o, write_vb):
    lt = rows // ns
    ri = lax.broadcasted_iota(jnp.int32, (rows, rows), 0)
    ci = lax.broadcasted_iota(jnp.int32, (rows, rows), 1)
    keep = ri >= ci
    if ns > 1:
        keep = keep & (_div_pow2(ri, lt) == _div_pow2(ci, lt))
    for g in range(N_HEADS):
        lo = g * HEAD_DIM
        w = jnp.where(keep, w_ref[g], 0.0).astype(BF16)
        u = _gelu(u_pre[:, lo:lo + HEAD_DIM])
        vb = _rms(_gelu(v_pre[:, lo:lo + HEAD_DIM]), g_rows[g:g + 1, :])
        if write_vb is not None:
            write_vb(g, vb)
        for c in range(u_pre.shape[0] // rows):
            rs = slice(c * rows, (c + 1) * rows)
            write_o(rs, g, u[rs] * (_mm(w, vb[rs]) + b_cols[:, g:g + 1]))


def _sgu_kernel(u_ref, v_ref, w_ref, b_ref, g_ref, o_ref, vb_ref, *, rows, ns):
    def write_o(rs, g, val):
        o_ref[rs, g * HEAD_DIM:(g + 1) * HEAD_DIM] = val

    def write_vb(g, val):
        vb_ref[:, g * HEAD_DIM:(g + 1) * HEAD_DIM] = val

    _sgu_apply(u_ref[...], v_ref[...], w_ref, b_ref[...], g_ref[...], rows, ns, write_o, write_vb)


def _sgu(proj, w_tiles, b_cols, norm_g, rows, ns, nchunks):
    hq = N_HEADS * HEAD_DIM
    t = proj.shape[0]
    u_blk = (3 * hq + hq) // hq
    blk = rows * nchunks
    return pl.pallas_call(
        functools.partial(_sgu_kernel, rows=rows, ns=ns),
        grid=(t // blk,),
        in_specs=[pl.BlockSpec((blk, hq), lambda i: (i, u_blk)),
                  pl.BlockSpec((blk, hq), lambda i: (i, u_blk + 1)),
                  pl.BlockSpec((N_HEADS, rows, rows), lambda i: (0, 0, 0)),
                  pl.BlockSpec((rows, SMALL_COLS), lambda i: (0, 0)),
                  pl.BlockSpec((N_HEADS, HEAD_DIM), lambda i: (0, 0))],
        out_specs=[pl.BlockSpec((blk, hq), lambda i: (i, 0)),
                   pl.BlockSpec((blk, hq), lambda i: (i, 0))],
        out_shape=[jax.ShapeDtypeStruct((t, hq), F32), jax.ShapeDtypeStruct((t, hq), F32)],
        compiler_params=_cparams(("parallel",)),
        name="sgu",
    )(proj, proj, w_tiles, b_cols, norm_g)


def _logsigmoid(x):
    return jnp.minimum(x, 0.0) - jnp.log1p(jnp.exp(-jnp.abs(x)))


def _mlstm_chunk(cs, q_all, k_all, v_all, opre, sm, smt, gb_row, gb_col, mng, c_list, n_rows, m_rows):
    neg_inf = -jnp.inf
    pre_c = sm + gb_row
    pre_r = smt + gb_col
    b_c = _mm_hi(cs.lower_f, _logsigmoid(pre_c))
    b_r = _mm_hi(_logsigmoid(pre_r), cs.upper_f)
    bl_c = _last_rows(cs, b_c)
    bl_r = _last_lanes(cs, b_r)
    res = []
    for h in range(N_HEADS):
        lo = h * HEAD_DIM
        vo = h * MLSTM_DV
        q = q_all[:, lo:lo + HEAD_DIM] * (HEAD_DIM ** -0.5)
        k = k_all[:, lo:lo + HEAD_DIM]
        v = v_all[:, vo:vo + MLSTM_DV]
        bcol = b_c[:, N_HEADS + h:N_HEADS + h + 1]
        brow = b_r[N_HEADS + h:N_HEADS + h + 1, :]
        icol = pre_c[:, h:h + 1]
        irow = pre_r[h:h + 1, :]
        blcol = bl_c[:, N_HEADS + h:N_HEADS + h + 1]
        blrow = bl_r[N_HEADS + h:N_HEADS + h + 1, :]
        mrow = m_rows[:, h:h + 1]
        inter = bcol + mrow
        dmat = jnp.where(cs.causal, bcol - brow + irow, neg_inf)
        m_t = jnp.maximum(inter, jnp.max(dmat, axis=-1, keepdims=True))
        w_intra = jnp.exp(dmat - m_t)
        w_inter = jnp.exp(inter - m_t)
        s = _mm_nt(q, k) * w_intra
        c_old = c_list[h]
        num = w_inter * _mm(_expand_lhs(cs, q), c_old) + _mm(s, v)
        qn = jnp.sum(q * n_rows[:, lo:lo + HEAD_DIM], axis=-1, keepdims=True)
        den = w_inter * qn + jnp.sum(s, axis=-1, keepdims=True)
        hval = num / jnp.maximum(jnp.abs(den), jnp.exp(-m_t))
        out = _rms(hval, mng) * _sigmoid(opre[:, vo:vo + MLSTM_DV])
        logw_row = blrow - brow + irow
        if cs.ns == 1:
            seq_max = jnp.max(logw_row, axis=-1, keepdims=True)
        else:
            seq_max = jnp.max(jnp.where(cs.same, logw_row, neg_inf), axis=-1, keepdims=True)
        m_new = jnp.maximum(blcol + mrow, seq_max)
        keep = jnp.exp(blcol + mrow - m_new)
        wk = jnp.exp(blcol - bcol + icol - m_new)
        kw = k * wk
        c_new = c_old * _stack_scalar(cs, keep) + _mm(_stack_t(cs, kw), v)
        res.append((out, c_new, keep, m_new, kw))
    return res


def _mlstm_prompt_block(cs, nchunks, q_all, k_all, v_all, opre, sm, smt, gb_row, gb_col, mng,
                        c_heads, n_heads, m_heads, write_out):
    rows = cs.rows
    pairs = [(c, h) for c in range(nchunks) for h in range(N_HEADS)]
    rs = lambda c: slice(c * rows, (c + 1) * rows)

    def heads(x, width):
        return jnp.stack([x[rs(c), h * width:(h + 1) * width] for c, h in pairs])

    def cols(x, lane0):
        return jnp.stack([x[rs(c), lane0 + h:lane0 + h + 1] for c, h in pairs])

    def lanes(x, row0):
        return jnp.stack([x[row0 + h:row0 + h + 1, rs(c)] for c, h in pairs])

    pre_c = sm + gb_row
    pre_r = smt + gb_col
    lf_c = _logsigmoid(pre_c)
    lf_r = _logsigmoid(pre_r)
    b_c = jnp.concatenate([_mm_hi(cs.lower_f, lf_c[rs(c)]) for c in range(nchunks)], axis=0)
    b_r = jnp.concatenate([_mm_hi(lf_r[:, rs(c)], cs.upper_f) for c in range(nchunks)], axis=1)

    q = heads(q_all, HEAD_DIM) * (HEAD_DIM ** -0.5)
    k = heads(k_all, HEAD_DIM)
    v = heads(v_all, MLSTM_DV)
    bcol = cols(b_c, N_HEADS)
    icol = cols(pre_c, 0)
    brow = lanes(b_r, N_HEADS)
    irow = lanes(pre_r, 0)
    blast = bcol[:, rows - 1:rows, :]
    dmat = jnp.where(cs.causal, bcol - brow + irow, -jnp.inf)
    rmax = jnp.max(dmat, axis=-1, keepdims=True)
    sp = _bmm(q, k, _BATCH_NT) * jnp.exp(dmat - rmax)
    sv = _bmm(sp, v)
    ssum = jnp.sum(sp, axis=-1, keepdims=True)
    lmax = jnp.max(blast - brow + irow, axis=-1, keepdims=True)
    kwp = k * jnp.exp(blast - bcol + icol - lmax)
    upd = _bmm(kwp, v, _BATCH_TN)
    nsum = jnp.sum(kwp, axis=1, keepdims=True)

    cst, n, m = c_heads, n_heads, m_heads
    for c in range(nchunks):
        hs = slice(c * N_HEADS, (c + 1) * N_HEADS)
        inter = bcol[hs] + m
        m_t = jnp.maximum(inter, rmax[hs])
        w_inter = jnp.exp(inter - m_t)
        w_intra = jnp.exp(rmax[hs] - m_t)
        num = w_inter * _bmm(q[hs], cst) + w_intra * sv[hs]
        den = (w_inter * jnp.sum(q[hs] * n, axis=-1, keepdims=True) + w_intra * ssum[hs])
        hval = num / jnp.maximum(jnp.abs(den), jnp.exp(-m_t))
        for h in range(N_HEADS):
            write_out(c, h, _rms(hval[h], mng) * _sigmoid(opre[rs(c), h * MLSTM_DV:(h + 1) * MLSTM_DV]))
        m_new = jnp.maximum(blast[hs] + m, lmax[hs])
        keep = jnp.exp(blast[hs] + m - m_new)
        fresh = jnp.exp(lmax[hs] - m_new)
        cst = keep * cst + fresh * upd[hs]
        n = keep * n + fresh * nsum[hs]
        m = m_new
    return cst, n, m


def _odd_prompt_kernel(x_ref, g_ref, w_ref, wst_ref, gbr_ref, gbc_ref, mng_ref, c0_ref, n0_ref, m0_ref,
                       o_ref, cout_ref, nout_ref, mout_ref,
                       proj_a, proj_b, smt_a, smt_b, c_scr, n_scr, m_scr, *, tb, nt):
    i = pl.program_id(0)
    t = lax.rem(jnp.maximum(i - 1, 0), nt)
    hq = N_HEADS * HEAD_DIM
    hv = N_HEADS * MLSTM_DV
    cs = _chunk_structs(CHUNK_ROWS, 1)
    lane = lax.broadcasted_iota(jnp.int32, (1, SMALL_COLS), 1)

    @pl.when(i == 0)
    def _():
        proj_b[...] = jnp.zeros_like(proj_b)
        smt_b[...] = jnp.zeros_like(smt_b)

    @pl.when(t == 0)
    def _():
        c_scr[...] = c0_ref[0]
        n_scr[...] = n0_ref[0]
        m_scr[...] = m0_ref[0]

    def write_out(c, h, o):
        o_ref[c * CHUNK_ROWS:(c + 1) * CHUNK_ROWS, h * MLSTM_DV:(h + 1) * MLSTM_DV] = o

    def step(parity):
        proj_w, smt_w, proj_r, smt_r = ((proj_a, smt_a, proj_b, smt_b) if parity == 0
                                        else (proj_b, smt_b, proj_a, smt_a))
        _project_block(x_ref, g_ref, w_ref, wst_ref, proj_w, smt_w)
        n_cur = n_scr[...]
        m_cur = m_scr[...]
        n_heads = jnp.stack([n_cur[:, h * HEAD_DIM:(h + 1) * HEAD_DIM] for h in range(N_HEADS)])
        m_heads = jnp.stack([m_cur[:, h:h + 1] for h in range(N_HEADS)])
        c_new, n_heads, m_heads = _mlstm_prompt_block(
            cs, tb // CHUNK_ROWS, proj_r[:, 0:hq], proj_r[:, hq:2 * hq], proj_r[:, 2 * hq:2 * hq + hv],
            proj_r[:, 2 * hq + hv:2 * hq + 2 * hv],
            proj_r[:, 2 * hq + 2 * hv:2 * hq + 2 * hv + SMALL_COLS], smt_r[...],
            gbr_ref[...], gbc_ref[...], mng_ref[...], c_scr[...], n_heads, m_heads, write_out)
        c_scr[...] = c_new
        n_scr[...] = jnp.concatenate([n_heads[h] for h in range(N_HEADS)], axis=1)
        for h in range(N_HEADS):
            m_cur = jnp.where(lane == h, m_heads[h], m_cur)
        m_scr[...] = m_cur

    _pipelined_steps(step)

    @pl.when(t == nt - 1)
    def _():
        cout_ref[0] = c_scr[...]
        nout_ref[0] = n_scr[...]
        mout_ref[0] = m_scr[...]


def _odd_prompt(x2, g, w_main_all, w_small_t_all, o, gb_row, gb_col, mng, c0, n0, m0, bsz, seq, tb):
    t, d = x2.shape
    hq = N_HEADS * HEAD_DIM
    hv = N_HEADS * MLSTM_DV
    n_proj = w_main_all.shape[2]
    n = t // tb
    nt = seq // tb
    blk = lambda i: jnp.minimum(i, n - 1)
    seq_of = lambda i: jnp.maximum(i - 1, 0) // nt
    const2 = lambda i: (0, 0)
    state_c = pl.BlockSpec((1, N_HEADS, HEAD_DIM, MLSTM_DV), lambda i: (seq_of(i), 0, 0, 0))
    state_n = pl.BlockSpec((1, 1, hq), lambda i: (seq_of(i), 0, 0))
    state_m = pl.BlockSpec((1, 1, SMALL_COLS), lambda i: (seq_of(i), 0, 0))
    return pl.pallas_call(
        functools.partial(_odd_prompt_kernel, tb=tb, nt=nt),
        grid=(n + 1,),
        in_specs=[pl.BlockSpec((tb, d), lambda i: (blk(i), 0)),
                  pl.BlockSpec((1, d), const2),
                  pl.BlockSpec((1, d, n_proj), lambda i: (o, 0, 0)),
                  pl.BlockSpec((1, SMALL_ROWS, d), lambda i: (o, 0, 0)),
                  pl.BlockSpec((1, SMALL_COLS), const2),
                  pl.BlockSpec((SMALL_ROWS, 1), const2),
                  pl.BlockSpec((1, MLSTM_DV), const2),
                  state_c, state_n, state_m],
        out_specs=[pl.BlockSpec((tb, hv), lambda i: (jnp.maximum(i - 1, 0), 0)),
                   state_c, state_n, state_m],
        out_shape=[jax.ShapeDtypeStruct((t, hv), F32),
                   jax.ShapeDtypeStruct(c0.shape, F32),
                   jax.ShapeDtypeStruct(n0.shape, F32),
                   jax.ShapeDtypeStruct(m0.shape, F32)],
        scratch_shapes=[pltpu.VMEM((tb, n_proj), F32), pltpu.VMEM((tb, n_proj), F32),
                        pltpu.VMEM((SMALL_ROWS, tb), F32), pltpu.VMEM((SMALL_ROWS, tb), F32),
                        pltpu.VMEM((N_HEADS, HEAD_DIM, MLSTM_DV), F32),
                        pltpu.VMEM((1, hq), F32),
                        pltpu.VMEM((1, SMALL_COLS), F32)],
        compiler_params=_cparams(("arbitrary",)),
        name="odd_prompt",
    )(x2, g.reshape(1, d), w_main_all, w_small_t_all, gb_row, gb_col, mng.reshape(1, MLSTM_DV), c0, n0, m0)


def _mlstm_sample_kernel(q_ref, k_ref, v_ref, op_ref, sm_ref, smt_ref, gbr_ref, gbc_ref, mng_ref,
                         c0_ref, n0_ref, m0_ref, *rest, ns):
    o_ref, cout_ref, nout_ref, mout_ref = rest[-4:]
    cs = _chunk_structs(CHUNK_ROWS, ns)
    lane = lax.broadcasted_iota(jnp.int32, (1, SMALL_COLS), 1)
    c_list = [c0_ref[0, :, h].reshape(ns * HEAD_DIM, MLSTM_DV) for h in range(N_HEADS)]
    n0 = n0_ref[...]
    m0 = m0_ref[...]
    n_rows = _mm_hi(cs.seq_expand, jnp.concatenate(
        [n0, jnp.zeros((HEAD_DIM - ns, n0.shape[1]), F32)], axis=0))
    m_rows = _mm_hi(cs.seq_expand, jnp.concatenate(
        [m0, jnp.zeros((HEAD_DIM - ns, m0.shape[1]), F32)], axis=0))
    res = _mlstm_chunk(cs, q_ref[...], k_ref[...], v_ref[...], op_ref[...], sm_ref[...], smt_ref[0],
                       gbr_ref[...], gbc_ref[...], mng_ref[...], c_list, n_rows, m_rows)
    m_all = jnp.zeros((CHUNK_ROWS, SMALL_COLS), F32)
    for h, (out, c_new, keep, m_new, kw) in enumerate(res):
        o_ref[:, h * MLSTM_DV:(h + 1) * MLSTM_DV] = out
        cout_ref[0, :, h] = c_new.reshape(ns, HEAD_DIM, MLSTM_DV)
        keep_seq = _mm_hi(cs.seq_last, jnp.broadcast_to(keep, (CHUNK_ROWS, HEAD_DIM)))
        nout_ref[:, h * HEAD_DIM:(h + 1) * HEAD_DIM] = (
            keep_seq * n0[:, h * HEAD_DIM:(h + 1) * HEAD_DIM] + _mm_hi(cs.seq_sum, kw))
        m_all = jnp.where(lane == h, m_new, m_all)
    mout_ref[...] = _mm_hi(cs.seq_last, m_all)


def _mlstm_sample(proj, smt, gb_row, gb_col, mng, c_all, layer, c_out_prev, n0, m0, lt):
    hq = N_HEADS * HEAD_DIM
    hv = N_HEADS * MLSTM_DV
    ns = CHUNK_ROWS // lt
    t = proj.shape[0]
    small_blk = (2 * hq + 2 * hv) // SMALL_COLS
    state_c = pl.BlockSpec((1, ns, N_HEADS, HEAD_DIM, MLSTM_DV), lambda c: (layer, c, 0, 0, 0))
    state_n = pl.BlockSpec((ns, hq), lambda c: (c, 0))
    state_m = pl.BlockSpec((ns, SMALL_COLS), lambda c: (c, 0))
    in_specs = [pl.BlockSpec((CHUNK_ROWS, hq), lambda c: (c, 0)),
                pl.BlockSpec((CHUNK_ROWS, hq), lambda c: (c, 1)),
                pl.BlockSpec((CHUNK_ROWS, hv), lambda c: (c, 2 * hq // hv)),
                pl.BlockSpec((CHUNK_ROWS, hv), lambda c: (c, 2 * hq // hv + 1)),
                pl.BlockSpec((CHUNK_ROWS, SMALL_COLS), lambda c: (c, small_blk)),
                pl.BlockSpec((1, SMALL_ROWS, CHUNK_ROWS), lambda c: (c, 0, 0)),
                pl.BlockSpec((1, SMALL_COLS), lambda c: (0, 0)),
                pl.BlockSpec((SMALL_ROWS, 1), lambda c: (0, 0)),
                pl.BlockSpec((1, MLSTM_DV), lambda c: (0, 0)),
                state_c, state_n, state_m]
    args = [proj, proj, proj, proj, proj, smt, gb_row, gb_col, mng.reshape(1, MLSTM_DV), c_all, n0, m0]
    aliases = {}
    if c_out_prev is not None:
        aliases = {len(args): 1}
        in_specs.append(pl.BlockSpec(memory_space=pl.ANY))
        args.append(c_out_prev)
    return pl.pallas_call(
        functools.partial(_mlstm_sample_kernel, ns=ns),
        grid=(t // CHUNK_ROWS,),
        in_specs=in_specs,
        out_specs=[pl.BlockSpec((CHUNK_ROWS, hv), lambda c: (c, 0)), state_c, state_n, state_m],
        out_shape=[jax.ShapeDtypeStruct((t, hv), F32),
                   jax.ShapeDtypeStruct(c_all.shape, F32),
                   jax.ShapeDtypeStruct(n0.shape, F32),
                   jax.ShapeDtypeStruct(m0.shape, F32)],
        input_output_aliases=aliases,
        compiler_params=_cparams(("parallel",)),
        name="mlstm_sample",
    )(*args)


def _pad_cols(w, n):
    return jnp.pad(w, ((0, 0), (0, n - w.shape[1])))


def _split_in_weight(w, n_a, n_small):
    small = w[:, :, n_a:n_a + n_small]
    pad = jnp.zeros(w.shape[:2] + (SMALL_COLS - n_small,), w.dtype)
    main = jnp.concatenate([w[:, :, :n_a], w[:, :, n_a + n_small:], small, pad], axis=2)
    small_t = jnp.pad(jnp.swapaxes(small, 1, 2), ((0, 0), (0, SMALL_ROWS - n_small), (0, 0)))
    return main.astype(BF16), small_t.astype(BF16)


def _chunk_lanes(smt):
    rows, t = smt.shape
    return smt.reshape(rows, t // CHUNK_ROWS, CHUNK_ROWS).transpose(1, 0, 2)


def _row_col_params(vals):
    row = jnp.stack([jnp.pad(v, (0, SMALL_COLS - v.shape[0])) for v in vals]).astype(F32)
    col = jnp.stack([jnp.pad(v, (0, SMALL_ROWS - v.shape[0])) for v in vals], axis=1).astype(F32)
    return row, col


def _trunk(x, is_prompt, s_gdn, s_conv, s_c, s_n, s_m, norm_g, w_in_even, conv_w, a_log, dt_bias,
           gdn_norm_g, sgu_norm_g, sgu_w, sgu_b, w_out_even, w_in_odd, gate_b_odd, mlstm_norm_g,
           w_out_odd, w_ff1, w_ff2):
    bsz, seq, d = x.shape
    t = bsz * seq
    depth = norm_g.shape[0]
    hq = N_HEADS * HEAD_DIM
    cch = 3 * hq
    tm = 1024 if t % 1024 == 0 else 512
    tb = min(512, seq)
    zeros_h = jnp.zeros((N_HEADS,), F32)
    x2 = x.reshape(t, d)
    new_gdn, new_conv, new_v, new_c, new_n, new_m = [], [], [], [], [], []
    gdn_out = c_out = None
    for l in range(depth):
        gl = norm_g[l]
        if l % 2 == 0:
            e = l // 2
            prm_row, prm_col = _row_col_params([jnp.concatenate([zeros_h, jnp.exp(a_log[e])]),
                                                jnp.concatenate([zeros_h, dt_bias[e]])])
            if is_prompt:
                buf8 = jnp.pad(s_conv[e], ((0, 0), (8 - (CONV_W - 1), 0), (0, 0)))
                o_mix, s_new, tail = _even_prompt(
                    x2, gl[0], *w_in_even, e, conv_w[e], prm_row, prm_col, gdn_norm_g[e], s_gdn[e], buf8,
                    sgu_w[e], _pad_cols(sgu_b[e].T, SMALL_COLS), sgu_norm_g[e], bsz, seq, tb)
                buf_new = tail[:, 8 - (CONV_W - 1):, :]
                new_gdn.append(s_new)
                mix_acts, mix_idx = [o_mix], [(e, 0)]
            else:
                proj, smt = _proj_in(x2, gl[0], *w_in_even, e, 512)
                qkv = proj[:, :cch].reshape(bsz, seq, cch)
                xcat = jnp.concatenate([s_conv[e], qkv], axis=1)
                buf_new = xcat[:, seq:, :]
                xc = _conv_sample(xcat.reshape(bsz, (CONV_W - 1 + seq) * cch), conv_w[e], seq, cch)
                o_a, gdn_out = _gdn_sample(xc.reshape(t, cch), proj, _chunk_lanes(smt), prm_row, prm_col,
                                           gdn_norm_g[e], s_gdn, e, gdn_out, seq)
                ns = CHUNK_ROWS // seq
                w_tiles = jnp.tile(sgu_w[e][:, :seq, :seq], (1, ns, ns))
                b_cols = _pad_cols(jnp.tile(sgu_b[e][:, :seq].T, (ns, 1)), SMALL_COLS)
                o_b, vb = _sgu(proj, w_tiles, b_cols, sgu_norm_g[e], CHUNK_ROWS, ns, 1)
                new_v.append(vb.reshape(bsz, seq, hq))
                mix_acts, mix_idx = [o_a, o_b], [(e, 0), (e, 1)]
            new_conv.append(buf_new)
            w_out_all = w_out_even
        else:
            o = l // 2
            gb_row, gb_col = _row_col_params([gate_b_odd[o]])
            if is_prompt:
                n0 = s_n[o].reshape(bsz, 1, hq)
                m0 = _pad_cols(s_m[o], SMALL_COLS).reshape(bsz, 1, SMALL_COLS)
                hh, c_new, n_new, m_new = _odd_prompt(x2, gl[0], *w_in_odd, o, gb_row, gb_col,
                                                      mlstm_norm_g[o], s_c[o], n0, m0, bsz, seq, tb)
                m_new = m_new.reshape(bsz, SMALL_COLS)
                new_c.append(c_new)
            else:
                proj, smt = _proj_in(x2, gl[0], *w_in_odd, o, 512)
                n0 = s_n[o].reshape(bsz, hq)
                m0 = _pad_cols(s_m[o], SMALL_COLS)
                hh, c_out, n_new, m_new = _mlstm_sample(proj, _chunk_lanes(smt), gb_row, gb_col,
                                                        mlstm_norm_g[o], s_c, o, c_out, n0, m0, seq)
            new_n.append(n_new.reshape(bsz, N_HEADS, HEAD_DIM))
            new_m.append(m_new[:, :N_HEADS])
            mix_acts, mix_idx, w_out_all = [hh], [(o, 0)], w_out_odd
        x2 = _mix_ffn(x2, gl[1:4], mix_acts, w_out_all, mix_idx, w_ff1, w_ff2, l, tm, 1024)
    return (x2.reshape(bsz, seq, d),
            jnp.stack(new_gdn) if is_prompt else gdn_out, jnp.stack(new_conv),
            None if is_prompt else jnp.stack(new_v),
            jnp.stack(new_c) if is_prompt else c_out, jnp.stack(new_n), jnp.stack(new_m))


def kernel(x_prompt, x_sample, state_gdn, state_gdn_conv, state_mlstm_c, state_mlstm_n, state_mlstm_m,
           norm_g, w_in_even, conv_w, a_log, dt_bias, gdn_norm_g, sgu_norm_g, sgu_w, sgu_b, w_out_even,
           w_in_odd, gate_b_odd, mlstm_norm_g, w_out_odd, w_ff1, w_ff2):
    hq = N_HEADS * HEAD_DIM
    weights = (norm_g, _split_in_weight(w_in_even, 4 * hq, 2 * N_HEADS), conv_w, a_log, dt_bias,
               gdn_norm_g, sgu_norm_g, sgu_w, sgu_b, w_out_even.astype(BF16),
               _split_in_weight(w_in_odd, 2 * hq + N_HEADS * MLSTM_DV, 2 * N_HEADS), gate_b_odd,
               mlstm_norm_g, w_out_odd.astype(BF16), w_ff1.astype(BF16), w_ff2.astype(BF16))
    bp = x_prompt.shape[0]
    n_even, n_odd = state_gdn.shape[0], state_mlstm_c.shape[0]
    y_prompt, p_gdn, p_conv, _, p_c, p_n, p_m = _trunk(
        x_prompt, True,
        jnp.zeros((n_even, bp) + state_gdn.shape[2:], F32),
        jnp.zeros((n_even, bp) + state_gdn_conv.shape[2:], x_prompt.dtype),
        jnp.zeros((n_odd, bp) + state_mlstm_c.shape[2:], F32),
        jnp.zeros((n_odd, bp) + state_mlstm_n.shape[2:], F32),
        jnp.zeros((n_odd, bp) + state_mlstm_m.shape[2:], F32),
        *weights)
    y_sample, s_gdn, s_conv, s_v, s_c, s_n, s_m = _trunk(
        x_sample, False, state_gdn, state_gdn_conv, state_mlstm_c, state_mlstm_n, state_mlstm_m,
        *weights)
    return (y_prompt, y_sample, p_gdn, s_gdn, p_conv, s_conv, s_v, p_c, s_c, p_n, s_n, p_m, s_m)
```

```python
import functools
import math
from types import SimpleNamespace

import jax
import jax.numpy as jnp
from jax import lax
from jax.experimental import pallas as pl
from jax.experimental.pallas import tpu as pltpu

F32 = jnp.float32
BF16 = jnp.bfloat16

EPS = 1e-6
N_HEADS = 4
HEAD_DIM = 128
MLSTM_DV = 256
CONV_W = 4
CHUNK_ROWS = 64
SGU_CHUNK = 128
SMALL_COLS = 128
SMALL_ROWS = 16
VMEM_LIMIT = 56 * 1024 * 1024


def _cparams(sem):
    return pltpu.CompilerParams(dimension_semantics=sem, vmem_limit_bytes=VMEM_LIMIT)


def _mm(a, b):
    return jnp.dot(a.astype(BF16), b.astype(BF16), preferred_element_type=F32)


def _mm_nt(a, b):
    return lax.dot_general(a.astype(BF16), b.astype(BF16), (((1,), (1,)), ((), ())),
                           preferred_element_type=F32)


def _trunc_bf16(x):
    bits = lax.bitcast_convert_type(x, jnp.uint32) & jnp.uint32(0xFFFF0000)
    return lax.bitcast_convert_type(bits, F32)


def _split3(x):
    hi = _trunc_bf16(x)
    rest = x - hi
    mid = _trunc_bf16(rest)
    return hi, mid, rest - mid


def _sel_mm(sel, x):
    pieces = jnp.concatenate(_split3(x), axis=0).astype(BF16)
    return jnp.dot(jnp.concatenate([sel.astype(BF16)] * 3, axis=1), pieces, preferred_element_type=F32)


def _mm_sel(x, sel):
    pieces = jnp.concatenate(_split3(x), axis=1).astype(BF16)
    return jnp.dot(pieces, jnp.concatenate([sel.astype(BF16)] * 3, axis=0), preferred_element_type=F32)


def _rms(x, g):
    return x * lax.rsqrt(jnp.mean(x * x, axis=-1, keepdims=True) + EPS) * g


def _l2norm(x):
    return x * lax.rsqrt(jnp.sum(x * x, axis=-1, keepdims=True) + EPS)


def _softplus(x):
    return jnp.maximum(x, 0.0) + jnp.log1p(jnp.exp(-jnp.abs(x)))


def _sigmoid(x):
    return 0.5 * jnp.tanh(0.5 * x) + 0.5


def _silu(x):
    return x * _sigmoid(x)


def _gelu(x):
    return 0.5 * x * (1.0 + lax.erf(x * (2.0 ** -0.5)))


def _proj_in_kernel(x_ref, g_ref, w_ref, wst_ref, o_ref, ot_ref):
    _project_block(x_ref, g_ref, w_ref, wst_ref, o_ref, ot_ref)


def _proj_in(x2d, g, w_main_all, w_small_t_all, layer, tm):
    t, d = x2d.shape
    n = w_main_all.shape[2]
    return pl.pallas_call(
        _proj_in_kernel,
        grid=(t // tm,),
        in_specs=[pl.BlockSpec((tm, d), lambda i: (i, 0)),
                  pl.BlockSpec((1, d), lambda i: (0, 0)),
                  pl.BlockSpec((1, d, n), lambda i: (layer, 0, 0)),
                  pl.BlockSpec((1, SMALL_ROWS, d), lambda i: (layer, 0, 0))],
        out_specs=[pl.BlockSpec((tm, n), lambda i: (i, 0)),
                   pl.BlockSpec((SMALL_ROWS, tm), lambda i: (0, i))],
        out_shape=[jax.ShapeDtypeStruct((t, n), F32),
                   jax.ShapeDtypeStruct((SMALL_ROWS, t), F32)],
        compiler_params=_cparams(("parallel",)),
        name="proj_in",
    )(x2d, g.reshape(1, d), w_main_all, w_small_t_all)


def _mix_ffn_kernel(*refs, n_act, nk):
    x_ref, g_ref = refs[0], refs[1]
    acts = refs[2:2 + n_act]
    wos = refs[2 + n_act:2 + 2 * n_act]
    w1_ref, w2_ref, o_ref, h_ref, acc_ref = refs[2 + 2 * n_act:]
    k = pl.program_id(1)
    tm = x_ref.shape[0]
    halves = [slice(0, tm // 2), slice(tm // 2, tm)] if tm >= 512 else [slice(0, tm)]

    def prologue(rs):
        mix = None
        for a_ref, w_ref in zip(acts, wos):
            p = jnp.dot(a_ref[rs, :].astype(BF16), w_ref[0], preferred_element_type=F32)
            mix = p if mix is None else mix + p
        x1 = x_ref[rs, :] + _rms(mix, g_ref[0:1, :])
        o_ref[rs, :] = x1
        h_ref[rs, :] = _rms(x1, g_ref[1:2, :]).astype(BF16)

    def chunk(rs, first):
        a = jnp.dot(h_ref[rs, :], w1_ref[0], preferred_element_type=F32)
        a = jnp.square(jnp.maximum(a, 0.0)).astype(BF16)
        part = jnp.dot(a, w2_ref[0], preferred_element_type=F32)
        acc_ref[rs, :] = part if first else acc_ref[rs, :] + part

    def epilogue(rs):
        o_ref[rs, :] = o_ref[rs, :] + _rms(acc_ref[rs, :], g_ref[2:3, :])

    @pl.when(k == 0)
    def _():
        for rs in halves:
            prologue(rs)
            chunk(rs, True)
        if nk == 1:
            for rs in halves:
                epilogue(rs)

    @pl.when((k > 0) & (k < nk - 1))
    def _():
        chunk(slice(0, tm), False)

    @pl.when((k > 0) & (k == nk - 1))
    def _():
        for rs in halves:
            chunk(rs, False)
            epilogue(rs)


def _mix_ffn(x2d, g3rows, acts, w_out_all, w_out_idx, w1_all, w2_all, layer, tm, tf):
    t, d = x2d.shape
    ff = w1_all.shape[2]
    n_act = len(acts)
    in_specs = [pl.BlockSpec((tm, d), lambda i, k: (i, 0)), pl.BlockSpec((3, d), lambda i, k: (0, 0))]
    in_specs += [pl.BlockSpec((tm, a.shape[1]), lambda i, k: (i, 0)) for a in acts]
    in_specs += [pl.BlockSpec((1, a.shape[1], d), lambda i, k, li=li, ri=ri: (li, ri, 0))
                 for a, (li, ri) in zip(acts, w_out_idx)]
    in_specs += [pl.BlockSpec((1, d, tf), lambda i, k: (layer, 0, k)),
                 pl.BlockSpec((1, tf, d), lambda i, k: (layer, k, 0))]
    return pl.pallas_call(
        functools.partial(_mix_ffn_kernel, n_act=n_act, nk=ff // tf),
        grid=(t // tm, ff // tf),
        in_specs=in_specs,
        out_specs=pl.BlockSpec((tm, d), lambda i, k: (i, 0)),
        out_shape=jax.ShapeDtypeStruct((t, d), F32),
        scratch_shapes=[pltpu.VMEM((tm, d), BF16), pltpu.VMEM((tm, d), F32)],
        compiler_params=_cparams(("parallel", "arbitrary")),
        name="mix_ffn",
    )(x2d, g3rows, *acts, *([w_out_all] * n_act), w1_all, w2_all)


def _ind(mask):
    return jnp.where(mask, 1.0, 0.0).astype(F32)


def _div_pow2(x, d):
    return jnp.right_shift(x, int(math.log2(d)))


def _chunk_structs(rows, ns):
    lt = rows // ns
    ri = lax.broadcasted_iota(jnp.int32, (rows, rows), 0)
    ci = lax.broadcasted_iota(jnp.int32, (rows, rows), 1)
    cs = SimpleNamespace(rows=rows, ns=ns, lt=lt)
    if ns == 1:
        cs.same = None
        cs.causal = ri >= ci
        cs.strict = ri > ci
        cs.upper = ri <= ci
    else:
        rseq = _div_pow2(ri, lt)
        cseq = _div_pow2(ci, lt)
        cs.same = rseq == cseq
        cs.causal = cs.same & (ri >= ci)
        cs.strict = cs.same & (ri > ci)
        cs.upper = cs.same & (ri <= ci)
        cs.last = _ind(ci == rseq * lt + (lt - 1))
        cs.last_t = _ind(ri == cseq * lt + (lt - 1))
        sr = _div_pow2(lax.broadcasted_iota(jnp.int32, (ns * HEAD_DIM, rows), 0), HEAD_DIM)
        sc = lax.broadcasted_iota(jnp.int32, (ns * HEAD_DIM, rows), 1)
        cs.stack_last = _ind(sc == sr * lt + (lt - 1))
        cs.stack_mask = _ind(_div_pow2(sc, lt) == sr)
        er = _div_pow2(lax.broadcasted_iota(jnp.int32, (rows, ns * HEAD_DIM), 0), lt)
        ec = _div_pow2(lax.broadcasted_iota(jnp.int32, (rows, ns * HEAD_DIM), 1), HEAD_DIM)
        cs.expand_mask = _ind(er == ec)
        kr = lax.broadcasted_iota(jnp.int32, (ns, rows), 0)
        kc = lax.broadcasted_iota(jnp.int32, (ns, rows), 1)
        cs.seq_last = _ind(kc == kr * lt + (lt - 1))
        cs.seq_sum = _ind(_div_pow2(kc, lt) == kr)
        pr = _div_pow2(lax.broadcasted_iota(jnp.int32, (rows, HEAD_DIM), 0), lt)
        pc = lax.broadcasted_iota(jnp.int32, (rows, HEAD_DIM), 1)
        cs.seq_expand = _ind(pr == pc)
    cs.lower_f = _ind(cs.causal)
    cs.upper_f = _ind(cs.upper)
    eye_r = lax.broadcasted_iota(jnp.int32, (HEAD_DIM, HEAD_DIM), 0)
    eye_c = lax.broadcasted_iota(jnp.int32, (HEAD_DIM, HEAD_DIM), 1)
    cs.eye = _ind(eye_r == eye_c)
    cs.eye_rows = _ind(ri == ci)
    return cs


def _last_rows(cs, x):
    if cs.ns == 1:
        return jnp.broadcast_to(x[cs.rows - 1:cs.rows, :], x.shape)
    return _sel_mm(cs.last, x)


def _last_lanes(cs, x):
    if cs.ns == 1:
        return jnp.broadcast_to(x[:, cs.rows - 1:cs.rows], x.shape)
    return _mm_sel(x, cs.last_t)


def _expand_lhs(cs, x):
    if cs.ns == 1:
        return x
    reps = x.shape[-2] // cs.rows
    mask = cs.expand_mask if reps == 1 else jnp.concatenate([cs.expand_mask] * reps, axis=0)
    return jnp.concatenate([x] * cs.ns, axis=-1) * mask


def _stack_t(cs, x):
    xt = _mm_nt(cs.eye, x)
    if cs.ns == 1:
        return xt
    return jnp.concatenate([xt] * cs.ns, axis=0) * cs.stack_mask


def _stack_scalar(cs, col):
    if cs.ns == 1:
        return col[cs.rows - 1:cs.rows, :]
    return _sel_mm(cs.stack_last, jnp.broadcast_to(col, (cs.rows, HEAD_DIM)))[:, 0:1]


_BATCH_NN = (((2,), (1,)), ((0,), (0,)))
_BATCH_NT = (((2,), (2,)), ((0,), (0,)))
_BATCH_TN = (((1,), (1,)), ((0,), (0,)))


def _bmm(a, b, dims=_BATCH_NN):
    return lax.dot_general(a.astype(BF16), b.astype(BF16), dims, preferred_element_type=F32)


def _split_lhs(a):
    hi = a.astype(BF16).astype(F32)
    return jnp.concatenate([hi, a - hi, hi], axis=-1).astype(BF16)


def _split_rhs(b):
    hi = b.astype(BF16).astype(F32)
    return jnp.concatenate([hi, hi, b - hi], axis=-2).astype(BF16)


def _bmm_split(a_cat, b_cat):
    return lax.dot_general(a_cat, b_cat, _BATCH_NN, preferred_element_type=F32)


def _unit_lower_inverse_off(cs, a):
    toff = -a
    p = toff
    p_rhs = _split_rhs(p)
    for _ in range(int(math.log2(cs.lt)) - 1):
        p = _bmm_split(_split_lhs(p), p_rhs)
        p_rhs = _split_rhs(p)
        toff = toff + p + _bmm_split(_split_lhs(toff), p_rhs)
    return toff


def _gdn_block(cs, nchunks, xc, gate, sm, smt, prm_row, prm_col, gng, s_heads, write_out):
    rows = cs.rows
    hq = N_HEADS * HEAD_DIM
    pairs = [(c, h) for c in range(nchunks) for h in range(N_HEADS)]
    rs = lambda c: slice(c * rows, (c + 1) * rows)

    def heads(x, col0):
        return jnp.stack([x[rs(c), col0 + h * HEAD_DIM:col0 + (h + 1) * HEAD_DIM] for c, h in pairs])

    def cols(x, lane0):
        return jnp.stack([x[rs(c), lane0 + h:lane0 + h + 1] for c, h in pairs])

    beta_all = _sigmoid(sm)
    g_c = -prm_row[0:1, :] * _softplus(sm + prm_row[1:2, :])
    g_r = -prm_col[:, 0:1] * _softplus(smt + prm_col[:, 1:2])
    gc_c = jnp.concatenate([_sel_mm(cs.lower_f, g_c[rs(c)]) for c in range(nchunks)], axis=0)
    gc_r = jnp.concatenate([_mm_sel(g_r[:, rs(c)], cs.upper_f) for c in range(nchunks)], axis=1)
    gl_c = jnp.concatenate([_last_rows(cs, gc_c[rs(c)]) for c in range(nchunks)], axis=0)
    egc = jnp.exp(gc_c)
    ekd = jnp.exp(gl_c - gc_c)

    q = _l2norm(heads(xc, 0)) * (HEAD_DIM ** -0.5)
    k = _l2norm(heads(xc, hq))
    v = heads(xc, 2 * hq)
    beta = cols(beta_all, 0)
    gcol = cols(gc_c, N_HEADS)
    eg = cols(egc, N_HEADS)
    grow = jnp.stack([gc_r[N_HEADS + h:N_HEADS + h + 1, rs(c)] for c, h in pairs])
    decay = jnp.where(cs.causal, jnp.exp(jnp.where(cs.causal, gcol - grow, 0.0)), 0.0)
    kb = k * beta
    a = jnp.where(cs.strict, _bmm(kb, k, _BATCH_NT) * decay, 0.0)
    toff = _unit_lower_inverse_off(cs, a)
    rhs = jnp.concatenate([v * beta, kb * eg], axis=-1)
    sol = rhs + _bmm_split(_split_lhs(toff), _split_rhs(rhs))
    u_val, w_k = sol[..., :HEAD_DIM], sol[..., HEAD_DIM:]
    qk = jnp.where(cs.causal, _bmm(q, k, _BATCH_NT) * decay, 0.0)
    q_dec = q * eg
    k_dec = k * cols(ekd, N_HEADS)

    s = s_heads
    for c in range(nchunks):
        hs = slice(c * N_HEADS, (c + 1) * N_HEADS)
        p = _bmm(_expand_lhs(cs, jnp.concatenate([w_k[hs], q_dec[hs]], axis=1)), s)
        v_new = u_val[hs] - p[:, :rows]
        o = p[:, rows:] + _bmm(qk[hs], v_new)
        gl_cols = [gl_c[rs(c), N_HEADS + h:N_HEADS + h + 1] for h in range(N_HEADS)]
        g_last = jnp.exp(jnp.stack([_stack_scalar(cs, col) for col in gl_cols]))
        if cs.ns == 1:
            upd = _bmm(k_dec[hs], v_new, _BATCH_TN)
        else:
            kt = _bmm(jnp.broadcast_to(cs.eye, (N_HEADS, HEAD_DIM, HEAD_DIM)), k_dec[hs], _BATCH_NT)
            upd = _bmm(jnp.concatenate([kt] * cs.ns, axis=1) * cs.stack_mask, v_new)
        s = s * g_last + upd
        for h in range(N_HEADS):
            write_out(c, h, _rms(o[h], gng) * _silu(gate[rs(c), h * HEAD_DIM:(h + 1) * HEAD_DIM]))
    return s


def _project_block(x_ref, g_ref, w_ref, wst_ref, proj_w, smt_w):
    hn = _rms(x_ref[...], g_ref[...]).astype(BF16)
    proj_w[...] = jnp.dot(hn, w_ref[0], preferred_element_type=F32)
    smt_w[...] = lax.dot_general(wst_ref[0], hn, (((1,), (1,)), ((), ())), preferred_element_type=F32)


def _pipelined_steps(step):
    i = pl.program_id(0)
    for parity in range(2):
        @pl.when(lax.rem(i, 2) == parity)
        def _():
            step(parity)


def _even_prompt_kernel(x_ref, g_ref, w_ref, wst_ref, cw_ref, prow_ref, pcol_ref, gng_ref, s0_ref, buf_ref,
                        sw_ref, sb_ref, sg_ref, o_ref, sout_ref, bufout_ref,
                        proj_a, proj_b, smt_a, smt_b, s_scr, xp_scr, *, tb, nt):
    i = pl.program_id(0)
    t = lax.rem(jnp.maximum(i - 1, 0), nt)
    hq = N_HEADS * HEAD_DIM
    cch = 3 * hq
    cs = _chunk_structs(CHUNK_ROWS, 1)

    @pl.when(i == 0)
    def _():
        proj_b[...] = jnp.zeros_like(proj_b)
        smt_b[...] = jnp.zeros_like(smt_b)

    @pl.when(t == 0)
    def _():
        s_scr[...] = s0_ref[0]
        xp_scr[...] = buf_ref[0]

    def write_delta(c, h, o):
        o_ref[c * CHUNK_ROWS:(c + 1) * CHUNK_ROWS, h * HEAD_DIM:(h + 1) * HEAD_DIM] = o

    def write_gating(rs, g, val):
        o_ref[rs, hq + g * HEAD_DIM:hq + (g + 1) * HEAD_DIM] = val

    def step(parity):
        proj_w, smt_w, proj_r, smt_r = ((proj_a, smt_a, proj_b, smt_b) if parity == 0
                                        else (proj_b, smt_b, proj_a, smt_a))
        _project_block(x_ref, g_ref, w_ref, wst_ref, proj_w, smt_w)
        x = proj_r[:, 0:cch]
        xp = jnp.concatenate([xp_scr[...], x], axis=0)
        y = cw_ref[CONV_W - 1:CONV_W, :] * x
        for back in range(1, CONV_W):
            y = y + cw_ref[CONV_W - 1 - back:CONV_W - back, :] * pltpu.roll(xp, back, axis=0)[8:]
        xp_scr[...] = x[tb - 8:]
        s_scr[...] = _gdn_block(cs, tb // CHUNK_ROWS, _silu(y), proj_r[:, cch:cch + hq],
                                proj_r[:, cch + 3 * hq:cch + 3 * hq + SMALL_COLS], smt_r[...],
                                prow_ref[...], pcol_ref[...], gng_ref[...], s_scr[...], write_delta)
        _sgu_apply(proj_r[:, cch + hq:cch + 2 * hq], proj_r[:, cch + 2 * hq:cch + 3 * hq],
                   sw_ref, sb_ref[...], sg_ref[...], SGU_CHUNK, 1, write_gating, None)

    _pipelined_steps(step)

    @pl.when(t == nt - 1)
    def _():
        sout_ref[0] = s_scr[...]
        bufout_ref[0] = xp_scr[...]


def _even_prompt(x2, g, w_main_all, w_small_t_all, e, conv_w, prm_row, prm_col, gng, s0, buf8,
                 sgu_w, sgu_b_cols, sgu_g, bsz, seq, tb):
    t, d = x2.shape
    hq = N_HEADS * HEAD_DIM
    cch = 3 * hq
    n_proj = w_main_all.shape[2]
    n = t // tb
    nt = seq // tb
    blk = lambda i: jnp.minimum(i, n - 1)
    seq_of = lambda i: jnp.maximum(i - 1, 0) // nt
    const2 = lambda i: (0, 0)
    return pl.pallas_call(
        functools.partial(_even_prompt_kernel, tb=tb, nt=nt),
        grid=(n + 1,),
        in_specs=[pl.BlockSpec((tb, d), lambda i: (blk(i), 0)),
                  pl.BlockSpec((1, d), const2),
                  pl.BlockSpec((1, d, n_proj), lambda i: (e, 0, 0)),
                  pl.BlockSpec((1, SMALL_ROWS, d), lambda i: (e, 0, 0)),
                  pl.BlockSpec((CONV_W, cch), const2),
                  pl.BlockSpec((2, SMALL_COLS), const2),
                  pl.BlockSpec((SMALL_ROWS, 2), const2),
                  pl.BlockSpec((1, HEAD_DIM), const2),
                  pl.BlockSpec((1, N_HEADS, HEAD_DIM, HEAD_DIM), lambda i: (seq_of(i), 0, 0, 0)),
                  pl.BlockSpec((1, 8, cch), lambda i: (seq_of(i), 0, 0)),
                  pl.BlockSpec((N_HEADS, SGU_CHUNK, SGU_CHUNK), lambda i: (0, 0, 0)),
                  pl.BlockSpec((SGU_CHUNK, SMALL_COLS), const2),
                  pl.BlockSpec((N_HEADS, HEAD_DIM), const2)],
        out_specs=[pl.BlockSpec((tb, 2 * hq), lambda i: (jnp.maximum(i - 1, 0), 0)),
                   pl.BlockSpec((1, N_HEADS, HEAD_DIM, HEAD_DIM), lambda i: (seq_of(i), 0, 0, 0)),
                   pl.BlockSpec((1, 8, cch), lambda i: (seq_of(i), 0, 0))],
        out_shape=[jax.ShapeDtypeStruct((t, 2 * hq), F32),
                   jax.ShapeDtypeStruct((bsz, N_HEADS, HEAD_DIM, HEAD_DIM), F32),
                   jax.ShapeDtypeStruct((bsz, 8, cch), F32)],
        scratch_shapes=[pltpu.VMEM((tb, n_proj), F32), pltpu.VMEM((tb, n_proj), F32),
                        pltpu.VMEM((SMALL_ROWS, tb), F32), pltpu.VMEM((SMALL_ROWS, tb), F32),
                        pltpu.VMEM((N_HEADS, HEAD_DIM, HEAD_DIM), F32),
                        pltpu.VMEM((8, cch), F32)],
        compiler_params=_cparams(("arbitrary",)),
        name="even_prompt",
    )(x2, g.reshape(1, d), w_main_all, w_small_t_all, conv_w, prm_row, prm_col,
      gng.reshape(1, HEAD_DIM), s0, buf8, sgu_w, sgu_b_cols, sgu_g)


def _conv_sample_kernel(xcat_ref, cw_ref, o_ref, *, lt, cch):
    for t in range(lt):
        y = None
        for j in range(CONV_W):
            term = cw_ref[j:j + 1, :] * xcat_ref[:, (t + j) * cch:(t + j + 1) * cch]
            y = term if y is None else y + term
        o_ref[:, t * cch:(t + 1) * cch] = _silu(y)


def _conv_sample(xcat, conv_w, lt, cch):
    nseq = xcat.shape[0]
    return pl.pallas_call(
        functools.partial(_conv_sample_kernel, lt=lt, cch=cch),
        grid=(1,),
        in_specs=[pl.BlockSpec(xcat.shape, lambda i: (0, 0)),
                  pl.BlockSpec((CONV_W, cch), lambda i: (0, 0))],
        out_specs=pl.BlockSpec((nseq, lt * cch), lambda i: (0, 0)),
        out_shape=jax.ShapeDtypeStruct((nseq, lt * cch), F32),
        compiler_params=_cparams(("arbitrary",)),
        name="conv_sample",
    )(xcat, conv_w)


def _gdn_sample_kernel(xc_ref, gate_ref, sm_ref, smt_ref, prow_ref, pcol_ref, gng_ref, s0_ref,
                       *rest, ns):
    o_ref, sout_ref = rest[-2:]
    cs = _chunk_structs(CHUNK_ROWS, ns)
    s_heads = jnp.stack([s0_ref[0, :, h].reshape(ns * HEAD_DIM, HEAD_DIM) for h in range(N_HEADS)])

    def write_out(c, h, o):
        o_ref[:, h * HEAD_DIM:(h + 1) * HEAD_DIM] = o

    s_new = _gdn_block(cs, 1, xc_ref[...], gate_ref[...], sm_ref[...], smt_ref[0],
                       prow_ref[...], pcol_ref[...], gng_ref[...], s_heads, write_out)
    for h in range(N_HEADS):
        sout_ref[0, :, h] = s_new[h].reshape(ns, HEAD_DIM, HEAD_DIM)


def _gdn_sample(xc, proj, smt, prm_row, prm_col, gng, s_all, layer, s_out_prev, lt):
    hq = N_HEADS * HEAD_DIM
    cch = 3 * hq
    ns = CHUNK_ROWS // lt
    nchunk = xc.shape[0] // CHUNK_ROWS
    small_blk = (cch + 3 * hq) // SMALL_COLS
    state_spec = pl.BlockSpec((1, ns, N_HEADS, HEAD_DIM, HEAD_DIM), lambda c: (layer, c, 0, 0, 0))
    in_specs = [pl.BlockSpec((CHUNK_ROWS, cch), lambda c: (c, 0)),
                pl.BlockSpec((CHUNK_ROWS, hq), lambda c: (c, cch // hq)),
                pl.BlockSpec((CHUNK_ROWS, SMALL_COLS), lambda c: (c, small_blk)),
                pl.BlockSpec((1, SMALL_ROWS, CHUNK_ROWS), lambda c: (c, 0, 0)),
                pl.BlockSpec((2, SMALL_COLS), lambda c: (0, 0)),
                pl.BlockSpec((SMALL_ROWS, 2), lambda c: (0, 0)),
                pl.BlockSpec((1, HEAD_DIM), lambda c: (0, 0)),
                state_spec]
    args = [xc, proj, proj, smt, prm_row, prm_col, gng.reshape(1, HEAD_DIM), s_all]
    aliases = {}
    if s_out_prev is not None:
        aliases = {len(args): 1}
        in_specs.append(pl.BlockSpec(memory_space=pl.ANY))
        args.append(s_out_prev)
    return pl.pallas_call(
        functools.partial(_gdn_sample_kernel, ns=ns),
        grid=(nchunk,),
        in_specs=in_specs,
        out_specs=[pl.BlockSpec((CHUNK_ROWS, hq), lambda c: (c, 0)), state_spec],
        out_shape=[jax.ShapeDtypeStruct((xc.shape[0], hq), F32),
                   jax.ShapeDtypeStruct(s_all.shape, F32)],
        input_output_aliases=aliases,
        compiler_params=_cparams(("parallel",)),
        name="gdn_sample",
    )(*args)


def _sgu_apply(u_pre, v_pre, w_ref, b_cols, g_rows, rows, ns, write_o, write_vb):
    lt = rows // ns
    ri = lax.broadcasted_iota(jnp.int32, (rows, rows), 0)
    ci = lax.broadcasted_iota(jnp.int32, (rows, rows), 1)
    keep = ri >= ci
    if ns > 1:
        keep = keep & (_div_pow2(ri, lt) == _div_pow2(ci, lt))
    for g in range(N_HEADS):
        lo = g * HEAD_DIM
        w = jnp.where(keep, w_ref[g], 0.0).astype(BF16)
        u = _gelu(u_pre[:, lo:lo + HEAD_DIM])
        vb = _rms(_gelu(v_pre[:, lo:lo + HEAD_DIM]), g_rows[g:g + 1, :])
        if write_vb is not None:
            write_vb(g, vb)
        for c in range(u_pre.shape[0] // rows):
            rs = slice(c * rows, (c + 1) * rows)
            write_o(rs, g, u[rs] * (_mm(w, vb[rs]) + b_cols[:, g:g + 1]))


def _sgu_kernel(u_ref, v_ref, w_ref, b_ref, g_ref, o_ref, vb_ref, *, rows, ns):
    def write_o(rs, g, val):
        o_ref[rs, g * HEAD_DIM:(g + 1) * HEAD_DIM] = val

    def write_vb(g, val):
        vb_ref[:, g * HEAD_DIM:(g + 1) * HEAD_DIM] = val

    _sgu_apply(u_ref[...], v_ref[...], w_ref, b_ref[...], g_ref[...], rows, ns, write_o, write_vb)


def _sgu(proj, w_tiles, b_cols, norm_g, rows, ns, nchunks):
    hq = N_HEADS * HEAD_DIM
    t = proj.shape[0]
    u_blk = (3 * hq + hq) // hq
    blk = rows * nchunks
    return pl.pallas_call(
        functools.partial(_sgu_kernel, rows=rows, ns=ns),
        grid=(t // blk,),
        in_specs=[pl.BlockSpec((blk, hq), lambda i: (i, u_blk)),
                  pl.BlockSpec((blk, hq), lambda i: (i, u_blk + 1)),
                  pl.BlockSpec((N_HEADS, rows, rows), lambda i: (0, 0, 0)),
                  pl.BlockSpec((rows, SMALL_COLS), lambda i: (0, 0)),
                  pl.BlockSpec((N_HEADS, HEAD_DIM), lambda i: (0, 0))],
        out_specs=[pl.BlockSpec((blk, hq), lambda i: (i, 0)),
                   pl.BlockSpec((blk, hq), lambda i: (i, 0))],
        out_shape=[jax.ShapeDtypeStruct((t, hq), F32), jax.ShapeDtypeStruct((t, hq), F32)],
        compiler_params=_cparams(("parallel",)),
        name="sgu",
    )(proj, proj, w_tiles, b_cols, norm_g)


def _logsigmoid(x):
    return jnp.minimum(x, 0.0) - jnp.log1p(jnp.exp(-jnp.abs(x)))


def _mlstm_chunk(cs, q_all, k_all, v_all, opre, sm, smt, gb_row, gb_col, mng, c_list, n_rows, m_rows):
    neg_inf = -jnp.inf
    pre_c = sm + gb_row
    pre_r = smt + gb_col
    b_c = _sel_mm(cs.lower_f, _logsigmoid(pre_c))
    b_r = _mm_sel(_logsigmoid(pre_r), cs.upper_f)
    bl_c = _last_rows(cs, b_c)
    bl_r = _last_lanes(cs, b_r)
    res = []
    for h in range(N_HEADS):
        lo = h * HEAD_DIM
        vo = h * MLSTM_DV
        q = q_all[:, lo:lo + HEAD_DIM] * (HEAD_DIM ** -0.5)
        k = k_all[:, lo:lo + HEAD_DIM]
        v = v_all[:, vo:vo + MLSTM_DV]
        bcol = b_c[:, N_HEADS + h:N_HEADS + h + 1]
        brow = b_r[N_HEADS + h:N_HEADS + h + 1, :]
        icol = pre_c[:, h:h + 1]
        irow = pre_r[h:h + 1, :]
        blcol = bl_c[:, N_HEADS + h:N_HEADS + h + 1]
        blrow = bl_r[N_HEADS + h:N_HEADS + h + 1, :]
        mrow = m_rows[:, h:h + 1]
        inter = bcol + mrow
        dmat = jnp.where(cs.causal, bcol - brow + irow, neg_inf)
        m_t = jnp.maximum(inter, jnp.max(dmat, axis=-1, keepdims=True))
        w_intra = jnp.exp(dmat - m_t)
        w_inter = jnp.exp(inter - m_t)
        s = _mm_nt(q, k) * w_intra
        c_old = c_list[h]
        num = w_inter * _mm(_expand_lhs(cs, q), c_old) + _mm(s, v)
        qn = jnp.sum(q * n_rows[:, lo:lo + HEAD_DIM], axis=-1, keepdims=True)
        den = w_inter * qn + jnp.sum(s, axis=-1, keepdims=True)
        hval = num / jnp.maximum(jnp.abs(den), jnp.exp(-m_t))
        out = _rms(hval, mng) * _sigmoid(opre[:, vo:vo + MLSTM_DV])
        logw_row = blrow - brow + irow
        if cs.ns == 1:
            seq_max = jnp.max(logw_row, axis=-1, keepdims=True)
        else:
            seq_max = jnp.max(jnp.where(cs.same, logw_row, neg_inf), axis=-1, keepdims=True)
        m_new = jnp.maximum(blcol + mrow, seq_max)
        keep = jnp.exp(blcol + mrow - m_new)
        wk = jnp.exp(blcol - bcol + icol - m_new)
        kw = k * wk
        c_new = c_old * _stack_scalar(cs, keep) + _mm(_stack_t(cs, kw), v)
        res.append((out, c_new, keep, m_new, kw))
    return res


def _mlstm_prompt_block(cs, nchunks, q_all, k_all, v_all, opre, sm, smt, gb_row, gb_col, mng,
                        c_heads, n_heads, m_heads, write_out):
    rows = cs.rows
    pairs = [(c, h) for c in range(nchunks) for h in range(N_HEADS)]
    rs = lambda c: slice(c * rows, (c + 1) * rows)

    def heads(x, width):
        return jnp.stack([x[rs(c), h * width:(h + 1) * width] for c, h in pairs])

    def cols(x, lane0):
        return jnp.stack([x[rs(c), lane0 + h:lane0 + h + 1] for c, h in pairs])

    def lanes(x, row0):
        return jnp.stack([x[row0 + h:row0 + h + 1, rs(c)] for c, h in pairs])

    pre_c = sm + gb_row
    pre_r = smt + gb_col
    lf_c = _logsigmoid(pre_c)
    lf_r = _logsigmoid(pre_r)
    b_c = jnp.concatenate([_sel_mm(cs.lower_f, lf_c[rs(c)]) for c in range(nchunks)], axis=0)
    b_r = jnp.concatenate([_mm_sel(lf_r[:, rs(c)], cs.upper_f) for c in range(nchunks)], axis=1)

    q = heads(q_all, HEAD_DIM) * (HEAD_DIM ** -0.5)
    k = heads(k_all, HEAD_DIM)
    v = heads(v_all, MLSTM_DV)
    bcol = cols(b_c, N_HEADS)
    icol = cols(pre_c, 0)
    brow = lanes(b_r, N_HEADS)
    irow = lanes(pre_r, 0)
    blast = bcol[:, rows - 1:rows, :]
    dmat = jnp.where(cs.causal, bcol - brow + irow, -jnp.inf)
    rmax = jnp.max(dmat, axis=-1, keepdims=True)
    sp = _bmm(q, k, _BATCH_NT) * jnp.exp(dmat - rmax)
    sv = _bmm(sp, v)
    ssum = jnp.sum(sp, axis=-1, keepdims=True)
    lmax = jnp.max(blast - brow + irow, axis=-1, keepdims=True)
    kwp = k * jnp.exp(blast - bcol + icol - lmax)
    upd = _bmm(kwp, v, _BATCH_TN)
    nsum = jnp.sum(kwp, axis=1, keepdims=True)

    cst, n, m = c_heads, n_heads, m_heads
    for c in range(nchunks):
        hs = slice(c * N_HEADS, (c + 1) * N_HEADS)
        inter = bcol[hs] + m
        m_t = jnp.maximum(inter, rmax[hs])
        w_inter = jnp.exp(inter - m_t)
        w_intra = jnp.exp(rmax[hs] - m_t)
        num = w_inter * _bmm(q[hs], cst) + w_intra * sv[hs]
        den = (w_inter * jnp.sum(q[hs] * n, axis=-1, keepdims=True) + w_intra * ssum[hs])
        hval = num / jnp.maximum(jnp.abs(den), jnp.exp(-m_t))
        for h in range(N_HEADS):
            write_out(c, h, _rms(hval[h], mng) * _sigmoid(opre[rs(c), h * MLSTM_DV:(h + 1) * MLSTM_DV]))
        m_new = jnp.maximum(blast[hs] + m, lmax[hs])
        keep = jnp.exp(blast[hs] + m - m_new)
        fresh = jnp.exp(lmax[hs] - m_new)
        cst = keep * cst + fresh * upd[hs]
        n = keep * n + fresh * nsum[hs]
        m = m_new
    return cst, n, m


def _odd_prompt_kernel(x_ref, g_ref, w_ref, wst_ref, gbr_ref, gbc_ref, mng_ref, c0_ref, n0_ref, m0_ref,
                       o_ref, cout_ref, nout_ref, mout_ref,
                       proj_a, proj_b, smt_a, smt_b, c_scr, n_scr, m_scr, *, tb, nt):
    i = pl.program_id(0)
    t = lax.rem(jnp.maximum(i - 1, 0), nt)
    hq = N_HEADS * HEAD_DIM
    hv = N_HEADS * MLSTM_DV
    cs = _chunk_structs(CHUNK_ROWS, 1)
    lane = lax.broadcasted_iota(jnp.int32, (1, SMALL_COLS), 1)

    @pl.when(i == 0)
    def _():
        proj_b[...] = jnp.zeros_like(proj_b)
        smt_b[...] = jnp.zeros_like(smt_b)

    @pl.when(t == 0)
    def _():
        c_scr[...] = c0_ref[0]
        n_scr[...] = n0_ref[0]
        m_scr[...] = m0_ref[0]

    def write_out(c, h, o):
        o_ref[c * CHUNK_ROWS:(c + 1) * CHUNK_ROWS, h * MLSTM_DV:(h + 1) * MLSTM_DV] = o

    def step(parity):
        proj_w, smt_w, proj_r, smt_r = ((proj_a, smt_a, proj_b, smt_b) if parity == 0
                                        else (proj_b, smt_b, proj_a, smt_a))
        _project_block(x_ref, g_ref, w_ref, wst_ref, proj_w, smt_w)
        n_cur = n_scr[...]
        m_cur = m_scr[...]
        n_heads = jnp.stack([n_cur[:, h * HEAD_DIM:(h + 1) * HEAD_DIM] for h in range(N_HEADS)])
        m_heads = jnp.stack([m_cur[:, h:h + 1] for h in range(N_HEADS)])
        c_new, n_heads, m_heads = _mlstm_prompt_block(
            cs, tb // CHUNK_ROWS, proj_r[:, 0:hq], proj_r[:, hq:2 * hq], proj_r[:, 2 * hq:2 * hq + hv],
            proj_r[:, 2 * hq + hv:2 * hq + 2 * hv],
            proj_r[:, 2 * hq + 2 * hv:2 * hq + 2 * hv + SMALL_COLS], smt_r[...],
            gbr_ref[...], gbc_ref[...], mng_ref[...], c_scr[...], n_heads, m_heads, write_out)
        c_scr[...] = c_new
        n_scr[...] = jnp.concatenate([n_heads[h] for h in range(N_HEADS)], axis=1)
        for h in range(N_HEADS):
            m_cur = jnp.where(lane == h, m_heads[h], m_cur)
        m_scr[...] = m_cur

    _pipelined_steps(step)

    @pl.when(t == nt - 1)
    def _():
        cout_ref[0] = c_scr[...]
        nout_ref[0] = n_scr[...]
        mout_ref[0] = m_scr[...]


def _odd_prompt(x2, g, w_main_all, w_small_t_all, o, gb_row, gb_col, mng, c0, n0, m0, bsz, seq, tb):
    t, d = x2.shape
    hq = N_HEADS * HEAD_DIM
    hv = N_HEADS * MLSTM_DV
    n_proj = w_main_all.shape[2]
    n = t // tb
    nt = seq // tb
    blk = lambda i: jnp.minimum(i, n - 1)
    seq_of = lambda i: jnp.maximum(i - 1, 0) // nt
    const2 = lambda i: (0, 0)
    state_c = pl.BlockSpec((1, N_HEADS, HEAD_DIM, MLSTM_DV), lambda i: (seq_of(i), 0, 0, 0))
    state_n = pl.BlockSpec((1, 1, hq), lambda i: (seq_of(i), 0, 0))
    state_m = pl.BlockSpec((1, 1, SMALL_COLS), lambda i: (seq_of(i), 0, 0))
    return pl.pallas_call(
        functools.partial(_odd_prompt_kernel, tb=tb, nt=nt),
        grid=(n + 1,),
        in_specs=[pl.BlockSpec((tb, d), lambda i: (blk(i), 0)),
                  pl.BlockSpec((1, d), const2),
                  pl.BlockSpec((1, d, n_proj), lambda i: (o, 0, 0)),
                  pl.BlockSpec((1, SMALL_ROWS, d), lambda i: (o, 0, 0)),
                  pl.BlockSpec((1, SMALL_COLS), const2),
                  pl.BlockSpec((SMALL_ROWS, 1), const2),
                  pl.BlockSpec((1, MLSTM_DV), const2),
                  state_c, state_n, state_m],
        out_specs=[pl.BlockSpec((tb, hv), lambda i: (jnp.maximum(i - 1, 0), 0)),
                   state_c, state_n, state_m],
        out_shape=[jax.ShapeDtypeStruct((t, hv), F32),
                   jax.ShapeDtypeStruct(c0.shape, F32),
                   jax.ShapeDtypeStruct(n0.shape, F32),
                   jax.ShapeDtypeStruct(m0.shape, F32)],
        scratch_shapes=[pltpu.VMEM((tb, n_proj), F32), pltpu.VMEM((tb, n_proj), F32),
                        pltpu.VMEM((SMALL_ROWS, tb), F32), pltpu.VMEM((SMALL_ROWS, tb), F32),
                        pltpu.VMEM((N_HEADS, HEAD_DIM, MLSTM_DV), F32),
                        pltpu.VMEM((1, hq), F32),
                        pltpu.VMEM((1, SMALL_COLS), F32)],
        compiler_params=_cparams(("arbitrary",)),
        name="odd_prompt",
    )(x2, g.reshape(1, d), w_main_all, w_small_t_all, gb_row, gb_col, mng.reshape(1, MLSTM_DV), c0, n0, m0)


def _mlstm_sample_kernel(q_ref, k_ref, v_ref, op_ref, sm_ref, smt_ref, gbr_ref, gbc_ref, mng_ref,
                         c0_ref, n0_ref, m0_ref, *rest, ns):
    o_ref, cout_ref, nout_ref, mout_ref = rest[-4:]
    cs = _chunk_structs(CHUNK_ROWS, ns)
    lane = lax.broadcasted_iota(jnp.int32, (1, SMALL_COLS), 1)
    c_list = [c0_ref[0, :, h].reshape(ns * HEAD_DIM, MLSTM_DV) for h in range(N_HEADS)]
    n0 = n0_ref[...]
    m0 = m0_ref[...]
    n_rows = _sel_mm(cs.seq_expand, jnp.concatenate(
        [n0, jnp.zeros((HEAD_DIM - ns, n0.shape[1]), F32)], axis=0))
    m_rows = _sel_mm(cs.seq_expand, jnp.concatenate(
        [m0, jnp.zeros((HEAD_DIM - ns, m0.shape[1]), F32)], axis=0))
    res = _mlstm_chunk(cs, q_ref[...], k_ref[...], v_ref[...], op_ref[...], sm_ref[...], smt_ref[0],
                       gbr_ref[...], gbc_ref[...], mng_ref[...], c_list, n_rows, m_rows)
    m_all = jnp.zeros((CHUNK_ROWS, SMALL_COLS), F32)
    for h, (out, c_new, keep, m_new, kw) in enumerate(res):
        o_ref[:, h * MLSTM_DV:(h + 1) * MLSTM_DV] = out
        cout_ref[0, :, h] = c_new.reshape(ns, HEAD_DIM, MLSTM_DV)
        keep_seq = _sel_mm(cs.seq_last, jnp.broadcast_to(keep, (CHUNK_ROWS, HEAD_DIM)))
        nout_ref[:, h * HEAD_DIM:(h + 1) * HEAD_DIM] = (
            keep_seq * n0[:, h * HEAD_DIM:(h + 1) * HEAD_DIM] + _sel_mm(cs.seq_sum, kw))
        m_all = jnp.where(lane == h, m_new, m_all)
    mout_ref[...] = _sel_mm(cs.seq_last, m_all)


def _mlstm_sample(proj, smt, gb_row, gb_col, mng, c_all, layer, c_out_prev, n0, m0, lt):
    hq = N_HEADS * HEAD_DIM
    hv = N_HEADS * MLSTM_DV
    ns = CHUNK_ROWS // lt
    t = proj.shape[0]
    small_blk = (2 * hq + 2 * hv) // SMALL_COLS
    state_c = pl.BlockSpec((1, ns, N_HEADS, HEAD_DIM, MLSTM_DV), lambda c: (layer, c, 0, 0, 0))
    state_n = pl.BlockSpec((ns, hq), lambda c: (c, 0))
    state_m = pl.BlockSpec((ns, SMALL_COLS), lambda c: (c, 0))
    in_specs = [pl.BlockSpec((CHUNK_ROWS, hq), lambda c: (c, 0)),
                pl.BlockSpec((CHUNK_ROWS, hq), lambda c: (c, 1)),
                pl.BlockSpec((CHUNK_ROWS, hv), lambda c: (c, 2 * hq // hv)),
                pl.BlockSpec((CHUNK_ROWS, hv), lambda c: (c, 2 * hq // hv + 1)),
                pl.BlockSpec((CHUNK_ROWS, SMALL_COLS), lambda c: (c, small_blk)),
                pl.BlockSpec((1, SMALL_ROWS, CHUNK_ROWS), lambda c: (c, 0, 0)),
                pl.BlockSpec((1, SMALL_COLS), lambda c: (0, 0)),
                pl.BlockSpec((SMALL_ROWS, 1), lambda c: (0, 0)),
                pl.BlockSpec((1, MLSTM_DV), lambda c: (0, 0)),
                state_c, state_n, state_m]
    args = [proj, proj, proj, proj, proj, smt, gb_row, gb_col, mng.reshape(1, MLSTM_DV), c_all, n0, m0]
    aliases = {}
    if c_out_prev is not None:
        aliases = {len(args): 1}
        in_specs.append(pl.BlockSpec(memory_space=pl.ANY))
        args.append(c_out_prev)
    return pl.pallas_call(
        functools.partial(_mlstm_sample_kernel, ns=ns),
        grid=(t // CHUNK_ROWS,),
        in_specs=in_specs,
        out_specs=[pl.BlockSpec((CHUNK_ROWS, hv), lambda c: (c, 0)), state_c, state_n, state_m],
        out_shape=[jax.ShapeDtypeStruct((t, hv), F32),
                   jax.ShapeDtypeStruct(c_all.shape, F32),
                   jax.ShapeDtypeStruct(n0.shape, F32),
                   jax.ShapeDtypeStruct(m0.shape, F32)],
        input_output_aliases=aliases,
        compiler_params=_cparams(("parallel",)),
        name="mlstm_sample",
    )(*args)


def _pad_cols(w, n):
    return jnp.pad(w, ((0, 0), (0, n - w.shape[1])))


def _split_in_weight(w, n_a, n_small):
    small = w[:, :, n_a:n_a + n_small]
    pad = jnp.zeros(w.shape[:2] + (SMALL_COLS - n_small,), w.dtype)
    main = jnp.concatenate([w[:, :, :n_a], w[:, :, n_a + n_small:], small, pad], axis=2)
    small_t = jnp.pad(jnp.swapaxes(small, 1, 2), ((0, 0), (0, SMALL_ROWS - n_small), (0, 0)))
    return main.astype(BF16), small_t.astype(BF16)


def _chunk_lanes(smt):
    rows, t = smt.shape
    return smt.reshape(rows, t // CHUNK_ROWS, CHUNK_ROWS).transpose(1, 0, 2)


def _row_col_params(vals):
    row = jnp.stack([jnp.pad(v, (0, SMALL_COLS - v.shape[0])) for v in vals]).astype(F32)
    col = jnp.stack([jnp.pad(v, (0, SMALL_ROWS - v.shape[0])) for v in vals], axis=1).astype(F32)
    return row, col


def _trunk(x, is_prompt, s_gdn, s_conv, s_c, s_n, s_m, norm_g, w_in_even, conv_w, a_log, dt_bias,
           gdn_norm_g, sgu_norm_g, sgu_w, sgu_b, w_out_even, w_in_odd, gate_b_odd, mlstm_norm_g,
           w_out_odd, w_ff1, w_ff2):
    bsz, seq, d = x.shape
    t = bsz * seq
    depth = norm_g.shape[0]
    hq = N_HEADS * HEAD_DIM
    cch = 3 * hq
    tm = 1024 if t % 1024 == 0 else 512
    tb = min(512, seq)
    zeros_h = jnp.zeros((N_HEADS,), F32)
    x2 = x.reshape(t, d)
    new_gdn, new_conv, new_v, new_c, new_n, new_m = [], [], [], [], [], []
    gdn_out = c_out = None
    for l in range(depth):
        gl = norm_g[l]
        if l % 2 == 0:
            e = l // 2
            prm_row, prm_col = _row_col_params([jnp.concatenate([zeros_h, jnp.exp(a_log[e])]),
                                                jnp.concatenate([zeros_h, dt_bias[e]])])
            if is_prompt:
                buf8 = jnp.pad(s_conv[e], ((0, 0), (8 - (CONV_W - 1), 0), (0, 0)))
                o_mix, s_new, tail = _even_prompt(
                    x2, gl[0], *w_in_even, e, conv_w[e], prm_row, prm_col, gdn_norm_g[e], s_gdn[e], buf8,
                    sgu_w[e], _pad_cols(sgu_b[e].T, SMALL_COLS), sgu_norm_g[e], bsz, seq, tb)
                buf_new = tail[:, 8 - (CONV_W - 1):, :]
                new_gdn.append(s_new)
                mix_acts, mix_idx = [o_mix], [(e, 0)]
            else:
                proj, smt = _proj_in(x2, gl[0], *w_in_even, e, 512)
                qkv = proj[:, :cch].reshape(bsz, seq, cch)
                xcat = jnp.concatenate([s_conv[e], qkv], axis=1)
                buf_new = xcat[:, seq:, :]
                xc = _conv_sample(xcat.reshape(bsz, (CONV_W - 1 + seq) * cch), conv_w[e], seq, cch)
                o_a, gdn_out = _gdn_sample(xc.reshape(t, cch), proj, _chunk_lanes(smt), prm_row, prm_col,
                                           gdn_norm_g[e], s_gdn, e, gdn_out, seq)
                ns = CHUNK_ROWS // seq
                w_tiles = jnp.tile(sgu_w[e][:, :seq, :seq], (1, ns, ns))
                b_cols = _pad_cols(jnp.tile(sgu_b[e][:, :seq].T, (ns, 1)), SMALL_COLS)
                o_b, vb = _sgu(proj, w_tiles, b_cols, sgu_norm_g[e], CHUNK_ROWS, ns, 1)
                new_v.append(vb.reshape(bsz, seq, hq))
                mix_acts, mix_idx = [o_a, o_b], [(e, 0), (e, 1)]
            new_conv.append(buf_new)
            w_out_all = w_out_even
        else:
            o = l // 2
            gb_row, gb_col = _row_col_params([gate_b_odd[o]])
            if is_prompt:
                n0 = s_n[o].reshape(bsz, 1, hq)
                m0 = _pad_cols(s_m[o], SMALL_COLS).reshape(bsz, 1, SMALL_COLS)
                hh, c_new, n_new, m_new = _odd_prompt(x2, gl[0], *w_in_odd, o, gb_row, gb_col,
                                                      mlstm_norm_g[o], s_c[o], n0, m0, bsz, seq, tb)
                m_new = m_new.reshape(bsz, SMALL_COLS)
                new_c.append(c_new)
            else:
                proj, smt = _proj_in(x2, gl[0], *w_in_odd, o, 512)
                n0 = s_n[o].reshape(bsz, hq)
                m0 = _pad_cols(s_m[o], SMALL_COLS)
                hh, c_out, n_new, m_new = _mlstm_sample(proj, _chunk_lanes(smt), gb_row, gb_col,
                                                        mlstm_norm_g[o], s_c, o, c_out, n0, m0, seq)
            new_n.append(n_new.reshape(bsz, N_HEADS, HEAD_DIM))
            new_m.append(m_new[:, :N_HEADS])
            mix_acts, mix_idx, w_out_all = [hh], [(o, 0)], w_out_odd
        x2 = _mix_ffn(x2, gl[1:4], mix_acts, w_out_all, mix_idx, w_ff1, w_ff2, l, tm, 1024)
    return (x2.reshape(bsz, seq, d),
            jnp.stack(new_gdn) if is_prompt else gdn_out, jnp.stack(new_conv),
            None if is_prompt else jnp.stack(new_v),
            jnp.stack(new_c) if is_prompt else c_out, jnp.stack(new_n), jnp.stack(new_m))


def kernel(x_prompt, x_sample, state_gdn, state_gdn_conv, state_mlstm_c, state_mlstm_n, state_mlstm_m,
           norm_g, w_in_even, conv_w, a_log, dt_bias, gdn_norm_g, sgu_norm_g, sgu_w, sgu_b, w_out_even,
           w_in_odd, gate_b_odd, mlstm_norm_g, w_out_odd, w_ff1, w_ff2):
    hq = N_HEADS * HEAD_DIM
    weights = (norm_g, _split_in_weight(w_in_even, 4 * hq, 2 * N_HEADS), conv_w, a_log, dt_bias,
               gdn_norm_g, sgu_norm_g, sgu_w, sgu_b, w_out_even.astype(BF16),
               _split_in_weight(w_in_odd, 2 * hq + N_HEADS * MLSTM_DV, 2 * N_HEADS), gate_b_odd,
               mlstm_norm_g, w_out_odd.astype(BF16), w_ff1.astype(BF16), w_ff2.astype(BF16))
    bp = x_prompt.shape[0]
    n_even, n_odd = state_gdn.shape[0], state_mlstm_c.shape[0]
    y_prompt, p_gdn, p_conv, _, p_c, p_n, p_m = _trunk(
        x_prompt, True,
        jnp.zeros((n_even, bp) + state_gdn.shape[2:], F32),
        jnp.zeros((n_even, bp) + state_gdn_conv.shape[2:], x_prompt.dtype),
        jnp.zeros((n_odd, bp) + state_mlstm_c.shape[2:], F32),
        jnp.zeros((n_odd, bp) + state_mlstm_n.shape[2:], F32),
        jnp.zeros((n_odd, bp) + state_mlstm_m.shape[2:], F32),
        *weights)
    y_sample, s_gdn, s_conv, s_v, s_c, s_n, s_m = _trunk(
        x_sample, False, state_gdn, state_gdn_conv, state_mlstm_c, state_mlstm_n, state_mlstm_m,
        *weights)
    return (y_prompt, y_sample, p_gdn, s_gdn, p_conv, s_conv, s_v, p_c, s_c, p_n, s_n, p_m, s_m)
```

```python
import functools
import math
from types import SimpleNamespace

import jax
import jax.numpy as jnp
from jax import lax
from jax.experimental import pallas as pl
from jax.experimental.pallas import tpu as pltpu

F32 = jnp.float32
BF16 = jnp.bfloat16

EPS = 1e-6
N_HEADS = 4
HEAD_DIM = 128
MLSTM_DV = 256
CONV_W = 4
CHUNK_ROWS = 64
SGU_CHUNK = 128
SMALL_COLS = 128
SMALL_ROWS = 16
VMEM_LIMIT = 56 * 1024 * 1024


def _cparams(sem):
    return pltpu.CompilerParams(dimension_semantics=sem, vmem_limit_bytes=VMEM_LIMIT)


def _mm(a, b):
    return jnp.dot(a.astype(BF16), b.astype(BF16), preferred_element_type=F32)


def _mm_nt(a, b):
    return lax.dot_general(a.astype(BF16), b.astype(BF16), (((1,), (1,)), ((), ())),
                           preferred_element_type=F32)


def _trunc_bf16(x):
    bits = lax.bitcast_convert_type(x, jnp.uint32) & jnp.uint32(0xFFFF0000)
    return lax.bitcast_convert_type(bits, F32)


def _split3(x):
    hi = _trunc_bf16(x)
    rest = x - hi
    mid = _trunc_bf16(rest)
    return hi, mid, rest - mid


def _sel_mm(sel, x):
    pieces = jnp.concatenate(_split3(x), axis=0).astype(BF16)
    return jnp.dot(jnp.concatenate([sel.astype(BF16)] * 3, axis=1), pieces, preferred_element_type=F32)


def _mm_sel(x, sel):
    pieces = jnp.concatenate(_split3(x), axis=1).astype(BF16)
    return jnp.dot(pieces, jnp.concatenate([sel.astype(BF16)] * 3, axis=0), preferred_element_type=F32)


def _rms(x, g):
    return x * lax.rsqrt(jnp.mean(x * x, axis=-1, keepdims=True) + EPS) * g


def _l2norm(x):
    return x * lax.rsqrt(jnp.sum(x * x, axis=-1, keepdims=True) + EPS)


def _softplus(x):
    return jnp.maximum(x, 0.0) + jnp.log1p(jnp.exp(-jnp.abs(x)))


def _sigmoid(x):
    return 0.5 * jnp.tanh(0.5 * x) + 0.5


def _silu(x):
    return x * _sigmoid(x)


def _gelu(x):
    return 0.5 * x * (1.0 + lax.erf(x * (2.0 ** -0.5)))


def _proj_in_kernel(x_ref, g_ref, w_ref, wst_ref, o_ref, ot_ref):
    _project_block(x_ref, g_ref, w_ref, wst_ref, o_ref, ot_ref)


def _proj_in(x2d, g, w_main_all, w_small_t_all, layer, tm):
    t, d = x2d.shape
    n = w_main_all.shape[2]
    return pl.pallas_call(
        _proj_in_kernel,
        grid=(t // tm,),
        in_specs=[pl.BlockSpec((tm, d), lambda i: (i, 0)),
                  pl.BlockSpec((1, d), lambda i: (0, 0)),
                  pl.BlockSpec((1, d, n), lambda i: (layer, 0, 0)),
                  pl.BlockSpec((1, SMALL_ROWS, d), lambda i: (layer, 0, 0))],
        out_specs=[pl.BlockSpec((tm, n), lambda i: (i, 0)),
                   pl.BlockSpec((SMALL_ROWS, tm), lambda i: (0, i))],
        out_shape=[jax.ShapeDtypeStruct((t, n), F32),
                   jax.ShapeDtypeStruct((SMALL_ROWS, t), F32)],
        compiler_params=_cparams(("parallel",)),
        name="proj_in",
    )(x2d, g.reshape(1, d), w_main_all, w_small_t_all)


def _mix_ffn_kernel(*refs, n_act, nk):
    x_ref, g_ref = refs[0], refs[1]
    acts = refs[2:2 + n_act]
    wos = refs[2 + n_act:2 + 2 * n_act]
    w1_ref, w2_ref, o_ref, h_ref, acc_ref = refs[2 + 2 * n_act:]
    k = pl.program_id(1)
    tm = x_ref.shape[0]
    halves = [slice(0, tm // 2), slice(tm // 2, tm)] if tm >= 512 else [slice(0, tm)]

    def prologue(rs):
        mix = None
        for a_ref, w_ref in zip(acts, wos):
            p = jnp.dot(a_ref[rs, :].astype(BF16), w_ref[0], preferred_element_type=F32)
            mix = p if mix is None else mix + p
        x1 = x_ref[rs, :] + _rms(mix, g_ref[0:1, :])
        o_ref[rs, :] = x1
        h_ref[rs, :] = _rms(x1, g_ref[1:2, :]).astype(BF16)

    def chunk(rs, first):
        a = jnp.dot(h_ref[rs, :], w1_ref[0], preferred_element_type=F32)
        a = jnp.square(jnp.maximum(a, 0.0)).astype(BF16)
        part = jnp.dot(a, w2_ref[0], preferred_element_type=F32)
        acc_ref[rs, :] = part if first else acc_ref[rs, :] + part

    def epilogue(rs):
        o_ref[rs, :] = o_ref[rs, :] + _rms(acc_ref[rs, :], g_ref[2:3, :])

    @pl.when(k == 0)
    def _():
        for rs in halves:
            prologue(rs)
            chunk(rs, True)
        if nk == 1:
            for rs in halves:
                epilogue(rs)

    @pl.when((k > 0) & (k < nk - 1))
    def _():
        chunk(slice(0, tm), False)

    @pl.when((k > 0) & (k == nk - 1))
    def _():
        for rs in halves:
            chunk(rs, False)
            epilogue(rs)


def _mix_ffn(x2d, g3rows, acts, w_out_all, w_out_idx, w1_all, w2_all, layer, tm, tf):
    t, d = x2d.shape
    ff = w1_all.shape[2]
    n_act = len(acts)
    in_specs = [pl.BlockSpec((tm, d), lambda i, k: (i, 0)), pl.BlockSpec((3, d), lambda i, k: (0, 0))]
    in_specs += [pl.BlockSpec((tm, a.shape[1]), lambda i, k: (i, 0)) for a in acts]
    in_specs += [pl.BlockSpec((1, a.shape[1], d), lambda i, k, li=li, ri=ri: (li, ri, 0))
                 for a, (li, ri) in zip(acts, w_out_idx)]
    in_specs += [pl.BlockSpec((1, d, tf), lambda i, k: (layer, 0, k)),
                 pl.BlockSpec((1, tf, d), lambda i, k: (layer, k, 0))]
    return pl.pallas_call(
        functools.partial(_mix_ffn_kernel, n_act=n_act, nk=ff // tf),
        grid=(t // tm, ff // tf),
        in_specs=in_specs,
        out_specs=pl.BlockSpec((tm, d), lambda i, k: (i, 0)),
        out_shape=jax.ShapeDtypeStruct((t, d), F32),
        scratch_shapes=[pltpu.VMEM((tm, d), BF16), pltpu.VMEM((tm, d), F32)],
        compiler_params=_cparams(("parallel", "arbitrary")),
        name="mix_ffn",
    )(x2d, g3rows, *acts, *([w_out_all] * n_act), w1_all, w2_all)


def _ind(mask):
    return jnp.where(mask, 1.0, 0.0).astype(F32)


def _div_pow2(x, d):
    return jnp.right_shift(x, int(math.log2(d)))


def _chunk_structs(rows, ns):
    lt = rows // ns
    ri = lax.broadcasted_iota(jnp.int32, (rows, rows), 0)
    ci = lax.broadcasted_iota(jnp.int32, (rows, rows), 1)
    cs = SimpleNamespace(rows=rows, ns=ns, lt=lt)
    if ns == 1:
        cs.same = None
        cs.causal = ri >= ci
        cs.strict = ri > ci
        cs.upper = ri <= ci
    else:
        rseq = _div_pow2(ri, lt)
        cseq = _div_pow2(ci, lt)
        cs.same = rseq == cseq
        cs.causal = cs.same & (ri >= ci)
        cs.strict = cs.same & (ri > ci)
        cs.upper = cs.same & (ri <= ci)
        cs.last = _ind(ci == rseq * lt + (lt - 1))
        cs.last_t = _ind(ri == cseq * lt + (lt - 1))
        sr = _div_pow2(lax.broadcasted_iota(jnp.int32, (ns * HEAD_DIM, rows), 0), HEAD_DIM)
        sc = lax.broadcasted_iota(jnp.int32, (ns * HEAD_DIM, rows), 1)
        cs.stack_last = _ind(sc == sr * lt + (lt - 1))
        cs.stack_mask = _ind(_div_pow2(sc, lt) == sr)
        er = _div_pow2(lax.broadcasted_iota(jnp.int32, (rows, ns * HEAD_DIM), 0), lt)
        ec = _div_pow2(lax.broadcasted_iota(jnp.int32, (rows, ns * HEAD_DIM), 1), HEAD_DIM)
        cs.expand_mask = _ind(er == ec)
        kr = lax.broadcasted_iota(jnp.int32, (ns, rows), 0)
        kc = lax.broadcasted_iota(jnp.int32, (ns, rows), 1)
        cs.seq_last = _ind(kc == kr * lt + (lt - 1))
        cs.seq_sum = _ind(_div_pow2(kc, lt) == kr)
        pr = _div_pow2(lax.broadcasted_iota(jnp.int32, (rows, HEAD_DIM), 0), lt)
        pc = lax.broadcasted_iota(jnp.int32, (rows, HEAD_DIM), 1)
        cs.seq_expand = _ind(pr == pc)
    cs.lower_f = _ind(cs.causal)
    cs.upper_f = _ind(cs.upper)
    eye_r = lax.broadcasted_iota(jnp.int32, (HEAD_DIM, HEAD_DIM), 0)
    eye_c = lax.broadcasted_iota(jnp.int32, (HEAD_DIM, HEAD_DIM), 1)
    cs.eye = _ind(eye_r == eye_c)
    cs.eye_rows = _ind(ri == ci)
    return cs


def _last_rows(cs, x):
    if cs.ns == 1:
        return jnp.broadcast_to(x[cs.rows - 1:cs.rows, :], x.shape)
    return _sel_mm(cs.last, x)


def _last_lanes(cs, x):
    if cs.ns == 1:
        return jnp.broadcast_to(x[:, cs.rows - 1:cs.rows], x.shape)
    return _mm_sel(x, cs.last_t)


def _expand_lhs(cs, x):
    if cs.ns == 1:
        return x
    reps = x.shape[-2] // cs.rows
    mask = cs.expand_mask if reps == 1 else jnp.concatenate([cs.expand_mask] * reps, axis=0)
    return jnp.concatenate([x] * cs.ns, axis=-1) * mask


def _stack_t(cs, x):
    xt = _mm_nt(cs.eye, x)
    if cs.ns == 1:
        return xt
    return jnp.concatenate([xt] * cs.ns, axis=0) * cs.stack_mask


def _stack_scalar(cs, col):
    if cs.ns == 1:
        return col[cs.rows - 1:cs.rows, :]
    return _sel_mm(cs.stack_last, jnp.broadcast_to(col, (cs.rows, HEAD_DIM)))[:, 0:1]


_BATCH_NN = (((2,), (1,)), ((0,), (0,)))
_BATCH_NT = (((2,), (2,)), ((0,), (0,)))
_BATCH_TN = (((1,), (1,)), ((0,), (0,)))


def _bmm(a, b, dims=_BATCH_NN):
    return lax.dot_general(a.astype(BF16), b.astype(BF16), dims, preferred_element_type=F32)


def _split_lhs(a):
    hi = a.astype(BF16).astype(F32)
    return jnp.concatenate([hi, a - hi, hi], axis=-1).astype(BF16)


def _split_rhs(b):
    hi = b.astype(BF16).astype(F32)
    return jnp.concatenate([hi, hi, b - hi], axis=-2).astype(BF16)


def _bmm_split(a_cat, b_cat):
    return lax.dot_general(a_cat, b_cat, _BATCH_NN, preferred_element_type=F32)


def _unit_lower_inverse_off(cs, a):
    rows = cs.rows
    n = -a
    steps = int(math.log2(cs.lt)) - 1
    if steps == 0:
        return n
    pt = jnp.concatenate([_bmm_split(_split_lhs(n), _split_rhs(n)), n], axis=-1)
    right = lax.broadcasted_iota(jnp.int32, (rows, 2 * rows), 1) >= rows
    for _ in range(steps):
        hi = pt.astype(BF16).astype(F32)
        lo = pt - hi
        p_cat = jnp.concatenate([hi[..., :rows], lo[..., :rows], hi[..., :rows]], axis=-1).astype(BF16)
        pt_cat = jnp.concatenate([hi, hi, lo], axis=-2).astype(BF16)
        p_both = jnp.concatenate([pt[..., :rows], pt[..., :rows]], axis=-1)
        pt = _bmm_split(p_cat, pt_cat) + jnp.where(right, pt + p_both, 0.0)
    return pt[..., rows:]


def _gdn_block(cs, nchunks, xc, gate, sm, smt, prm_row, prm_col, gng, s_heads, write_out):
    rows = cs.rows
    hq = N_HEADS * HEAD_DIM
    pairs = [(c, h) for c in range(nchunks) for h in range(N_HEADS)]
    rs = lambda c: slice(c * rows, (c + 1) * rows)

    def heads(x, col0):
        return jnp.stack([x[rs(c), col0 + h * HEAD_DIM:col0 + (h + 1) * HEAD_DIM] for c, h in pairs])

    def cols(x, lane0):
        return jnp.stack([x[rs(c), lane0 + h:lane0 + h + 1] for c, h in pairs])

    beta_all = _sigmoid(sm)
    g_c = -prm_row[0:1, :] * _softplus(sm + prm_row[1:2, :])
    g_r = -prm_col[:, 0:1] * _softplus(smt + prm_col[:, 1:2])
    gc_c = jnp.concatenate([_sel_mm(cs.lower_f, g_c[rs(c)]) for c in range(nchunks)], axis=0)
    gc_r = jnp.concatenate([_mm_sel(g_r[:, rs(c)], cs.upper_f) for c in range(nchunks)], axis=1)
    gl_c = jnp.concatenate([_last_rows(cs, gc_c[rs(c)]) for c in range(nchunks)], axis=0)
    egc = jnp.exp(gc_c)
    ekd = jnp.exp(gl_c - gc_c)

    q = _l2norm(heads(xc, 0)) * (HEAD_DIM ** -0.5)
    k = _l2norm(heads(xc, hq))
    v = heads(xc, 2 * hq)
    beta = cols(beta_all, 0)
    gcol = cols(gc_c, N_HEADS)
    eg = cols(egc, N_HEADS)
    grow = jnp.stack([gc_r[N_HEADS + h:N_HEADS + h + 1, rs(c)] for c, h in pairs])
    decay = jnp.where(cs.causal, jnp.exp(jnp.where(cs.causal, gcol - grow, 0.0)), 0.0)
    kb = k * beta
    a = jnp.where(cs.strict, _bmm(kb, k, _BATCH_NT) * decay, 0.0)
    toff = _unit_lower_inverse_off(cs, a)
    rhs = jnp.concatenate([v * beta, kb * eg], axis=-1)
    sol = rhs + _bmm_split(_split_lhs(toff), _split_rhs(rhs))
    u_val, w_k = sol[..., :HEAD_DIM], sol[..., HEAD_DIM:]
    qk = jnp.where(cs.causal, _bmm(q, k, _BATCH_NT) * decay, 0.0)
    q_dec = q * eg
    k_dec = k * cols(ekd, N_HEADS)

    s = s_heads
    for c in range(nchunks):
        hs = slice(c * N_HEADS, (c + 1) * N_HEADS)
        p = _bmm(_expand_lhs(cs, jnp.concatenate([w_k[hs], q_dec[hs]], axis=1)), s)
        v_new = u_val[hs] - p[:, :rows]
        o = p[:, rows:] + _bmm(qk[hs], v_new)
        gl_cols = [gl_c[rs(c), N_HEADS + h:N_HEADS + h + 1] for h in range(N_HEADS)]
        g_last = jnp.exp(jnp.stack([_stack_scalar(cs, col) for col in gl_cols]))
        if cs.ns == 1:
            upd = _bmm(k_dec[hs], v_new, _BATCH_TN)
        else:
            kt = _bmm(jnp.broadcast_to(cs.eye, (N_HEADS, HEAD_DIM, HEAD_DIM)), k_dec[hs], _BATCH_NT)
            upd = _bmm(jnp.concatenate([kt] * cs.ns, axis=1) * cs.stack_mask, v_new)
        s = s * g_last + upd
        for h in range(N_HEADS):
            write_out(c, h, _rms(o[h], gng) * _silu(gate[rs(c), h * HEAD_DIM:(h + 1) * HEAD_DIM]))
    return s


def _project_block(x_ref, g_ref, w_ref, wst_ref, proj_w, smt_w):
    hn = _rms(x_ref[...], g_ref[...]).astype(BF16)
    proj_w[...] = jnp.dot(hn, w_ref[0], preferred_element_type=F32)
    smt_w[...] = lax.dot_general(wst_ref[0], hn, (((1,), (1,)), ((), ())), preferred_element_type=F32)


def _pipelined_steps(step):
    i = pl.program_id(0)
    for parity in range(2):
        @pl.when(lax.rem(i, 2) == parity)
        def _():
            step(parity)


def _even_prompt_kernel(x_ref, g_ref, w_ref, wst_ref, cw_ref, prow_ref, pcol_ref, gng_ref, s0_ref, buf_ref,
                        sw_ref, sb_ref, sg_ref, o_ref, sout_ref, bufout_ref,
                        proj_a, proj_b, smt_a, smt_b, s_scr, xp_scr, *, tb, nt):
    i = pl.program_id(0)
    t = lax.rem(jnp.maximum(i - 1, 0), nt)
    hq = N_HEADS * HEAD_DIM
    cch = 3 * hq
    cs = _chunk_structs(CHUNK_ROWS, 1)

    @pl.when(i == 0)
    def _():
        proj_b[...] = jnp.zeros_like(proj_b)
        smt_b[...] = jnp.zeros_like(smt_b)

    @pl.when(t == 0)
    def _():
        s_scr[...] = s0_ref[0]
        xp_scr[...] = buf_ref[0]

    def write_delta(c, h, o):
        o_ref[c * CHUNK_ROWS:(c + 1) * CHUNK_ROWS, h * HEAD_DIM:(h + 1) * HEAD_DIM] = o

    def write_gating(rs, g, val):
        o_ref[rs, hq + g * HEAD_DIM:hq + (g + 1) * HEAD_DIM] = val

    def step(parity):
        proj_w, smt_w, proj_r, smt_r = ((proj_a, smt_a, proj_b, smt_b) if parity == 0
                                        else (proj_b, smt_b, proj_a, smt_a))
        _project_block(x_ref, g_ref, w_ref, wst_ref, proj_w, smt_w)
        x = proj_r[:, 0:cch]
        xp = jnp.concatenate([xp_scr[...], x], axis=0)
        y = cw_ref[CONV_W - 1:CONV_W, :] * x
        for back in range(1, CONV_W):
            y = y + cw_ref[CONV_W - 1 - back:CONV_W - back, :] * pltpu.roll(xp, back, axis=0)[8:]
        xp_scr[...] = x[tb - 8:]
        s_scr[...] = _gdn_block(cs, tb // CHUNK_ROWS, _silu(y), proj_r[:, cch:cch + hq],
                                proj_r[:, cch + 3 * hq:cch + 3 * hq + SMALL_COLS], smt_r[...],
                                prow_ref[...], pcol_ref[...], gng_ref[...], s_scr[...], write_delta)
        _sgu_apply(proj_r[:, cch + hq:cch + 2 * hq], proj_r[:, cch + 2 * hq:cch + 3 * hq],
                   sw_ref, sb_ref[...], sg_ref[...], SGU_CHUNK, 1, write_gating, None)

    _pipelined_steps(step)

    @pl.when(t == nt - 1)
    def _():
        sout_ref[0] = s_scr[...]
        bufout_ref[0] = xp_scr[...]


def _even_prompt(x2, g, w_main_all, w_small_t_all, e, conv_w, prm_row, prm_col, gng, s0, buf8,
                 sgu_w, sgu_b_cols, sgu_g, bsz, seq, tb):
    t, d = x2.shape
    hq = N_HEADS * HEAD_DIM
    cch = 3 * hq
    n_proj = w_main_all.shape[2]
    n = t // tb
    nt = seq // tb
    blk = lambda i: jnp.minimum(i, n - 1)
    seq_of = lambda i: jnp.maximum(i - 1, 0) // nt
    const2 = lambda i: (0, 0)
    return pl.pallas_call(
        functools.partial(_even_prompt_kernel, tb=tb, nt=nt),
        grid=(n + 1,),
        in_specs=[pl.BlockSpec((tb, d), lambda i: (blk(i), 0)),
                  pl.BlockSpec((1, d), const2),
                  pl.BlockSpec((1, d, n_proj), lambda i: (e, 0, 0)),
                  pl.BlockSpec((1, SMALL_ROWS, d), lambda i: (e, 0, 0)),
                  pl.BlockSpec((CONV_W, cch), const2),
                  pl.BlockSpec((2, SMALL_COLS), const2),
                  pl.BlockSpec((SMALL_ROWS, 2), const2),
                  pl.BlockSpec((1, HEAD_DIM), const2),
                  pl.BlockSpec((1, N_HEADS, HEAD_DIM, HEAD_DIM), lambda i: (seq_of(i), 0, 0, 0)),
                  pl.BlockSpec((1, 8, cch), lambda i: (seq_of(i), 0, 0)),
                  pl.BlockSpec((N_HEADS, SGU_CHUNK, SGU_CHUNK), lambda i: (0, 0, 0)),
                  pl.BlockSpec((SGU_CHUNK, SMALL_COLS), const2),
                  pl.BlockSpec((N_HEADS, HEAD_DIM), const2)],
        out_specs=[pl.BlockSpec((tb, 2 * hq), lambda i: (jnp.maximum(i - 1, 0), 0)),
                   pl.BlockSpec((1, N_HEADS, HEAD_DIM, HEAD_DIM), lambda i: (seq_of(i), 0, 0, 0)),
                   pl.BlockSpec((1, 8, cch), lambda i: (seq_of(i), 0, 0))],
        out_shape=[jax.ShapeDtypeStruct((t, 2 * hq), F32),
                   jax.ShapeDtypeStruct((bsz, N_HEADS, HEAD_DIM, HEAD_DIM), F32),
                   jax.ShapeDtypeStruct((bsz, 8, cch), F32)],
        scratch_shapes=[pltpu.VMEM((tb, n_proj), F32), pltpu.VMEM((tb, n_proj), F32),
                        pltpu.VMEM((SMALL_ROWS, tb), F32), pltpu.VMEM((SMALL_ROWS, tb), F32),
                        pltpu.VMEM((N_HEADS, HEAD_DIM, HEAD_DIM), F32),
                        pltpu.VMEM((8, cch), F32)],
        compiler_params=_cparams(("arbitrary",)),
        name="even_prompt",
    )(x2, g.reshape(1, d), w_main_all, w_small_t_all, conv_w, prm_row, prm_col,
      gng.reshape(1, HEAD_DIM), s0, buf8, sgu_w, sgu_b_cols, sgu_g)


def _conv_sample_kernel(xcat_ref, cw_ref, o_ref, *, lt, cch):
    for t in range(lt):
        y = None
        for j in range(CONV_W):
            term = cw_ref[j:j + 1, :] * xcat_ref[:, (t + j) * cch:(t + j + 1) * cch]
            y = term if y is None else y + term
        o_ref[:, t * cch:(t + 1) * cch] = _silu(y)


def _conv_sample(xcat, conv_w, lt, cch):
    nseq = xcat.shape[0]
    return pl.pallas_call(
        functools.partial(_conv_sample_kernel, lt=lt, cch=cch),
        grid=(1,),
        in_specs=[pl.BlockSpec(xcat.shape, lambda i: (0, 0)),
                  pl.BlockSpec((CONV_W, cch), lambda i: (0, 0))],
        out_specs=pl.BlockSpec((nseq, lt * cch), lambda i: (0, 0)),
        out_shape=jax.ShapeDtypeStruct((nseq, lt * cch), F32),
        compiler_params=_cparams(("arbitrary",)),
        name="conv_sample",
    )(xcat, conv_w)


def _gdn_sample_kernel(xc_ref, gate_ref, sm_ref, smt_ref, prow_ref, pcol_ref, gng_ref, s0_ref,
                       *rest, ns):
    o_ref, sout_ref = rest[-2:]
    cs = _chunk_structs(CHUNK_ROWS, ns)
    s_heads = jnp.stack([s0_ref[0, :, h].reshape(ns * HEAD_DIM, HEAD_DIM) for h in range(N_HEADS)])

    def write_out(c, h, o):
        o_ref[:, h * HEAD_DIM:(h + 1) * HEAD_DIM] = o

    s_new = _gdn_block(cs, 1, xc_ref[...], gate_ref[...], sm_ref[...], smt_ref[0],
                       prow_ref[...], pcol_ref[...], gng_ref[...], s_heads, write_out)
    for h in range(N_HEADS):
        sout_ref[0, :, h] = s_new[h].reshape(ns, HEAD_DIM, HEAD_DIM)


def _gdn_sample(xc, proj, smt, prm_row, prm_col, gng, s_all, layer, s_out_prev, lt):
    hq = N_HEADS * HEAD_DIM
    cch = 3 * hq
    ns = CHUNK_ROWS // lt
    nchunk = xc.shape[0] // CHUNK_ROWS
    small_blk = (cch + 3 * hq) // SMALL_COLS
    state_spec = pl.BlockSpec((1, ns, N_HEADS, HEAD_DIM, HEAD_DIM), lambda c: (layer, c, 0, 0, 0))
    in_specs = [pl.BlockSpec((CHUNK_ROWS, cch), lambda c: (c, 0)),
                pl.BlockSpec((CHUNK_ROWS, hq), lambda c: (c, cch // hq)),
                pl.BlockSpec((CHUNK_ROWS, SMALL_COLS), lambda c: (c, small_blk)),
                pl.BlockSpec((1, SMALL_ROWS, CHUNK_ROWS), lambda c: (c, 0, 0)),
                pl.BlockSpec((2, SMALL_COLS), lambda c: (0, 0)),
                pl.BlockSpec((SMALL_ROWS, 2), lambda c: (0, 0)),
                pl.BlockSpec((1, HEAD_DIM), lambda c: (0, 0)),
                state_spec]
    args = [xc, proj, proj, smt, prm_row, prm_col, gng.reshape(1, HEAD_DIM), s_all]
    aliases = {len(args): 1}
    in_specs.append(pl.BlockSpec(memory_space=pl.ANY))
    args.append(s_out_prev)
    return pl.pallas_call(
        functools.partial(_gdn_sample_kernel, ns=ns),
        grid=(nchunk,),
        in_specs=in_specs,
        out_specs=[pl.BlockSpec((CHUNK_ROWS, hq), lambda c: (c, 0)), state_spec],
        out_shape=[jax.ShapeDtypeStruct((xc.shape[0], hq), F32),
                   jax.ShapeDtypeStruct(s_all.shape, F32)],
        input_output_aliases=aliases,
        compiler_params=_cparams(("parallel",)),
        name="gdn_sample",
    )(*args)


def _sgu_apply(u_pre, v_pre, w_ref, b_cols, g_rows, rows, ns, write_o, write_vb):
    lt = rows // ns
    ri = lax.broadcasted_iota(jnp.int32, (rows, rows), 0)
    ci = lax.broadcasted_iota(jnp.int32, (rows, rows), 1)
    keep = ri >= ci
    if ns > 1:
        keep = keep & (_div_pow2(ri, lt) == _div_pow2(ci, lt))
    for g in range(N_HEADS):
        lo = g * HEAD_DIM
        w = jnp.where(keep, w_ref[g], 0.0).astype(BF16)
        u = _gelu(u_pre[:, lo:lo + HEAD_DIM])
        vb = _rms(_gelu(v_pre[:, lo:lo + HEAD_DIM]), g_rows[g:g + 1, :])
        if write_vb is not None:
            write_vb(g, vb)
        for c in range(u_pre.shape[0] // rows):
            rs = slice(c * rows, (c + 1) * rows)
            write_o(rs, g, u[rs] * (_mm(w, vb[rs]) + b_cols[:, g:g + 1]))


def _sgu_kernel(u_ref, v_ref, w_ref, b_ref, g_ref, o_ref, vb_ref, *, rows, ns):
    def write_o(rs, g, val):
        o_ref[rs, g * HEAD_DIM:(g + 1) * HEAD_DIM] = val

    def write_vb(g, val):
        vb_ref[:, g * HEAD_DIM:(g + 1) * HEAD_DIM] = val

    _sgu_apply(u_ref[...], v_ref[...], w_ref, b_ref[...], g_ref[...], rows, ns, write_o, write_vb)


def _sgu(proj, w_tiles, b_cols, norm_g, rows, ns, nchunks):
    hq = N_HEADS * HEAD_DIM
    t = proj.shape[0]
    u_blk = (3 * hq + hq) // hq
    blk = rows * nchunks
    return pl.pallas_call(
        functools.partial(_sgu_kernel, rows=rows, ns=ns),
        grid=(t // blk,),
        in_specs=[pl.BlockSpec((blk, hq), lambda i: (i, u_blk)),
                  pl.BlockSpec((blk, hq), lambda i: (i, u_blk + 1)),
                  pl.BlockSpec((N_HEADS, rows, rows), lambda i: (0, 0, 0)),
                  pl.BlockSpec((rows, SMALL_COLS), lambda i: (0, 0)),
                  pl.BlockSpec((N_HEADS, HEAD_DIM), lambda i: (0, 0))],
        out_specs=[pl.BlockSpec((blk, hq), lambda i: (i, 0)),
                   pl.BlockSpec((blk, hq), lambda i: (i, 0))],
        out_shape=[jax.ShapeDtypeStruct((t, hq), F32), jax.ShapeDtypeStruct((t, hq), F32)],
        compiler_params=_cparams(("parallel",)),
        name="sgu",
    )(proj, proj, w_tiles, b_cols, norm_g)


def _logsigmoid(x):
    return jnp.minimum(x, 0.0) - jnp.log1p(jnp.exp(-jnp.abs(x)))


def _mlstm_chunk(cs, q_all, k_all, v_all, opre, sm, smt, gb_row, gb_col, mng, c_list, n_rows, m_rows):
    neg_inf = -jnp.inf
    pre_c = sm + gb_row
    pre_r = smt + gb_col
    b_c = _sel_mm(cs.lower_f, _logsigmoid(pre_c))
    b_r = _mm_sel(_logsigmoid(pre_r), cs.upper_f)
    bl_c = _last_rows(cs, b_c)
    bl_r = _last_lanes(cs, b_r)
    res = []
    for h in range(N_HEADS):
        lo = h * HEAD_DIM
        vo = h * MLSTM_DV
        q = q_all[:, lo:lo + HEAD_DIM] * (HEAD_DIM ** -0.5)
        k = k_all[:, lo:lo + HEAD_DIM]
        v = v_all[:, vo:vo + MLSTM_DV]
        bcol = b_c[:, N_HEADS + h:N_HEADS + h + 1]
        brow = b_r[N_HEADS + h:N_HEADS + h + 1, :]
        icol = pre_c[:, h:h + 1]
        irow = pre_r[h:h + 1, :]
        blcol = bl_c[:, N_HEADS + h:N_HEADS + h + 1]
        blrow = bl_r[N_HEADS + h:N_HEADS + h + 1, :]
        mrow = m_rows[:, h:h + 1]
        inter = bcol + mrow
        dmat = jnp.where(cs.causal, bcol - brow + irow, neg_inf)
        m_t = jnp.maximum(inter, jnp.max(dmat, axis=-1, keepdims=True))
        w_intra = jnp.exp(dmat - m_t)
        w_inter = jnp.exp(inter - m_t)
        s = _mm_nt(q, k) * w_intra
        c_old = c_list[h]
        num = w_inter * _mm(_expand_lhs(cs, q), c_old) + _mm(s, v)
        qn = jnp.sum(q * n_rows[:, lo:lo + HEAD_DIM], axis=-1, keepdims=True)
        den = w_inter * qn + jnp.sum(s, axis=-1, keepdims=True)
        hval = num / jnp.maximum(jnp.abs(den), jnp.exp(-m_t))
        out = _rms(hval, mng) * _sigmoid(opre[:, vo:vo + MLSTM_DV])
        logw_row = blrow - brow + irow
        if cs.ns == 1:
            seq_max = jnp.max(logw_row, axis=-1, keepdims=True)
        else:
            seq_max = jnp.max(jnp.where(cs.same, logw_row, neg_inf), axis=-1, keepdims=True)
        m_new = jnp.maximum(blcol + mrow, seq_max)
        keep = jnp.exp(blcol + mrow - m_new)
        wk = jnp.exp(blcol - bcol + icol - m_new)
        kw = k * wk
        c_new = c_old * _stack_scalar(cs, keep) + _mm(_stack_t(cs, kw), v)
        res.append((out, c_new, keep, m_new, kw))
    return res


def _mlstm_prompt_block(cs, nchunks, q_all, k_all, v_all, opre, sm, smt, gb_row, gb_col, mng,
                        c_heads, n_heads, m_heads, write_out):
    rows = cs.rows
    pairs = [(c, h) for c in range(nchunks) for h in range(N_HEADS)]
    rs = lambda c: slice(c * rows, (c + 1) * rows)

    def heads(x, width):
        return jnp.stack([x[rs(c), h * width:(h + 1) * width] for c, h in pairs])

    def cols(x, lane0):
        return jnp.stack([x[rs(c), lane0 + h:lane0 + h + 1] for c, h in pairs])

    def lanes(x, row0):
        return jnp.stack([x[row0 + h:row0 + h + 1, rs(c)] for c, h in pairs])

    pre_c = sm + gb_row
    pre_r = smt + gb_col
    lf_c = _logsigmoid(pre_c)
    lf_r = _logsigmoid(pre_r)
    b_c = jnp.concatenate([_sel_mm(cs.lower_f, lf_c[rs(c)]) for c in range(nchunks)], axis=0)
    b_r = jnp.concatenate([_mm_sel(lf_r[:, rs(c)], cs.upper_f) for c in range(nchunks)], axis=1)

    q = heads(q_all, HEAD_DIM) * (HEAD_DIM ** -0.5)
    k = heads(k_all, HEAD_DIM)
    v = heads(v_all, MLSTM_DV)
    bcol = cols(b_c, N_HEADS)
    icol = cols(pre_c, 0)
    brow = lanes(b_r, N_HEADS)
    irow = lanes(pre_r, 0)
    blast = bcol[:, rows - 1:rows, :]
    dmat = jnp.where(cs.causal, bcol - brow + irow, -jnp.inf)
    rmax = jnp.max(dmat, axis=-1, keepdims=True)
    sp = _bmm(q, k, _BATCH_NT) * jnp.exp(dmat - rmax)
    sv = _bmm(sp, v)
    ssum = jnp.sum(sp, axis=-1, keepdims=True)
    lmax = jnp.max(blast - brow + irow, axis=-1, keepdims=True)
    kwp = k * jnp.exp(blast - bcol + icol - lmax)
    upd = _bmm(kwp, v, _BATCH_TN)
    nsum = jnp.sum(kwp, axis=1, keepdims=True)

    cst, n, m = c_heads, n_heads, m_heads
    for c in range(nchunks):
        hs = slice(c * N_HEADS, (c + 1) * N_HEADS)
        inter = bcol[hs] + m
        m_t = jnp.maximum(inter, rmax[hs])
        w_inter = jnp.exp(inter - m_t)
        w_intra = jnp.exp(rmax[hs] - m_t)
        num = w_inter * _bmm(q[hs], cst) + w_intra * sv[hs]
        den = (w_inter * jnp.sum(q[hs] * n, axis=-1, keepdims=True) + w_intra * ssum[hs])
        hval = num / jnp.maximum(jnp.abs(den), jnp.exp(-m_t))
        for h in range(N_HEADS):
            write_out(c, h, _rms(hval[h], mng) * _sigmoid(opre[rs(c), h * MLSTM_DV:(h + 1) * MLSTM_DV]))
        m_new = jnp.maximum(blast[hs] + m, lmax[hs])
        keep = jnp.exp(blast[hs] + m - m_new)
        fresh = jnp.exp(lmax[hs] - m_new)
        cst = keep * cst + fresh * upd[hs]
        n = keep * n + fresh * nsum[hs]
        m = m_new
    return cst, n, m


def _odd_prompt_kernel(x_ref, g_ref, w_ref, wst_ref, gbr_ref, gbc_ref, mng_ref, c0_ref, n0_ref, m0_ref,
                       o_ref, cout_ref, nout_ref, mout_ref,
                       proj_a, proj_b, smt_a, smt_b, c_scr, n_scr, m_scr, *, tb, nt):
    i = pl.program_id(0)
    t = lax.rem(jnp.maximum(i - 1, 0), nt)
    hq = N_HEADS * HEAD_DIM
    hv = N_HEADS * MLSTM_DV
    cs = _chunk_structs(CHUNK_ROWS, 1)
    lane = lax.broadcasted_iota(jnp.int32, (1, SMALL_COLS), 1)

    @pl.when(i == 0)
    def _():
        proj_b[...] = jnp.zeros_like(proj_b)
        smt_b[...] = jnp.zeros_like(smt_b)

    @pl.when(t == 0)
    def _():
        c_scr[...] = c0_ref[0]
        n_scr[...] = n0_ref[0]
        m_scr[...] = m0_ref[0]

    def write_out(c, h, o):
        o_ref[c * CHUNK_ROWS:(c + 1) * CHUNK_ROWS, h * MLSTM_DV:(h + 1) * MLSTM_DV] = o

    def step(parity):
        proj_w, smt_w, proj_r, smt_r = ((proj_a, smt_a, proj_b, smt_b) if parity == 0
                                        else (proj_b, smt_b, proj_a, smt_a))
        _project_block(x_ref, g_ref, w_ref, wst_ref, proj_w, smt_w)
        n_cur = n_scr[...]
        m_cur = m_scr[...]
        n_heads = jnp.stack([n_cur[:, h * HEAD_DIM:(h + 1) * HEAD_DIM] for h in range(N_HEADS)])
        m_heads = jnp.stack([m_cur[:, h:h + 1] for h in range(N_HEADS)])
        c_new, n_heads, m_heads = _mlstm_prompt_block(
            cs, tb // CHUNK_ROWS, proj_r[:, 0:hq], proj_r[:, hq:2 * hq], proj_r[:, 2 * hq:2 * hq + hv],
            proj_r[:, 2 * hq + hv:2 * hq + 2 * hv],
            proj_r[:, 2 * hq + 2 * hv:2 * hq + 2 * hv + SMALL_COLS], smt_r[...],
            gbr_ref[...], gbc_ref[...], mng_ref[...], c_scr[...], n_heads, m_heads, write_out)
        c_scr[...] = c_new
        n_scr[...] = jnp.concatenate([n_heads[h] for h in range(N_HEADS)], axis=1)
        for h in range(N_HEADS):
            m_cur = jnp.where(lane == h, m_heads[h], m_cur)
        m_scr[...] = m_cur

    _pipelined_steps(step)

    @pl.when(t == nt - 1)
    def _():
        cout_ref[0] = c_scr[...]
        nout_ref[0] = n_scr[...]
        mout_ref[0] = m_scr[...]


def _odd_prompt(x2, g, w_main_all, w_small_t_all, o, gb_row, gb_col, mng, c0, n0, m0, bsz, seq, tb):
    t, d = x2.shape
    hq = N_HEADS * HEAD_DIM
    hv = N_HEADS * MLSTM_DV
    n_proj = w_main_all.shape[2]
    n = t // tb
    nt = seq // tb
    blk = lambda i: jnp.minimum(i, n - 1)
    seq_of = lambda i: jnp.maximum(i - 1, 0) // nt
    const2 = lambda i: (0, 0)
    state_c = pl.BlockSpec((1, N_HEADS, HEAD_DIM, MLSTM_DV), lambda i: (seq_of(i), 0, 0, 0))
    state_n = pl.BlockSpec((1, 1, hq), lambda i: (seq_of(i), 0, 0))
    state_m = pl.BlockSpec((1, 1, SMALL_COLS), lambda i: (seq_of(i), 0, 0))
    return pl.pallas_call(
        functools.partial(_odd_prompt_kernel, tb=tb, nt=nt),
        grid=(n + 1,),
        in_specs=[pl.BlockSpec((tb, d), lambda i: (blk(i), 0)),
                  pl.BlockSpec((1, d), const2),
                  pl.BlockSpec((1, d, n_proj), lambda i: (o, 0, 0)),
                  pl.BlockSpec((1, SMALL_ROWS, d), lambda i: (o, 0, 0)),
                  pl.BlockSpec((1, SMALL_COLS), const2),
                  pl.BlockSpec((SMALL_ROWS, 1), const2),
                  pl.BlockSpec((1, MLSTM_DV), const2),
                  state_c, state_n, state_m],
        out_specs=[pl.BlockSpec((tb, hv), lambda i: (jnp.maximum(i - 1, 0), 0)),
                   state_c, state_n, state_m],
        out_shape=[jax.ShapeDtypeStruct((t, hv), F32),
                   jax.ShapeDtypeStruct(c0.shape, F32),
                   jax.ShapeDtypeStruct(n0.shape, F32),
                   jax.ShapeDtypeStruct(m0.shape, F32)],
        scratch_shapes=[pltpu.VMEM((tb, n_proj), F32), pltpu.VMEM((tb, n_proj), F32),
                        pltpu.VMEM((SMALL_ROWS, tb), F32), pltpu.VMEM((SMALL_ROWS, tb), F32),
                        pltpu.VMEM((N_HEADS, HEAD_DIM, MLSTM_DV), F32),
                        pltpu.VMEM((1, hq), F32),
                        pltpu.VMEM((1, SMALL_COLS), F32)],
        compiler_params=_cparams(("arbitrary",)),
        name="odd_prompt",
    )(x2, g.reshape(1, d), w_main_all, w_small_t_all, gb_row, gb_col, mng.reshape(1, MLSTM_DV), c0, n0, m0)


def _mlstm_sample_kernel(q_ref, k_ref, v_ref, op_ref, sm_ref, smt_ref, gbr_ref, gbc_ref, mng_ref,
                         c0_ref, n0_ref, m0_ref, *rest, ns):
    o_ref, cout_ref, nout_ref, mout_ref = rest[-4:]
    cs = _chunk_structs(CHUNK_ROWS, ns)
    lane = lax.broadcasted_iota(jnp.int32, (1, SMALL_COLS), 1)
    c_list = [c0_ref[0, :, h].reshape(ns * HEAD_DIM, MLSTM_DV) for h in range(N_HEADS)]
    n0 = n0_ref[...]
    m0 = m0_ref[...]
    n_rows = _sel_mm(cs.seq_expand, jnp.concatenate(
        [n0, jnp.zeros((HEAD_DIM - ns, n0.shape[1]), F32)], axis=0))
    m_rows = _sel_mm(cs.seq_expand, jnp.concatenate(
        [m0, jnp.zeros((HEAD_DIM - ns, m0.shape[1]), F32)], axis=0))
    res = _mlstm_chunk(cs, q_ref[...], k_ref[...], v_ref[...], op_ref[...], sm_ref[...], smt_ref[0],
                       gbr_ref[...], gbc_ref[...], mng_ref[...], c_list, n_rows, m_rows)
    m_all = jnp.zeros((CHUNK_ROWS, SMALL_COLS), F32)
    for h, (out, c_new, keep, m_new, kw) in enumerate(res):
        o_ref[:, h * MLSTM_DV:(h + 1) * MLSTM_DV] = out
        cout_ref[0, :, h] = c_new.reshape(ns, HEAD_DIM, MLSTM_DV)
        keep_seq = _sel_mm(cs.seq_last, jnp.broadcast_to(keep, (CHUNK_ROWS, HEAD_DIM)))
        nout_ref[:, h * HEAD_DIM:(h + 1) * HEAD_DIM] = (
            keep_seq * n0[:, h * HEAD_DIM:(h + 1) * HEAD_DIM] + _sel_mm(cs.seq_sum, kw))
        m_all = jnp.where(lane == h, m_new, m_all)
    mout_ref[...] = _sel_mm(cs.seq_last, m_all)


def _mlstm_sample(proj, smt, gb_row, gb_col, mng, c_all, layer, c_out_prev, n0, m0, lt):
    hq = N_HEADS * HEAD_DIM
    hv = N_HEADS * MLSTM_DV
    ns = CHUNK_ROWS // lt
    t = proj.shape[0]
    small_blk = (2 * hq + 2 * hv) // SMALL_COLS
    state_c = pl.BlockSpec((1, ns, N_HEADS, HEAD_DIM, MLSTM_DV), lambda c: (layer, c, 0, 0, 0))
    state_n = pl.BlockSpec((ns, hq), lambda c: (c, 0))
    state_m = pl.BlockSpec((ns, SMALL_COLS), lambda c: (c, 0))
    in_specs = [pl.BlockSpec((CHUNK_ROWS, hq), lambda c: (c, 0)),
                pl.BlockSpec((CHUNK_ROWS, hq), lambda c: (c, 1)),
                pl.BlockSpec((CHUNK_ROWS, hv), lambda c: (c, 2 * hq // hv)),
                pl.BlockSpec((CHUNK_ROWS, hv), lambda c: (c, 2 * hq // hv + 1)),
                pl.BlockSpec((CHUNK_ROWS, SMALL_COLS), lambda c: (c, small_blk)),
                pl.BlockSpec((1, SMALL_ROWS, CHUNK_ROWS), lambda c: (c, 0, 0)),
                pl.BlockSpec((1, SMALL_COLS), lambda c: (0, 0)),
                pl.BlockSpec((SMALL_ROWS, 1), lambda c: (0, 0)),
                pl.BlockSpec((1, MLSTM_DV), lambda c: (0, 0)),
                state_c, state_n, state_m]
    args = [proj, proj, proj, proj, proj, smt, gb_row, gb_col, mng.reshape(1, MLSTM_DV), c_all, n0, m0]
    aliases = {len(args): 1}
    in_specs.append(pl.BlockSpec(memory_space=pl.ANY))
    args.append(c_out_prev)
    return pl.pallas_call(
        functools.partial(_mlstm_sample_kernel, ns=ns),
        grid=(t // CHUNK_ROWS,),
        in_specs=in_specs,
        out_specs=[pl.BlockSpec((CHUNK_ROWS, hv), lambda c: (c, 0)), state_c, state_n, state_m],
        out_shape=[jax.ShapeDtypeStruct((t, hv), F32),
                   jax.ShapeDtypeStruct(c_all.shape, F32),
                   jax.ShapeDtypeStruct(n0.shape, F32),
                   jax.ShapeDtypeStruct(m0.shape, F32)],
        input_output_aliases=aliases,
        compiler_params=_cparams(("parallel",)),
        name="mlstm_sample",
    )(*args)


def _pad_cols(w, n):
    return jnp.pad(w, ((0, 0), (0, n - w.shape[1])))


def _split_in_weight_kernel(w_ref, main_ref, st_ref, *, n_a, n_small):
    w = w_ref[0]
    rows = w.shape[0]
    n_b = w.shape[1] - n_a - n_small
    main_ref[0, :, 0:n_a] = w[:, 0:n_a].astype(BF16)
    main_ref[0, :, n_a:n_a + n_b] = w[:, n_a + n_small:].astype(BF16)
    small = jnp.concatenate([w[:, n_a:n_a + n_small], jnp.zeros((rows, SMALL_COLS - n_small), F32)],
                            axis=1)
    main_ref[0, :, n_a + n_b:] = small.astype(BF16)
    st_ref[0] = small.T[0:SMALL_ROWS, :].astype(BF16)


def _split_in_weight(w, n_a, n_small):
    layers, d, n_tot = w.shape
    n_main = n_tot - n_small + SMALL_COLS
    rows = 256
    return pl.pallas_call(
        functools.partial(_split_in_weight_kernel, n_a=n_a, n_small=n_small),
        grid=(layers, d // rows),
        in_specs=[pl.BlockSpec((1, rows, n_tot), lambda l, r: (l, r, 0))],
        out_specs=[pl.BlockSpec((1, rows, n_main), lambda l, r: (l, r, 0)),
                   pl.BlockSpec((1, SMALL_ROWS, rows), lambda l, r: (l, 0, r))],
        out_shape=[jax.ShapeDtypeStruct((layers, d, n_main), BF16),
                   jax.ShapeDtypeStruct((layers, SMALL_ROWS, d), BF16)],
        compiler_params=_cparams(("parallel", "parallel")),
        name="split_in_weight",
    )(w)


def _chunk_lanes(smt):
    rows, t = smt.shape
    return smt.reshape(rows, t // CHUNK_ROWS, CHUNK_ROWS).transpose(1, 0, 2)


def _row_col_params(vals):
    row = jnp.stack([jnp.pad(v, (0, SMALL_COLS - v.shape[0])) for v in vals]).astype(F32)
    col = jnp.stack([jnp.pad(v, (0, SMALL_ROWS - v.shape[0])) for v in vals], axis=1).astype(F32)
    return row, col


def _trunk(x, is_prompt, s_gdn, s_conv, s_c, s_n, s_m, norm_g, w_in_even, conv_w, a_log, dt_bias,
           gdn_norm_g, sgu_norm_g, sgu_w, sgu_b, w_out_even, w_in_odd, gate_b_odd, mlstm_norm_g,
           w_out_odd, w_ff1, w_ff2):
    bsz, seq, d = x.shape
    t = bsz * seq
    depth = norm_g.shape[0]
    hq = N_HEADS * HEAD_DIM
    cch = 3 * hq
    tm = 1024 if t % 1024 == 0 else 512
    tb = min(512, seq)
    zeros_h = jnp.zeros((N_HEADS,), F32)
    x2 = x.reshape(t, d)
    new_gdn, new_conv, new_v, new_c, new_n, new_m = [], [], [], [], [], []
    gdn_out = None if is_prompt else jnp.zeros_like(s_gdn)
    c_out = None if is_prompt else jnp.zeros_like(s_c)
    for l in range(depth):
        gl = norm_g[l]
        if l % 2 == 0:
            e = l // 2
            prm_row, prm_col = _row_col_params([jnp.concatenate([zeros_h, jnp.exp(a_log[e])]),
                                                jnp.concatenate([zeros_h, dt_bias[e]])])
            if is_prompt:
                buf8 = jnp.pad(s_conv[e], ((0, 0), (8 - (CONV_W - 1), 0), (0, 0)))
                o_mix, s_new, tail = _even_prompt(
                    x2, gl[0], *w_in_even, e, conv_w[e], prm_row, prm_col, gdn_norm_g[e], s_gdn[e], buf8,
                    sgu_w[e], _pad_cols(sgu_b[e].T, SMALL_COLS), sgu_norm_g[e], bsz, seq, tb)
                buf_new = tail[:, 8 - (CONV_W - 1):, :]
                new_gdn.append(s_new)
                mix_acts, mix_idx = [o_mix], [(e, 0)]
            else:
                proj, smt = _proj_in(x2, gl[0], *w_in_even, e, 512)
                qkv = proj[:, :cch].reshape(bsz, seq, cch)
                xcat = jnp.concatenate([s_conv[e], qkv], axis=1)
                buf_new = xcat[:, seq:, :]
                xc = _conv_sample(xcat.reshape(bsz, (CONV_W - 1 + seq) * cch), conv_w[e], seq, cch)
                o_a, gdn_out = _gdn_sample(xc.reshape(t, cch), proj, _chunk_lanes(smt), prm_row, prm_col,
                                           gdn_norm_g[e], s_gdn, e, gdn_out, seq)
                ns = CHUNK_ROWS // seq
                w_tiles = jnp.tile(sgu_w[e][:, :seq, :seq], (1, ns, ns))
                b_cols = _pad_cols(jnp.tile(sgu_b[e][:, :seq].T, (ns, 1)), SMALL_COLS)
                o_b, vb = _sgu(proj, w_tiles, b_cols, sgu_norm_g[e], CHUNK_ROWS, ns, 1)
                new_v.append(vb.reshape(bsz, seq, hq))
                mix_acts, mix_idx = [o_a, o_b], [(e, 0), (e, 1)]
            new_conv.append(buf_new)
            w_out_all = w_out_even
        else:
            o = l // 2
            gb_row, gb_col = _row_col_params([gate_b_odd[o]])
            if is_prompt:
                n0 = s_n[o].reshape(bsz, 1, hq)
                m0 = _pad_cols(s_m[o], SMALL_COLS).reshape(bsz, 1, SMALL_COLS)
                hh, c_new, n_new, m_new = _odd_prompt(x2, gl[0], *w_in_odd, o, gb_row, gb_col,
                                                      mlstm_norm_g[o], s_c[o], n0, m0, bsz, seq, tb)
                m_new = m_new.reshape(bsz, SMALL_COLS)
                new_c.append(c_new)
            else:
                proj, smt = _proj_in(x2, gl[0], *w_in_odd, o, 512)
                n0 = s_n[o].reshape(bsz, hq)
                m0 = _pad_cols(s_m[o], SMALL_COLS)
                hh, c_out, n_new, m_new = _mlstm_sample(proj, _chunk_lanes(smt), gb_row, gb_col,
                                                        mlstm_norm_g[o], s_c, o, c_out, n0, m0, seq)
            new_n.append(n_new.reshape(bsz, N_HEADS, HEAD_DIM))
            new_m.append(m_new[:, :N_HEADS])
            mix_acts, mix_idx, w_out_all = [hh], [(o, 0)], w_out_odd
        x2 = _mix_ffn(x2, gl[1:4], mix_acts, w_out_all, mix_idx, w_ff1, w_ff2, l, tm, 1024)
    return (x2.reshape(bsz, seq, d),
            jnp.stack(new_gdn) if is_prompt else gdn_out, jnp.stack(new_conv),
            None if is_prompt else jnp.stack(new_v),
            jnp.stack(new_c) if is_prompt else c_out, jnp.stack(new_n), jnp.stack(new_m))


def kernel(x_prompt, x_sample, state_gdn, state_gdn_conv, state_mlstm_c, state_mlstm_n, state_mlstm_m,
           norm_g, w_in_even, conv_w, a_log, dt_bias, gdn_norm_g, sgu_norm_g, sgu_w, sgu_b, w_out_even,
           w_in_odd, gate_b_odd, mlstm_norm_g, w_out_odd, w_ff1, w_ff2):
    hq = N_HEADS * HEAD_DIM
    weights = (norm_g, _split_in_weight(w_in_even, 4 * hq, 2 * N_HEADS), conv_w, a_log, dt_bias,
               gdn_norm_g, sgu_norm_g, sgu_w, sgu_b, w_out_even.astype(BF16),
               _split_in_weight(w_in_odd, 2 * hq + N_HEADS * MLSTM_DV, 2 * N_HEADS), gate_b_odd,
               mlstm_norm_g, w_out_odd.astype(BF16), w_ff1.astype(BF16), w_ff2.astype(BF16))
    bp = x_prompt.shape[0]
    n_even, n_odd = state_gdn.shape[0], state_mlstm_c.shape[0]
    y_prompt, p_gdn, p_conv, _, p_c, p_n, p_m = _trunk(
        x_prompt, True,
        jnp.zeros((n_even, bp) + state_gdn.shape[2:], F32),
        jnp.zeros((n_even, bp) + state_gdn_conv.shape[2:], x_prompt.dtype),
        jnp.zeros((n_odd, bp) + state_mlstm_c.shape[2:], F32),
        jnp.zeros((n_odd, bp) + state_mlstm_n.shape[2:], F32),
        jnp.zeros((n_odd, bp) + state_mlstm_m.shape[2:], F32),
        *weights)
    y_sample, s_gdn, s_conv, s_v, s_c, s_n, s_m = _trunk(
        x_sample, False, state_gdn, state_gdn_conv, state_mlstm_c, state_mlstm_n, state_mlstm_m,
        *weights)
    return (y_prompt, y_sample, p_gdn, s_gdn, p_conv, s_conv, s_v, p_c, s_c, p_n, s_n, p_m, s_m)
```

```python
import functools
import math
from types import SimpleNamespace

import jax
import jax.numpy as jnp
from jax import lax
from jax.experimental import pallas as pl
from jax.experimental.pallas import tpu as pltpu

F32 = jnp.float32
BF16 = jnp.bfloat16

EPS = 1e-6
N_HEADS = 4
HEAD_DIM = 128
MLSTM_DV = 256
CONV_W = 4
CHUNK_ROWS = 64
SGU_CHUNK = 128
SMALL_COLS = 128
SMALL_ROWS = 16
VMEM_LIMIT = 56 * 1024 * 1024


def _cparams(sem):
    return pltpu.CompilerParams(dimension_semantics=sem, vmem_limit_bytes=VMEM_LIMIT)


def _mm(a, b):
    return jnp.dot(a.astype(BF16), b.astype(BF16), preferred_element_type=F32)


def _mm_nt(a, b):
    return lax.dot_general(a.astype(BF16), b.astype(BF16), (((1,), (1,)), ((), ())),
                           preferred_element_type=F32)


def _trunc_bf16(x):
    bits = lax.bitcast_convert_type(x, jnp.uint32) & jnp.uint32(0xFFFF0000)
    return lax.bitcast_convert_type(bits, F32)


def _split3(x):
    hi = _trunc_bf16(x)
    rest = x - hi
    mid = _trunc_bf16(rest)
    return hi, mid, rest - mid


def _sel_mm(sel, x):
    pieces = jnp.concatenate(_split3(x), axis=0).astype(BF16)
    return jnp.dot(jnp.concatenate([sel.astype(BF16)] * 3, axis=1), pieces, preferred_element_type=F32)


def _mm_sel(x, sel):
    pieces = jnp.concatenate(_split3(x), axis=1).astype(BF16)
    return jnp.dot(pieces, jnp.concatenate([sel.astype(BF16)] * 3, axis=0), preferred_element_type=F32)


def _rms(x, g):
    return x * lax.rsqrt(jnp.mean(x * x, axis=-1, keepdims=True) + EPS) * g


def _l2norm(x):
    return x * lax.rsqrt(jnp.sum(x * x, axis=-1, keepdims=True) + EPS)


def _softplus(x):
    return jnp.maximum(x, 0.0) + jnp.log1p(jnp.exp(-jnp.abs(x)))


def _sigmoid(x):
    return 0.5 * jnp.tanh(0.5 * x) + 0.5


def _silu(x):
    return x * _sigmoid(x)


def _gelu(x):
    return 0.5 * x * (1.0 + lax.erf(x * (2.0 ** -0.5)))


def _proj_in_kernel(x_ref, g_ref, w_ref, wst_ref, o_ref, ot_ref):
    _project_block(x_ref, g_ref, w_ref, wst_ref, o_ref, ot_ref)


def _proj_in(x2d, g, w_main_all, w_small_t_all, layer, tm):
    t, d = x2d.shape
    n = w_main_all.shape[2]
    return pl.pallas_call(
        _proj_in_kernel,
        grid=(t // tm,),
        in_specs=[pl.BlockSpec((tm, d), lambda i: (i, 0)),
                  pl.BlockSpec((1, d), lambda i: (0, 0)),
                  pl.BlockSpec((1, d, n), lambda i: (layer, 0, 0)),
                  pl.BlockSpec((1, SMALL_ROWS, d), lambda i: (layer, 0, 0))],
        out_specs=[pl.BlockSpec((tm, n), lambda i: (i, 0)),
                   pl.BlockSpec((SMALL_ROWS, tm), lambda i: (0, i))],
        out_shape=[jax.ShapeDtypeStruct((t, n), F32),
                   jax.ShapeDtypeStruct((SMALL_ROWS, t), F32)],
        compiler_params=_cparams(("parallel",)),
        name="proj_in",
    )(x2d, g.reshape(1, d), w_main_all, w_small_t_all)


def _mix_ffn_kernel(*refs, n_act, nk):
    x_ref, g_ref = refs[0], refs[1]
    acts = refs[2:2 + n_act]
    wos = refs[2 + n_act:2 + 2 * n_act]
    w1_ref, w2_ref, o_ref, h_ref, acc_ref = refs[2 + 2 * n_act:]
    k = pl.program_id(1)
    tm = x_ref.shape[0]
    halves = [slice(0, tm // 2), slice(tm // 2, tm)]

    def prologue(rs):
        mix = None
        for a_ref, w_ref in zip(acts, wos):
            p = jnp.dot(a_ref[rs, :].astype(BF16), w_ref[0], preferred_element_type=F32)
            mix = p if mix is None else mix + p
        x1 = x_ref[rs, :] + _rms(mix, g_ref[0:1, :])
        o_ref[rs, :] = x1
        h_ref[rs, :] = _rms(x1, g_ref[1:2, :]).astype(BF16)

    def chunk(rs, first):
        a = jnp.dot(h_ref[rs, :], w1_ref[0], preferred_element_type=F32)
        a = jnp.square(jnp.maximum(a, 0.0)).astype(BF16)
        part = jnp.dot(a, w2_ref[0], preferred_element_type=F32)
        acc_ref[rs, :] = part if first else acc_ref[rs, :] + part

    def epilogue(rs):
        o_ref[rs, :] = o_ref[rs, :] + _rms(acc_ref[rs, :], g_ref[2:3, :])

    @pl.when(k == 0)
    def _():
        for q in range(4):
            prologue(slice(q * tm // 4, (q + 1) * tm // 4))
        chunk(slice(0, tm), True)
        if nk == 1:
            for rs in halves:
                epilogue(rs)

    @pl.when((k > 0) & (k < nk - 1))
    def _():
        chunk(slice(0, tm), False)

    @pl.when((k > 0) & (k == nk - 1))
    def _():
        for rs in halves:
            chunk(rs, False)
            epilogue(rs)


def _mix_ffn(x2d, g3rows, acts, w_out_all, w_out_idx, w1_all, w2_all, layer, tm, tf):
    t, d = x2d.shape
    ff = w1_all.shape[2]
    n_act = len(acts)
    in_specs = [pl.BlockSpec((tm, d), lambda i, k: (i, 0)), pl.BlockSpec((3, d), lambda i, k: (0, 0))]
    in_specs += [pl.BlockSpec((tm, a.shape[1]), lambda i, k: (i, 0)) for a in acts]
    in_specs += [pl.BlockSpec((1, a.shape[1], d), lambda i, k, li=li, ri=ri: (li, ri, 0))
                 for a, (li, ri) in zip(acts, w_out_idx)]
    in_specs += [pl.BlockSpec((1, d, tf), lambda i, k: (layer, 0, k)),
                 pl.BlockSpec((1, tf, d), lambda i, k: (layer, k, 0))]
    return pl.pallas_call(
        functools.partial(_mix_ffn_kernel, n_act=n_act, nk=ff // tf),
        grid=(t // tm, ff // tf),
        in_specs=in_specs,
        out_specs=pl.BlockSpec((tm, d), lambda i, k: (i, 0)),
        out_shape=jax.ShapeDtypeStruct((t, d), F32),
        scratch_shapes=[pltpu.VMEM((tm, d), BF16), pltpu.VMEM((tm, d), F32)],
        compiler_params=_cparams(("parallel", "arbitrary")),
        name="mix_ffn",
    )(x2d, g3rows, *acts, *([w_out_all] * n_act), w1_all, w2_all)


def _ind(mask):
    return jnp.where(mask, 1.0, 0.0).astype(F32)


def _div_pow2(x, d):
    return jnp.right_shift(x, int(math.log2(d)))


def _chunk_structs(rows, ns):
    lt = rows // ns
    ri = lax.broadcasted_iota(jnp.int32, (rows, rows), 0)
    ci = lax.broadcasted_iota(jnp.int32, (rows, rows), 1)
    cs = SimpleNamespace(rows=rows, ns=ns, lt=lt)
    if ns == 1:
        cs.same = None
        cs.causal = ri >= ci
        cs.strict = ri > ci
        cs.upper = ri <= ci
    else:
        rseq = _div_pow2(ri, lt)
        cseq = _div_pow2(ci, lt)
        cs.same = rseq == cseq
        cs.causal = cs.same & (ri >= ci)
        cs.strict = cs.same & (ri > ci)
        cs.upper = cs.same & (ri <= ci)
        cs.last = _ind(ci == rseq * lt + (lt - 1))
        cs.last_t = _ind(ri == cseq * lt + (lt - 1))
        sr = _div_pow2(lax.broadcasted_iota(jnp.int32, (ns * HEAD_DIM, rows), 0), HEAD_DIM)
        sc = lax.broadcasted_iota(jnp.int32, (ns * HEAD_DIM, rows), 1)
        cs.stack_last = _ind(sc == sr * lt + (lt - 1))
        cs.stack_mask = _ind(_div_pow2(sc, lt) == sr)
        er = _div_pow2(lax.broadcasted_iota(jnp.int32, (rows, ns * HEAD_DIM), 0), lt)
        ec = _div_pow2(lax.broadcasted_iota(jnp.int32, (rows, ns * HEAD_DIM), 1), HEAD_DIM)
        cs.expand_mask = _ind(er == ec)
        kr = lax.broadcasted_iota(jnp.int32, (ns, rows), 0)
        kc = lax.broadcasted_iota(jnp.int32, (ns, rows), 1)
        cs.seq_last = _ind(kc == kr * lt + (lt - 1))
        cs.seq_sum = _ind(_div_pow2(kc, lt) == kr)
        pr = _div_pow2(lax.broadcasted_iota(jnp.int32, (rows, HEAD_DIM), 0), lt)
        pc = lax.broadcasted_iota(jnp.int32, (rows, HEAD_DIM), 1)
        cs.seq_expand = _ind(pr == pc)
    cs.lower_f = _ind(cs.causal)
    cs.upper_f = _ind(cs.upper)
    eye_r = lax.broadcasted_iota(jnp.int32, (HEAD_DIM, HEAD_DIM), 0)
    eye_c = lax.broadcasted_iota(jnp.int32, (HEAD_DIM, HEAD_DIM), 1)
    cs.eye = _ind(eye_r == eye_c)
    cs.eye_rows = _ind(ri == ci)
    return cs


def _last_rows(cs, x):
    if cs.ns == 1:
        return jnp.broadcast_to(x[cs.rows - 1:cs.rows, :], x.shape)
    return _sel_mm(cs.last, x)


def _last_lanes(cs, x):
    if cs.ns == 1:
        return jnp.broadcast_to(x[:, cs.rows - 1:cs.rows], x.shape)
    return _mm_sel(x, cs.last_t)


def _expand_lhs(cs, x):
    if cs.ns == 1:
        return x
    reps = x.shape[-2] // cs.rows
    mask = cs.expand_mask if reps == 1 else jnp.concatenate([cs.expand_mask] * reps, axis=0)
    return jnp.concatenate([x] * cs.ns, axis=-1) * mask


def _stack_t(cs, x):
    xt = _mm_nt(cs.eye, x)
    if cs.ns == 1:
        return xt
    return jnp.concatenate([xt] * cs.ns, axis=0) * cs.stack_mask


def _stack_scalar(cs, col):
    if cs.ns == 1:
        return col[cs.rows - 1:cs.rows, :]
    return _sel_mm(cs.stack_last, jnp.broadcast_to(col, (cs.rows, HEAD_DIM)))[:, 0:1]


_BATCH_NN = (((2,), (1,)), ((0,), (0,)))
_BATCH_NT = (((2,), (2,)), ((0,), (0,)))
_BATCH_TN = (((1,), (1,)), ((0,), (0,)))


def _bmm(a, b, dims=_BATCH_NN):
    return lax.dot_general(a.astype(BF16), b.astype(BF16), dims, preferred_element_type=F32)


def _split_lhs(a):
    hi = a.astype(BF16).astype(F32)
    return jnp.concatenate([hi, a - hi, hi], axis=-1).astype(BF16)


def _split_rhs(b):
    hi = b.astype(BF16).astype(F32)
    return jnp.concatenate([hi, hi, b - hi], axis=-2).astype(BF16)


def _bmm_split(a_cat, b_cat):
    return lax.dot_general(a_cat, b_cat, _BATCH_NN, preferred_element_type=F32)


def _unit_lower_inverse_off(cs, a):
    rows = cs.rows
    n = -a
    steps = int(math.log2(cs.lt)) - 1
    if steps == 0:
        return n
    pt = jnp.concatenate([_bmm_split(_split_lhs(n), _split_rhs(n)), n], axis=-1)
    right = lax.broadcasted_iota(jnp.int32, (rows, 2 * rows), 1) >= rows
    for _ in range(steps):
        hi = pt.astype(BF16).astype(F32)
        lo = pt - hi
        p_cat = jnp.concatenate([hi[..., :rows], lo[..., :rows], hi[..., :rows]], axis=-1).astype(BF16)
        pt_cat = jnp.concatenate([hi, hi, lo], axis=-2).astype(BF16)
        p_both = jnp.concatenate([pt[..., :rows], pt[..., :rows]], axis=-1)
        pt = _bmm_split(p_cat, pt_cat) + jnp.where(right, pt + p_both, 0.0)
    return pt[..., rows:]


def _gdn_block(cs, nchunks, xc, gate, sm, smt, prm_row, prm_col, gng, s_heads, write_out):
    rows = cs.rows
    hq = N_HEADS * HEAD_DIM
    pairs = [(c, h) for c in range(nchunks) for h in range(N_HEADS)]
    rs = lambda c: slice(c * rows, (c + 1) * rows)

    def heads(x, col0):
        return jnp.stack([x[rs(c), col0 + h * HEAD_DIM:col0 + (h + 1) * HEAD_DIM] for c, h in pairs])

    def cols(x, lane0):
        return jnp.stack([x[rs(c), lane0 + h:lane0 + h + 1] for c, h in pairs])

    beta_all = _sigmoid(sm)
    g_c = -prm_row[0:1, :] * _softplus(sm + prm_row[1:2, :])
    g_r = -prm_col[:, 0:1] * _softplus(smt + prm_col[:, 1:2])
    gc_c = jnp.concatenate([_sel_mm(cs.lower_f, g_c[rs(c)]) for c in range(nchunks)], axis=0)
    gc_r = jnp.concatenate([_mm_sel(g_r[:, rs(c)], cs.upper_f) for c in range(nchunks)], axis=1)
    gl_c = jnp.concatenate([_last_rows(cs, gc_c[rs(c)]) for c in range(nchunks)], axis=0)
    egc = jnp.exp(gc_c)
    ekd = jnp.exp(gl_c - gc_c)

    q = _l2norm(heads(xc, 0)) * (HEAD_DIM ** -0.5)
    k = _l2norm(heads(xc, hq))
    v = heads(xc, 2 * hq)
    beta = cols(beta_all, 0)
    gcol = cols(gc_c, N_HEADS)
    eg = cols(egc, N_HEADS)
    grow = jnp.stack([gc_r[N_HEADS + h:N_HEADS + h + 1, rs(c)] for c, h in pairs])
    decay = jnp.where(cs.causal, jnp.exp(jnp.where(cs.causal, gcol - grow, 0.0)), 0.0)
    kb = k * beta
    a = jnp.where(cs.strict, _bmm(kb, k, _BATCH_NT) * decay, 0.0)
    toff = _unit_lower_inverse_off(cs, a)
    rhs = jnp.concatenate([v * beta, kb * eg], axis=-1)
    sol = rhs + _bmm_split(_split_lhs(toff), _split_rhs(rhs))
    u_val, w_k = sol[..., :HEAD_DIM], sol[..., HEAD_DIM:]
    qk = jnp.where(cs.causal, _bmm(q, k, _BATCH_NT) * decay, 0.0)
    q_dec = q * eg
    k_dec = k * cols(ekd, N_HEADS)

    s = s_heads
    for c in range(nchunks):
        hs = slice(c * N_HEADS, (c + 1) * N_HEADS)
        p = _bmm(_expand_lhs(cs, jnp.concatenate([w_k[hs], q_dec[hs]], axis=1)), s)
        v_new = u_val[hs] - p[:, :rows]
        o = p[:, rows:] + _bmm(qk[hs], v_new)
        gl_cols = [gl_c[rs(c), N_HEADS + h:N_HEADS + h + 1] for h in range(N_HEADS)]
        g_last = jnp.exp(jnp.stack([_stack_scalar(cs, col) for col in gl_cols]))
        if cs.ns == 1:
            upd = _bmm(k_dec[hs], v_new, _BATCH_TN)
        else:
            kt = _bmm(jnp.broadcast_to(cs.eye, (N_HEADS, HEAD_DIM, HEAD_DIM)), k_dec[hs], _BATCH_NT)
            upd = _bmm(jnp.concatenate([kt] * cs.ns, axis=1) * cs.stack_mask, v_new)
        s = s * g_last + upd
        for h in range(N_HEADS):
            write_out(c, h, _rms(o[h], gng) * _silu(gate[rs(c), h * HEAD_DIM:(h + 1) * HEAD_DIM]))
    return s


def _project_block(x_ref, g_ref, w_ref, wst_ref, proj_w, smt_w):
    hn = _rms(x_ref[...], g_ref[...]).astype(BF16)
    proj_w[...] = jnp.dot(hn, w_ref[0], preferred_element_type=F32)
    smt_w[...] = lax.dot_general(wst_ref[0], hn, (((1,), (1,)), ((), ())), preferred_element_type=F32)


def _pipelined_steps(step):
    i = pl.program_id(0)
    for parity in range(2):
        @pl.when(lax.rem(i, 2) == parity)
        def _():
            step(parity)


def _even_prompt_kernel(x_ref, g_ref, w_ref, wst_ref, cw_ref, prow_ref, pcol_ref, gng_ref, s0_ref, buf_ref,
                        sw_ref, sb_ref, sg_ref, o_ref, sout_ref, bufout_ref,
                        proj_a, proj_b, smt_a, smt_b, s_scr, xp_scr, *, tb, nt):
    i = pl.program_id(0)
    t = lax.rem(jnp.maximum(i - 1, 0), nt)
    hq = N_HEADS * HEAD_DIM
    cch = 3 * hq
    cs = _chunk_structs(CHUNK_ROWS, 1)

    @pl.when(i == 0)
    def _():
        proj_b[...] = jnp.zeros_like(proj_b)
        smt_b[...] = jnp.zeros_like(smt_b)

    @pl.when(t == 0)
    def _():
        s_scr[...] = s0_ref[0]
        xp_scr[...] = buf_ref[0]

    def write_delta(c, h, o):
        o_ref[c * CHUNK_ROWS:(c + 1) * CHUNK_ROWS, h * HEAD_DIM:(h + 1) * HEAD_DIM] = o

    def write_gating(rs, g, val):
        o_ref[rs, hq + g * HEAD_DIM:hq + (g + 1) * HEAD_DIM] = val

    def step(parity):
        proj_w, smt_w, proj_r, smt_r = ((proj_a, smt_a, proj_b, smt_b) if parity == 0
                                        else (proj_b, smt_b, proj_a, smt_a))
        _project_block(x_ref, g_ref, w_ref, wst_ref, proj_w, smt_w)
        x = proj_r[:, 0:cch]
        xp = jnp.concatenate([xp_scr[...], x], axis=0)
        y = cw_ref[CONV_W - 1:CONV_W, :] * x
        for back in range(1, CONV_W):
            y = y + cw_ref[CONV_W - 1 - back:CONV_W - back, :] * pltpu.roll(xp, back, axis=0)[8:]
        xp_scr[...] = x[tb - 8:]
        s_scr[...] = _gdn_block(cs, tb // CHUNK_ROWS, _silu(y), proj_r[:, cch:cch + hq],
                                proj_r[:, cch + 3 * hq:cch + 3 * hq + SMALL_COLS], smt_r[...],
                                prow_ref[...], pcol_ref[...], gng_ref[...], s_scr[...], write_delta)
        _sgu_apply(proj_r[:, cch + hq:cch + 2 * hq], proj_r[:, cch + 2 * hq:cch + 3 * hq],
                   sw_ref, sb_ref[...], sg_ref[...], SGU_CHUNK, 1, write_gating, None)

    _pipelined_steps(step)

    @pl.when(t == nt - 1)
    def _():
        sout_ref[0] = s_scr[...]
        bufout_ref[0] = xp_scr[...]


def _even_prompt(x2, g, w_main_all, w_small_t_all, e, conv_w, prm_row, prm_col, gng, s0, buf8,
                 sgu_w, sgu_b_cols, sgu_g, bsz, seq, tb):
    t, d = x2.shape
    hq = N_HEADS * HEAD_DIM
    cch = 3 * hq
    n_proj = w_main_all.shape[2]
    n = t // tb
    nt = seq // tb
    blk = lambda i: jnp.minimum(i, n - 1)
    seq_of = lambda i: jnp.maximum(i - 1, 0) // nt
    const2 = lambda i: (0, 0)
    return pl.pallas_call(
        functools.partial(_even_prompt_kernel, tb=tb, nt=nt),
        grid=(n + 1,),
        in_specs=[pl.BlockSpec((tb, d), lambda i: (blk(i), 0)),
                  pl.BlockSpec((1, d), const2),
                  pl.BlockSpec((1, d, n_proj), lambda i: (e, 0, 0)),
                  pl.BlockSpec((1, SMALL_ROWS, d), lambda i: (e, 0, 0)),
                  pl.BlockSpec((CONV_W, cch), const2),
                  pl.BlockSpec((2, SMALL_COLS), const2),
                  pl.BlockSpec((SMALL_ROWS, 2), const2),
                  pl.BlockSpec((1, HEAD_DIM), const2),
                  pl.BlockSpec((1, N_HEADS, HEAD_DIM, HEAD_DIM), lambda i: (seq_of(i), 0, 0, 0)),
                  pl.BlockSpec((1, 8, cch), lambda i: (seq_of(i), 0, 0)),
                  pl.BlockSpec((N_HEADS, SGU_CHUNK, SGU_CHUNK), lambda i: (0, 0, 0)),
                  pl.BlockSpec((SGU_CHUNK, SMALL_COLS), const2),
                  pl.BlockSpec((N_HEADS, HEAD_DIM), const2)],
        out_specs=[pl.BlockSpec((tb, 2 * hq), lambda i: (jnp.maximum(i - 1, 0), 0)),
                   pl.BlockSpec((1, N_HEADS, HEAD_DIM, HEAD_DIM), lambda i: (seq_of(i), 0, 0, 0)),
                   pl.BlockSpec((1, 8, cch), lambda i: (seq_of(i), 0, 0))],
        out_shape=[jax.ShapeDtypeStruct((t, 2 * hq), F32),
                   jax.ShapeDtypeStruct((bsz, N_HEADS, HEAD_DIM, HEAD_DIM), F32),
                   jax.ShapeDtypeStruct((bsz, 8, cch), F32)],
        scratch_shapes=[pltpu.VMEM((tb, n_proj), F32), pltpu.VMEM((tb, n_proj), F32),
                        pltpu.VMEM((SMALL_ROWS, tb), F32), pltpu.VMEM((SMALL_ROWS, tb), F32),
                        pltpu.VMEM((N_HEADS, HEAD_DIM, HEAD_DIM), F32),
                        pltpu.VMEM((8, cch), F32)],
        compiler_params=_cparams(("arbitrary",)),
        name="even_prompt",
    )(x2, g.reshape(1, d), w_main_all, w_small_t_all, conv_w, prm_row, prm_col,
      gng.reshape(1, HEAD_DIM), s0, buf8, sgu_w, sgu_b_cols, sgu_g)


def _conv_sample_kernel(xcat_ref, cw_ref, o_ref, *, lt, cch):
    for t in range(lt):
        y = None
        for j in range(CONV_W):
            term = cw_ref[j:j + 1, :] * xcat_ref[:, (t + j) * cch:(t + j + 1) * cch]
            y = term if y is None else y + term
        o_ref[:, t * cch:(t + 1) * cch] = _silu(y)


def _conv_sample(xcat, conv_w, lt, cch):
    nseq = xcat.shape[0]
    return pl.pallas_call(
        functools.partial(_conv_sample_kernel, lt=lt, cch=cch),
        grid=(1,),
        in_specs=[pl.BlockSpec(xcat.shape, lambda i: (0, 0)),
                  pl.BlockSpec((CONV_W, cch), lambda i: (0, 0))],
        out_specs=pl.BlockSpec((nseq, lt * cch), lambda i: (0, 0)),
        out_shape=jax.ShapeDtypeStruct((nseq, lt * cch), F32),
        compiler_params=_cparams(("arbitrary",)),
        name="conv_sample",
    )(xcat, conv_w)


def _gdn_sample_kernel(xc_ref, gate_ref, sm_ref, smt_ref, prow_ref, pcol_ref, gng_ref, s0_ref,
                       *rest, ns):
    o_ref, sout_ref = rest[-2:]

    @pl.when(pl.program_id(1) == 0)
    def _():
        cs = _chunk_structs(CHUNK_ROWS, ns)
        s_heads = jnp.stack([s0_ref[0, :, h].reshape(ns * HEAD_DIM, HEAD_DIM) for h in range(N_HEADS)])

        def write_out(c, h, o):
            o_ref[:, h * HEAD_DIM:(h + 1) * HEAD_DIM] = o

        s_new = _gdn_block(cs, 1, xc_ref[...], gate_ref[...], sm_ref[...], smt_ref[0],
                           prow_ref[...], pcol_ref[...], gng_ref[...], s_heads, write_out)
        for h in range(N_HEADS):
            sout_ref[0, :, h] = s_new[h].reshape(ns, HEAD_DIM, HEAD_DIM)

    @pl.when(pl.program_id(1) > 0)
    def _():
        sout_ref[...] = jnp.zeros_like(sout_ref)


def _gdn_sample(xc, proj, smt, prm_row, prm_col, gng, s_all, layer, s_out_prev, lt):
    hq = N_HEADS * HEAD_DIM
    cch = 3 * hq
    ns = CHUNK_ROWS // lt
    nchunk = xc.shape[0] // CHUNK_ROWS
    n_layers = s_all.shape[0]
    small_blk = (cch + 3 * hq) // SMALL_COLS
    state_shape = (1, ns, N_HEADS, HEAD_DIM, HEAD_DIM)
    in_specs = [pl.BlockSpec((CHUNK_ROWS, cch), lambda c, j: (c, 0)),
                pl.BlockSpec((CHUNK_ROWS, hq), lambda c, j: (c, cch // hq)),
                pl.BlockSpec((CHUNK_ROWS, SMALL_COLS), lambda c, j: (c, small_blk)),
                pl.BlockSpec((1, SMALL_ROWS, CHUNK_ROWS), lambda c, j: (c, 0, 0)),
                pl.BlockSpec((2, SMALL_COLS), lambda c, j: (0, 0)),
                pl.BlockSpec((SMALL_ROWS, 2), lambda c, j: (0, 0)),
                pl.BlockSpec((1, HEAD_DIM), lambda c, j: (0, 0)),
                pl.BlockSpec(state_shape, lambda c, j: (layer, c, 0, 0, 0))]
    args = [xc, proj, proj, smt, prm_row, prm_col, gng.reshape(1, HEAD_DIM), s_all]
    aliases = {}
    if s_out_prev is not None:
        aliases = {len(args): 1}
        in_specs.append(pl.BlockSpec(memory_space=pl.ANY))
        args.append(s_out_prev)
    return pl.pallas_call(
        functools.partial(_gdn_sample_kernel, ns=ns),
        grid=(nchunk, 1 if aliases else n_layers),
        in_specs=in_specs,
        out_specs=[pl.BlockSpec((CHUNK_ROWS, hq), lambda c, j: (c, 0)),
                   pl.BlockSpec(state_shape, lambda c, j: (lax.rem(layer + j, n_layers), c, 0, 0, 0))],
        out_shape=[jax.ShapeDtypeStruct((xc.shape[0], hq), F32),
                   jax.ShapeDtypeStruct(s_all.shape, F32)],
        input_output_aliases=aliases,
        compiler_params=_cparams(("parallel", "arbitrary")),
        name="gdn_sample",
    )(*args)


def _sgu_apply(u_pre, v_pre, w_ref, b_cols, g_rows, rows, ns, write_o, write_vb):
    lt = rows // ns
    ri = lax.broadcasted_iota(jnp.int32, (rows, rows), 0)
    ci = lax.broadcasted_iota(jnp.int32, (rows, rows), 1)
    keep = ri >= ci
    if ns > 1:
        keep = keep & (_div_pow2(ri, lt) == _div_pow2(ci, lt))
    for g in range(N_HEADS):
        lo = g * HEAD_DIM
        w = jnp.where(keep, w_ref[g], 0.0).astype(BF16)
        u = _gelu(u_pre[:, lo:lo + HEAD_DIM])
        vb = _rms(_gelu(v_pre[:, lo:lo + HEAD_DIM]), g_rows[g:g + 1, :])
        if write_vb is not None:
            write_vb(g, vb)
        for c in range(u_pre.shape[0] // rows):
            rs = slice(c * rows, (c + 1) * rows)
            write_o(rs, g, u[rs] * (_mm(w, vb[rs]) + b_cols[:, g:g + 1]))


def _sgu_kernel(u_ref, v_ref, w_ref, b_ref, g_ref, o_ref, vb_ref, *, rows, ns):
    def write_o(rs, g, val):
        o_ref[rs, g * HEAD_DIM:(g + 1) * HEAD_DIM] = val

    def write_vb(g, val):
        vb_ref[:, g * HEAD_DIM:(g + 1) * HEAD_DIM] = val

    _sgu_apply(u_ref[...], v_ref[...], w_ref, b_ref[...], g_ref[...], rows, ns, write_o, write_vb)


def _sgu(proj, w_tiles, b_cols, norm_g, rows, ns, nchunks):
    hq = N_HEADS * HEAD_DIM
    t = proj.shape[0]
    u_blk = (3 * hq + hq) // hq
    blk = rows * nchunks
    return pl.pallas_call(
        functools.partial(_sgu_kernel, rows=rows, ns=ns),
        grid=(t // blk,),
        in_specs=[pl.BlockSpec((blk, hq), lambda i: (i, u_blk)),
                  pl.BlockSpec((blk, hq), lambda i: (i, u_blk + 1)),
                  pl.BlockSpec((N_HEADS, rows, rows), lambda i: (0, 0, 0)),
                  pl.BlockSpec((rows, SMALL_COLS), lambda i: (0, 0)),
                  pl.BlockSpec((N_HEADS, HEAD_DIM), lambda i: (0, 0))],
        out_specs=[pl.BlockSpec((blk, hq), lambda i: (i, 0)),
                   pl.BlockSpec((blk, hq), lambda i: (i, 0))],
        out_shape=[jax.ShapeDtypeStruct((t, hq), F32), jax.ShapeDtypeStruct((t, hq), F32)],
        compiler_params=_cparams(("parallel",)),
        name="sgu",
    )(proj, proj, w_tiles, b_cols, norm_g)


def _logsigmoid(x):
    return jnp.minimum(x, 0.0) - jnp.log1p(jnp.exp(-jnp.abs(x)))


def _mlstm_chunk(cs, q_all, k_all, v_all, opre, sm, smt, gb_row, gb_col, mng, c_list, n_rows, m_rows):
    neg_inf = -jnp.inf
    pre_c = sm + gb_row
    pre_r = smt + gb_col
    b_c = _sel_mm(cs.lower_f, _logsigmoid(pre_c))
    b_r = _mm_sel(_logsigmoid(pre_r), cs.upper_f)
    bl_c = _last_rows(cs, b_c)
    bl_r = _last_lanes(cs, b_r)
    res = []
    for h in range(N_HEADS):
        lo = h * HEAD_DIM
        vo = h * MLSTM_DV
        q = q_all[:, lo:lo + HEAD_DIM] * (HEAD_DIM ** -0.5)
        k = k_all[:, lo:lo + HEAD_DIM]
        v = v_all[:, vo:vo + MLSTM_DV]
        bcol = b_c[:, N_HEADS + h:N_HEADS + h + 1]
        brow = b_r[N_HEADS + h:N_HEADS + h + 1, :]
        icol = pre_c[:, h:h + 1]
        irow = pre_r[h:h + 1, :]
        blcol = bl_c[:, N_HEADS + h:N_HEADS + h + 1]
        blrow = bl_r[N_HEADS + h:N_HEADS + h + 1, :]
        mrow = m_rows[:, h:h + 1]
        inter = bcol + mrow
        dmat = jnp.where(cs.causal, bcol - brow + irow, neg_inf)
        m_t = jnp.maximum(inter, jnp.max(dmat, axis=-1, keepdims=True))
        w_intra = jnp.exp(dmat - m_t)
        w_inter = jnp.exp(inter - m_t)
        s = _mm_nt(q, k) * w_intra
        c_old = c_list[h]
        num = w_inter * _mm(_expand_lhs(cs, q), c_old) + _mm(s, v)
        qn = jnp.sum(q * n_rows[:, lo:lo + HEAD_DIM], axis=-1, keepdims=True)
        den = w_inter * qn + jnp.sum(s, axis=-1, keepdims=True)
        hval = num / jnp.maximum(jnp.abs(den), jnp.exp(-m_t))
        out = _rms(hval, mng) * _sigmoid(opre[:, vo:vo + MLSTM_DV])
        logw_row = blrow - brow + irow
        if cs.ns == 1:
            seq_max = jnp.max(logw_row, axis=-1, keepdims=True)
        else:
            seq_max = jnp.max(jnp.where(cs.same, logw_row, neg_inf), axis=-1, keepdims=True)
        m_new = jnp.maximum(blcol + mrow, seq_max)
        keep = jnp.exp(blcol + mrow - m_new)
        wk = jnp.exp(blcol - bcol + icol - m_new)
        kw = k * wk
        c_new = c_old * _stack_scalar(cs, keep) + _mm(_stack_t(cs, kw), v)
        res.append((out, c_new, keep, m_new, kw))
    return res


def _mlstm_prompt_block(cs, nchunks, q_all, k_all, v_all, opre, sm, smt, gb_row, gb_col, mng,
                        c_heads, n_heads, m_heads, write_out):
    rows = cs.rows
    pairs = [(c, h) for c in range(nchunks) for h in range(N_HEADS)]
    rs = lambda c: slice(c * rows, (c + 1) * rows)

    def heads(x, width):
        return jnp.stack([x[rs(c), h * width:(h + 1) * width] for c, h in pairs])

    def cols(x, lane0):
        return jnp.stack([x[rs(c), lane0 + h:lane0 + h + 1] for c, h in pairs])

    def lanes(x, row0):
        return jnp.stack([x[row0 + h:row0 + h + 1, rs(c)] for c, h in pairs])

    pre_c = sm + gb_row
    pre_r = smt + gb_col
    lf_c = _logsigmoid(pre_c)
    lf_r = _logsigmoid(pre_r)
    b_c = jnp.concatenate([_sel_mm(cs.lower_f, lf_c[rs(c)]) for c in range(nchunks)], axis=0)
    b_r = jnp.concatenate([_mm_sel(lf_r[:, rs(c)], cs.upper_f) for c in range(nchunks)], axis=1)

    q = heads(q_all, HEAD_DIM) * (HEAD_DIM ** -0.5)
    k = heads(k_all, HEAD_DIM)
    v = heads(v_all, MLSTM_DV)
    bcol = cols(b_c, N_HEADS)
    icol = cols(pre_c, 0)
    brow = lanes(b_r, N_HEADS)
    irow = lanes(pre_r, 0)
    blast = bcol[:, rows - 1:rows, :]
    dmat = jnp.where(cs.causal, bcol - brow + irow, -jnp.inf)
    rmax = jnp.max(dmat, axis=-1, keepdims=True)
    sp = _bmm(q, k, _BATCH_NT) * jnp.exp(dmat - rmax)
    sv = _bmm(sp, v)
    ssum = jnp.sum(sp, axis=-1, keepdims=True)
    lmax = jnp.max(blast - brow + irow, axis=-1, keepdims=True)
    kwp = k * jnp.exp(blast - bcol + icol - lmax)
    upd = _bmm(kwp, v, _BATCH_TN)
    nsum = jnp.sum(kwp, axis=1, keepdims=True)

    cst, n, m = c_heads, n_heads, m_heads
    for c in range(nchunks):
        hs = slice(c * N_HEADS, (c + 1) * N_HEADS)
        inter = bcol[hs] + m
        m_t = jnp.maximum(inter, rmax[hs])
        w_inter = jnp.exp(inter - m_t)
        w_intra = jnp.exp(rmax[hs] - m_t)
        num = w_inter * _bmm(q[hs], cst) + w_intra * sv[hs]
        den = (w_inter * jnp.sum(q[hs] * n, axis=-1, keepdims=True) + w_intra * ssum[hs])
        hval = num / jnp.maximum(jnp.abs(den), jnp.exp(-m_t))
        for h in range(N_HEADS):
            write_out(c, h, _rms(hval[h], mng) * _sigmoid(opre[rs(c), h * MLSTM_DV:(h + 1) * MLSTM_DV]))
        m_new = jnp.maximum(blast[hs] + m, lmax[hs])
        keep = jnp.exp(blast[hs] + m - m_new)
        fresh = jnp.exp(lmax[hs] - m_new)
        cst = keep * cst + fresh * upd[hs]
        n = keep * n + fresh * nsum[hs]
        m = m_new
    return cst, n, m


def _odd_prompt_kernel(x_ref, g_ref, w_ref, wst_ref, gbr_ref, gbc_ref, mng_ref, c0_ref, n0_ref, m0_ref,
                       o_ref, cout_ref, nout_ref, mout_ref,
                       proj_a, proj_b, smt_a, smt_b, c_scr, n_scr, m_scr, *, tb, nt):
    i = pl.program_id(0)
    t = lax.rem(jnp.maximum(i - 1, 0), nt)
    hq = N_HEADS * HEAD_DIM
    hv = N_HEADS * MLSTM_DV
    cs = _chunk_structs(CHUNK_ROWS, 1)
    lane = lax.broadcasted_iota(jnp.int32, (1, SMALL_COLS), 1)

    @pl.when(i == 0)
    def _():
        proj_b[...] = jnp.zeros_like(proj_b)
        smt_b[...] = jnp.zeros_like(smt_b)

    @pl.when(t == 0)
    def _():
        c_scr[...] = c0_ref[0]
        n_scr[...] = n0_ref[0]
        m_scr[...] = m0_ref[0]

    def write_out(c, h, o):
        o_ref[c * CHUNK_ROWS:(c + 1) * CHUNK_ROWS, h * MLSTM_DV:(h + 1) * MLSTM_DV] = o

    def step(parity):
        proj_w, smt_w, proj_r, smt_r = ((proj_a, smt_a, proj_b, smt_b) if parity == 0
                                        else (proj_b, smt_b, proj_a, smt_a))
        _project_block(x_ref, g_ref, w_ref, wst_ref, proj_w, smt_w)
        n_cur = n_scr[...]
        m_cur = m_scr[...]
        n_heads = jnp.stack([n_cur[:, h * HEAD_DIM:(h + 1) * HEAD_DIM] for h in range(N_HEADS)])
        m_heads = jnp.stack([m_cur[:, h:h + 1] for h in range(N_HEADS)])
        c_new, n_heads, m_heads = _mlstm_prompt_block(
            cs, tb // CHUNK_ROWS, proj_r[:, 0:hq], proj_r[:, hq:2 * hq], proj_r[:, 2 * hq:2 * hq + hv],
            proj_r[:, 2 * hq + hv:2 * hq + 2 * hv],
            proj_r[:, 2 * hq + 2 * hv:2 * hq + 2 * hv + SMALL_COLS], smt_r[...],
            gbr_ref[...], gbc_ref[...], mng_ref[...], c_scr[...], n_heads, m_heads, write_out)
        c_scr[...] = c_new
        n_scr[...] = jnp.concatenate([n_heads[h] for h in range(N_HEADS)], axis=1)
        for h in range(N_HEADS):
            m_cur = jnp.where(lane == h, m_heads[h], m_cur)
        m_scr[...] = m_cur

    _pipelined_steps(step)

    @pl.when(t == nt - 1)
    def _():
        cout_ref[0] = c_scr[...]
        nout_ref[0] = n_scr[...]
        mout_ref[0] = m_scr[...]


def _odd_prompt(x2, g, w_main_all, w_small_t_all, o, gb_row, gb_col, mng, c0, n0, m0, bsz, seq, tb):
    t, d = x2.shape
    hq = N_HEADS * HEAD_DIM
    hv = N_HEADS * MLSTM_DV
    n_proj = w_main_all.shape[2]
    n = t // tb
    nt = seq // tb
    blk = lambda i: jnp.minimum(i, n - 1)
    seq_of = lambda i: jnp.maximum(i - 1, 0) // nt
    const2 = lambda i: (0, 0)
    state_c = pl.BlockSpec((1, N_HEADS, HEAD_DIM, MLSTM_DV), lambda i: (seq_of(i), 0, 0, 0))
    state_n = pl.BlockSpec((1, 1, hq), lambda i: (seq_of(i), 0, 0))
    state_m = pl.BlockSpec((1, 1, SMALL_COLS), lambda i: (seq_of(i), 0, 0))
    return pl.pallas_call(
        functools.partial(_odd_prompt_kernel, tb=tb, nt=nt),
        grid=(n + 1,),
        in_specs=[pl.BlockSpec((tb, d), lambda i: (blk(i), 0)),
                  pl.BlockSpec((1, d), const2),
                  pl.BlockSpec((1, d, n_proj), lambda i: (o, 0, 0)),
                  pl.BlockSpec((1, SMALL_ROWS, d), lambda i: (o, 0, 0)),
                  pl.BlockSpec((1, SMALL_COLS), const2),
                  pl.BlockSpec((SMALL_ROWS, 1), const2),
                  pl.BlockSpec((1, MLSTM_DV), const2),
                  state_c, state_n, state_m],
        out_specs=[pl.BlockSpec((tb, hv), lambda i: (jnp.maximum(i - 1, 0), 0)),
                   state_c, state_n, state_m],
        out_shape=[jax.ShapeDtypeStruct((t, hv), F32),
                   jax.ShapeDtypeStruct(c0.shape, F32),
                   jax.ShapeDtypeStruct(n0.shape, F32),
                   jax.ShapeDtypeStruct(m0.shape, F32)],
        scratch_shapes=[pltpu.VMEM((tb, n_proj), F32), pltpu.VMEM((tb, n_proj), F32),
                        pltpu.VMEM((SMALL_ROWS, tb), F32), pltpu.VMEM((SMALL_ROWS, tb), F32),
                        pltpu.VMEM((N_HEADS, HEAD_DIM, MLSTM_DV), F32),
                        pltpu.VMEM((1, hq), F32),
                        pltpu.VMEM((1, SMALL_COLS), F32)],
        compiler_params=_cparams(("arbitrary",)),
        name="odd_prompt",
    )(x2, g.reshape(1, d), w_main_all, w_small_t_all, gb_row, gb_col, mng.reshape(1, MLSTM_DV), c0, n0, m0)


def _mlstm_sample_kernel(q_ref, k_ref, v_ref, op_ref, sm_ref, smt_ref, gbr_ref, gbc_ref, mng_ref,
                         c0_ref, n0_ref, m0_ref, *rest, ns):
    o_ref, cout_ref, nout_ref, mout_ref = rest[-4:]

    @pl.when(pl.program_id(1) == 0)
    def _():
        cs = _chunk_structs(CHUNK_ROWS, ns)
        lane = lax.broadcasted_iota(jnp.int32, (1, SMALL_COLS), 1)
        c_list = [c0_ref[0, :, h].reshape(ns * HEAD_DIM, MLSTM_DV) for h in range(N_HEADS)]
        n0 = n0_ref[...]
        m0 = m0_ref[...]
        n_rows = _sel_mm(cs.seq_expand, jnp.concatenate(
            [n0, jnp.zeros((HEAD_DIM - ns, n0.shape[1]), F32)], axis=0))
        m_rows = _sel_mm(cs.seq_expand, jnp.concatenate(
            [m0, jnp.zeros((HEAD_DIM - ns, m0.shape[1]), F32)], axis=0))
        res = _mlstm_chunk(cs, q_ref[...], k_ref[...], v_ref[...], op_ref[...], sm_ref[...], smt_ref[0],
                           gbr_ref[...], gbc_ref[...], mng_ref[...], c_list, n_rows, m_rows)
        m_all = jnp.zeros((CHUNK_ROWS, SMALL_COLS), F32)
        for h, (out, c_new, keep, m_new, kw) in enumerate(res):
            o_ref[:, h * MLSTM_DV:(h + 1) * MLSTM_DV] = out
            cout_ref[0, :, h] = c_new.reshape(ns, HEAD_DIM, MLSTM_DV)
            keep_seq = _sel_mm(cs.seq_last, jnp.broadcast_to(keep, (CHUNK_ROWS, HEAD_DIM)))
            nout_ref[:, h * HEAD_DIM:(h + 1) * HEAD_DIM] = (
                keep_seq * n0[:, h * HEAD_DIM:(h + 1) * HEAD_DIM] + _sel_mm(cs.seq_sum, kw))
            m_all = jnp.where(lane == h, m_new, m_all)
        mout_ref[...] = _sel_mm(cs.seq_last, m_all)

    @pl.when(pl.program_id(1) > 0)
    def _():
        cout_ref[...] = jnp.zeros_like(cout_ref)


def _mlstm_sample(proj, smt, gb_row, gb_col, mng, c_all, layer, c_out_prev, n0, m0, lt):
    hq = N_HEADS * HEAD_DIM
    hv = N_HEADS * MLSTM_DV
    ns = CHUNK_ROWS // lt
    t = proj.shape[0]
    n_layers = c_all.shape[0]
    small_blk = (2 * hq + 2 * hv) // SMALL_COLS
    c_shape = (1, ns, N_HEADS, HEAD_DIM, MLSTM_DV)
    state_n = pl.BlockSpec((ns, hq), lambda c, j: (c, 0))
    state_m = pl.BlockSpec((ns, SMALL_COLS), lambda c, j: (c, 0))
    in_specs = [pl.BlockSpec((CHUNK_ROWS, hq), lambda c, j: (c, 0)),
                pl.BlockSpec((CHUNK_ROWS, hq), lambda c, j: (c, 1)),
                pl.BlockSpec((CHUNK_ROWS, hv), lambda c, j: (c, 2 * hq // hv)),
                pl.BlockSpec((CHUNK_ROWS, hv), lambda c, j: (c, 2 * hq // hv + 1)),
                pl.BlockSpec((CHUNK_ROWS, SMALL_COLS), lambda c, j: (c, small_blk)),
                pl.BlockSpec((1, SMALL_ROWS, CHUNK_ROWS), lambda c, j: (c, 0, 0)),
                pl.BlockSpec((1, SMALL_COLS), lambda c, j: (0, 0)),
                pl.BlockSpec((SMALL_ROWS, 1), lambda c, j: (0, 0)),
                pl.BlockSpec((1, MLSTM_DV), lambda c, j: (0, 0)),
                pl.BlockSpec(c_shape, lambda c, j: (layer, c, 0, 0, 0)), state_n, state_m]
    args = [proj, proj, proj, proj, proj, smt, gb_row, gb_col, mng.reshape(1, MLSTM_DV), c_all, n0, m0]
    aliases = {}
    if c_out_prev is not None:
        aliases = {len(args): 1}
        in_specs.append(pl.BlockSpec(memory_space=pl.ANY))
        args.append(c_out_prev)
    return pl.pallas_call(
        functools.partial(_mlstm_sample_kernel, ns=ns),
        grid=(t // CHUNK_ROWS, 1 if aliases else n_layers),
        in_specs=in_specs,
        out_specs=[pl.BlockSpec((CHUNK_ROWS, hv), lambda c, j: (c, 0)),
                   pl.BlockSpec(c_shape, lambda c, j: (lax.rem(layer + j, n_layers), c, 0, 0, 0)),
                   state_n, state_m],
        out_shape=[jax.ShapeDtypeStruct((t, hv), F32),
                   jax.ShapeDtypeStruct(c_all.shape, F32),
                   jax.ShapeDtypeStruct(n0.shape, F32),
                   jax.ShapeDtypeStruct(m0.shape, F32)],
        input_output_aliases=aliases,
        compiler_params=_cparams(("parallel", "arbitrary")),
        name="mlstm_sample",
    )(*args)


def _pad_cols(w, n):
    return jnp.pad(w, ((0, 0), (0, n - w.shape[1])))


def _split_in_weight_kernel(w_ref, main_ref, st_ref, *, n_a, n_small):
    w = w_ref[0]
    rows = w.shape[0]
    n_b = w.shape[1] - n_a - n_small
    main_ref[0, :, 0:n_a] = w[:, 0:n_a].astype(BF16)
    main_ref[0, :, n_a:n_a + n_b] = w[:, n_a + n_small:].astype(BF16)
    small = jnp.concatenate([w[:, n_a:n_a + n_small], jnp.zeros((rows, SMALL_COLS - n_small), F32)],
                            axis=1)
    main_ref[0, :, n_a + n_b:] = small.astype(BF16)
    st_ref[0] = small.T[0:SMALL_ROWS, :].astype(BF16)


def _split_in_weight(w, n_a, n_small):
    layers, d, n_tot = w.shape
    n_main = n_tot - n_small + SMALL_COLS
    rows = 256
    return pl.pallas_call(
        functools.partial(_split_in_weight_kernel, n_a=n_a, n_small=n_small),
        grid=(layers, d // rows),
        in_specs=[pl.BlockSpec((1, rows, n_tot), lambda l, r: (l, r, 0))],
        out_specs=[pl.BlockSpec((1, rows, n_main), lambda l, r: (l, r, 0)),
                   pl.BlockSpec((1, SMALL_ROWS, rows), lambda l, r: (l, 0, r))],
        out_shape=[jax.ShapeDtypeStruct((layers, d, n_main), BF16),
                   jax.ShapeDtypeStruct((layers, SMALL_ROWS, d), BF16)],
        compiler_params=_cparams(("parallel", "parallel")),
        name="split_in_weight",
    )(w)


def _chunk_lanes(smt):
    rows, t = smt.shape
    return smt.reshape(rows, t // CHUNK_ROWS, CHUNK_ROWS).transpose(1, 0, 2)


def _row_col_params(vals):
    row = jnp.stack([jnp.pad(v, (0, SMALL_COLS - v.shape[0])) for v in vals]).astype(F32)
    col = jnp.stack([jnp.pad(v, (0, SMALL_ROWS - v.shape[0])) for v in vals], axis=1).astype(F32)
    return row, col


def _trunk(x, is_prompt, s_gdn, s_conv, s_c, s_n, s_m, norm_g, w_in_even, conv_w, a_log, dt_bias,
           gdn_norm_g, sgu_norm_g, sgu_w, sgu_b, w_out_even, w_in_odd, gate_b_odd, mlstm_norm_g,
           w_out_odd, w_ff1, w_ff2):
    bsz, seq, d = x.shape
    t = bsz * seq
    depth = norm_g.shape[0]
    hq = N_HEADS * HEAD_DIM
    cch = 3 * hq
    tm = 1024 if t % 1024 == 0 else 512
    tb = min(512, seq)
    zeros_h = jnp.zeros((N_HEADS,), F32)
    x2 = x.reshape(t, d)
    new_gdn, new_conv, new_v, new_c, new_n, new_m = [], [], [], [], [], []
    gdn_out = c_out = None
    for l in range(depth):
        gl = norm_g[l]
        if l % 2 == 0:
            e = l // 2
            prm_row, prm_col = _row_col_params([jnp.concatenate([zeros_h, jnp.exp(a_log[e])]),
                                                jnp.concatenate([zeros_h, dt_bias[e]])])
            if is_prompt:
                buf8 = jnp.pad(s_conv[e], ((0, 0), (8 - (CONV_W - 1), 0), (0, 0)))
                o_mix, s_new, tail = _even_prompt(
                    x2, gl[0], *w_in_even, e, conv_w[e], prm_row, prm_col, gdn_norm_g[e], s_gdn[e], buf8,
                    sgu_w[e], _pad_cols(sgu_b[e].T, SMALL_COLS), sgu_norm_g[e], bsz, seq, tb)
                buf_new = tail[:, 8 - (CONV_W - 1):, :]
                new_gdn.append(s_new)
                mix_acts, mix_idx = [o_mix], [(e, 0)]
            else:
                proj, smt = _proj_in(x2, gl[0], *w_in_even, e, 512)
                qkv = proj[:, :cch].reshape(bsz, seq, cch)
                xcat = jnp.concatenate([s_conv[e], qkv], axis=1)
                buf_new = xcat[:, seq:, :]
                xc = _conv_sample(xcat.reshape(bsz, (CONV_W - 1 + seq) * cch), conv_w[e], seq, cch)
                o_a, gdn_out = _gdn_sample(xc.reshape(t, cch), proj, _chunk_lanes(smt), prm_row, prm_col,
                                           gdn_norm_g[e], s_gdn, e, gdn_out, seq)
                ns = CHUNK_ROWS // seq
                w_tiles = jnp.tile(sgu_w[e][:, :seq, :seq], (1, ns, ns))
                b_cols = _pad_cols(jnp.tile(sgu_b[e][:, :seq].T, (ns, 1)), SMALL_COLS)
                o_b, vb = _sgu(proj, w_tiles, b_cols, sgu_norm_g[e], CHUNK_ROWS, ns, 1)
                new_v.append(vb.reshape(bsz, seq, hq))
                mix_acts, mix_idx = [o_a, o_b], [(e, 0), (e, 1)]
            new_conv.append(buf_new)
            w_out_all = w_out_even
        else:
            o = l // 2
            gb_row, gb_col = _row_col_params([gate_b_odd[o]])
            if is_prompt:
                n0 = s_n[o].reshape(bsz, 1, hq)
                m0 = _pad_cols(s_m[o], SMALL_COLS).reshape(bsz, 1, SMALL_COLS)
                hh, c_new, n_new, m_new = _odd_prompt(x2, gl[0], *w_in_odd, o, gb_row, gb_col,
                                                      mlstm_norm_g[o], s_c[o], n0, m0, bsz, seq, tb)
                m_new = m_new.reshape(bsz, SMALL_COLS)
                new_c.append(c_new)
            else:
                proj, smt = _proj_in(x2, gl[0], *w_in_odd, o, 512)
                n0 = s_n[o].reshape(bsz, hq)
                m0 = _pad_cols(s_m[o], SMALL_COLS)
                hh, c_out, n_new, m_new = _mlstm_sample(proj, _chunk_lanes(smt), gb_row, gb_col,
                                                        mlstm_norm_g[o], s_c, o, c_out, n0, m0, seq)
            new_n.append(n_new.reshape(bsz, N_HEADS, HEAD_DIM))
            new_m.append(m_new[:, :N_HEADS])
            mix_acts, mix_idx, w_out_all = [hh], [(o, 0)], w_out_odd
        x2 = _mix_ffn(x2, gl[1:4], mix_acts, w_out_all, mix_idx, w_ff1, w_ff2, l, tm, 1024)
    return (x2.reshape(bsz, seq, d),
            jnp.stack(new_gdn) if is_prompt else gdn_out, jnp.stack(new_conv),
            None if is_prompt else jnp.stack(new_v),
            jnp.stack(new_c) if is_prompt else c_out, jnp.stack(new_n), jnp.stack(new_m))


def kernel(x_prompt, x_sample, state_gdn, state_gdn_conv, state_mlstm_c, state_mlstm_n, state_mlstm_m,
           norm_g, w_in_even, conv_w, a_log, dt_bias, gdn_norm_g, sgu_norm_g, sgu_w, sgu_b, w_out_even,
           w_in_odd, gate_b_odd, mlstm_norm_g, w_out_odd, w_ff1, w_ff2):
    hq = N_HEADS * HEAD_DIM
    weights = (norm_g, _split_in_weight(w_in_even, 4 * hq, 2 * N_HEADS), conv_w, a_log, dt_bias,
               gdn_norm_g, sgu_norm_g, sgu_w, sgu_b, w_out_even.astype(BF16),
               _split_in_weight(w_in_odd, 2 * hq + N_HEADS * MLSTM_DV, 2 * N_HEADS), gate_b_odd,
               mlstm_norm_g, w_out_odd.astype(BF16), w_ff1.astype(BF16), w_ff2.astype(BF16))
    bp = x_prompt.shape[0]
    n_even, n_odd = state_gdn.shape[0], state_mlstm_c.shape[0]
    y_prompt, p_gdn, p_conv, _, p_c, p_n, p_m = _trunk(
        x_prompt, True,
        jnp.zeros((n_even, bp) + state_gdn.shape[2:], F32),
        jnp.zeros((n_even, bp) + state_gdn_conv.shape[2:], x_prompt.dtype),
        jnp.zeros((n_odd, bp) + state_mlstm_c.shape[2:], F32),
        jnp.zeros((n_odd, bp) + state_mlstm_n.shape[2:], F32),
        jnp.zeros((n_odd, bp) + state_mlstm_m.shape[2:], F32),
        *weights)
    y_sample, s_gdn, s_conv, s_v, s_c, s_n, s_m = _trunk(
        x_sample, False, state_gdn, state_gdn_conv, state_mlstm_c, state_mlstm_n, state_mlstm_m,
        *weights)
    return (y_prompt, y_sample, p_gdn, s_gdn, p_conv, s_conv, s_v, p_c, s_c, p_n, s_n, p_m, s_m)
```

```python
import functools
import math
from types import SimpleNamespace

import jax
import jax.numpy as jnp
from jax import lax
from jax.experimental import pallas as pl
from jax.experimental.pallas import tpu as pltpu

F32 = jnp.float32
BF16 = jnp.bfloat16

EPS = 1e-6
N_HEADS = 4
HEAD_DIM = 128
MLSTM_DV = 256
CONV_W = 4
CHUNK_ROWS = 64
SGU_CHUNK = 128
SMALL_COLS = 128
SMALL_ROWS = 16
VMEM_LIMIT = 56 * 1024 * 1024


def _cparams(sem):
    return pltpu.CompilerParams(dimension_semantics=sem, vmem_limit_bytes=VMEM_LIMIT)


def _mm(a, b):
    return jnp.dot(a.astype(BF16), b.astype(BF16), preferred_element_type=F32)


def _mm_nt(a, b):
    return lax.dot_general(a.astype(BF16), b.astype(BF16), (((1,), (1,)), ((), ())),
                           preferred_element_type=F32)


def _trunc_bf16(x):
    bits = lax.bitcast_convert_type(x, jnp.uint32) & jnp.uint32(0xFFFF0000)
    return lax.bitcast_convert_type(bits, F32)


def _split3(x):
    hi = _trunc_bf16(x)
    rest = x - hi
    mid = _trunc_bf16(rest)
    return hi, mid, rest - mid


def _sel_mm(sel, x):
    pieces = jnp.concatenate(_split3(x), axis=0).astype(BF16)
    return jnp.dot(jnp.concatenate([sel.astype(BF16)] * 3, axis=1), pieces, preferred_element_type=F32)


def _mm_sel(x, sel):
    pieces = jnp.concatenate(_split3(x), axis=1).astype(BF16)
    return jnp.dot(pieces, jnp.concatenate([sel.astype(BF16)] * 3, axis=0), preferred_element_type=F32)


def _rms(x, g):
    return x * lax.rsqrt(jnp.mean(x * x, axis=-1, keepdims=True) + EPS) * g


def _l2norm(x):
    return x * lax.rsqrt(jnp.sum(x * x, axis=-1, keepdims=True) + EPS)


def _softplus(x):
    return jnp.maximum(x, 0.0) + jnp.log1p(jnp.exp(-jnp.abs(x)))


def _sigmoid(x):
    return 0.5 * jnp.tanh(0.5 * x) + 0.5


def _silu(x):
    return x * _sigmoid(x)


def _gelu(x):
    return 0.5 * x * (1.0 + lax.erf(x * (2.0 ** -0.5)))


def _proj_in_kernel(x_ref, g_ref, w_ref, wst_ref, o_ref, ot_ref):
    _project_block(x_ref, g_ref, w_ref, wst_ref, o_ref, ot_ref)


def _proj_in(x2d, g, w_main_all, w_small_t_all, layer, tm):
    t, d = x2d.shape
    n = w_main_all.shape[2]
    return pl.pallas_call(
        _proj_in_kernel,
        grid=(t // tm,),
        in_specs=[pl.BlockSpec((tm, d), lambda i: (i, 0)),
                  pl.BlockSpec((1, d), lambda i: (0, 0)),
                  pl.BlockSpec((1, d, n), lambda i: (layer, 0, 0)),
                  pl.BlockSpec((1, SMALL_ROWS, d), lambda i: (layer, 0, 0))],
        out_specs=[pl.BlockSpec((tm, n), lambda i: (i, 0)),
                   pl.BlockSpec((SMALL_ROWS, tm), lambda i: (0, i))],
        out_shape=[jax.ShapeDtypeStruct((t, n), F32),
                   jax.ShapeDtypeStruct((SMALL_ROWS, t), F32)],
        compiler_params=_cparams(("parallel",)),
        name="proj_in",
    )(x2d, g.reshape(1, d), w_main_all, w_small_t_all)


def _mix_ffn_kernel(*refs, n_act, nk):
    x_ref, g_ref = refs[0], refs[1]
    acts = refs[2:2 + n_act]
    wos = refs[2 + n_act:2 + 2 * n_act]
    w1_ref, w2_ref, o_ref, h_ref, acc_ref = refs[2 + 2 * n_act:]
    k = pl.program_id(1)
    tm = x_ref.shape[0]
    halves = [slice(0, tm // 2), slice(tm // 2, tm)]

    def prologue(rs):
        mix = None
        for a_ref, w_ref in zip(acts, wos):
            p = jnp.dot(a_ref[rs, :].astype(BF16), w_ref[0], preferred_element_type=F32)
            mix = p if mix is None else mix + p
        x1 = x_ref[rs, :] + _rms(mix, g_ref[0:1, :])
        o_ref[rs, :] = x1
        h_ref[rs, :] = _rms(x1, g_ref[1:2, :]).astype(BF16)

    def chunk(rs, first):
        a = jnp.dot(h_ref[rs, :], w1_ref[0], preferred_element_type=F32)
        a = jnp.square(jnp.maximum(a, 0.0)).astype(BF16)
        part = jnp.dot(a, w2_ref[0], preferred_element_type=F32)
        acc_ref[rs, :] = part if first else acc_ref[rs, :] + part

    def epilogue(rs):
        o_ref[rs, :] = o_ref[rs, :] + _rms(acc_ref[rs, :], g_ref[2:3, :])

    @pl.when(k == 0)
    def _():
        for q in range(4):
            prologue(slice(q * tm // 4, (q + 1) * tm // 4))
        chunk(slice(0, tm), True)
        if nk == 1:
            for rs in halves:
                epilogue(rs)

    @pl.when((k > 0) & (k < nk - 1))
    def _():
        chunk(slice(0, tm), False)

    @pl.when((k > 0) & (k == nk - 1))
    def _():
        for rs in halves:
            chunk(rs, False)
            epilogue(rs)


def _mix_ffn(x2d, g3rows, acts, w_out_all, w_out_idx, w1_all, w2_all, layer, tm, tf):
    t, d = x2d.shape
    ff = w1_all.shape[2]
    n_act = len(acts)
    in_specs = [pl.BlockSpec((tm, d), lambda i, k: (i, 0)), pl.BlockSpec((3, d), lambda i, k: (0, 0))]
    in_specs += [pl.BlockSpec((tm, a.shape[1]), lambda i, k: (i, 0)) for a in acts]
    in_specs += [pl.BlockSpec((1, a.shape[1], d), lambda i, k, li=li, ri=ri: (li, ri, 0))
                 for a, (li, ri) in zip(acts, w_out_idx)]
    in_specs += [pl.BlockSpec((1, d, tf), lambda i, k: (layer, 0, k)),
                 pl.BlockSpec((1, tf, d), lambda i, k: (layer, k, 0))]
    return pl.pallas_call(
        functools.partial(_mix_ffn_kernel, n_act=n_act, nk=ff // tf),
        grid=(t // tm, ff // tf),
        in_specs=in_specs,
        out_specs=pl.BlockSpec((tm, d), lambda i, k: (i, 0)),
        out_shape=jax.ShapeDtypeStruct((t, d), F32),
        scratch_shapes=[pltpu.VMEM((tm, d), BF16), pltpu.VMEM((tm, d), F32)],
        compiler_params=_cparams(("parallel", "arbitrary")),
        name="mix_ffn",
    )(x2d, g3rows, *acts, *([w_out_all] * n_act), w1_all, w2_all)


def _ind(mask):
    return jnp.where(mask, 1.0, 0.0).astype(F32)


def _div_pow2(x, d):
    return jnp.right_shift(x, int(math.log2(d)))


def _chunk_structs(rows, ns):
    lt = rows // ns
    ri = lax.broadcasted_iota(jnp.int32, (rows, rows), 0)
    ci = lax.broadcasted_iota(jnp.int32, (rows, rows), 1)
    cs = SimpleNamespace(rows=rows, ns=ns, lt=lt)
    if ns == 1:
        cs.same = None
        cs.causal = ri >= ci
        cs.strict = ri > ci
        cs.upper = ri <= ci
    else:
        rseq = _div_pow2(ri, lt)
        cseq = _div_pow2(ci, lt)
        cs.same = rseq == cseq
        cs.causal = cs.same & (ri >= ci)
        cs.strict = cs.same & (ri > ci)
        cs.upper = cs.same & (ri <= ci)
        cs.last = _ind(ci == rseq * lt + (lt - 1))
        cs.last_t = _ind(ri == cseq * lt + (lt - 1))
        sr = _div_pow2(lax.broadcasted_iota(jnp.int32, (ns * HEAD_DIM, rows), 0), HEAD_DIM)
        sc = lax.broadcasted_iota(jnp.int32, (ns * HEAD_DIM, rows), 1)
        cs.stack_last = _ind(sc == sr * lt + (lt - 1))
        cs.stack_mask = _ind(_div_pow2(sc, lt) == sr)
        er = _div_pow2(lax.broadcasted_iota(jnp.int32, (rows, ns * HEAD_DIM), 0), lt)
        ec = _div_pow2(lax.broadcasted_iota(jnp.int32, (rows, ns * HEAD_DIM), 1), HEAD_DIM)
        cs.expand_mask = _ind(er == ec)
        kr = lax.broadcasted_iota(jnp.int32, (ns, rows), 0)
        kc = lax.broadcasted_iota(jnp.int32, (ns, rows), 1)
        cs.seq_last = _ind(kc == kr * lt + (lt - 1))
        cs.seq_sum = _ind(_div_pow2(kc, lt) == kr)
        pr = _div_pow2(lax.broadcasted_iota(jnp.int32, (rows, HEAD_DIM), 0), lt)
        pc = lax.broadcasted_iota(jnp.int32, (rows, HEAD_DIM), 1)
        cs.seq_expand = _ind(pr == pc)
    cs.lower_f = _ind(cs.causal)
    cs.upper_f = _ind(cs.upper)
    eye_r = lax.broadcasted_iota(jnp.int32, (HEAD_DIM, HEAD_DIM), 0)
    eye_c = lax.broadcasted_iota(jnp.int32, (HEAD_DIM, HEAD_DIM), 1)
    cs.eye = _ind(eye_r == eye_c)
    cs.eye_rows = _ind(ri == ci)
    return cs


def _last_rows(cs, x):
    if cs.ns == 1:
        return jnp.broadcast_to(x[cs.rows - 1:cs.rows, :], x.shape)
    return _sel_mm(cs.last, x)


def _last_lanes(cs, x):
    if cs.ns == 1:
        return jnp.broadcast_to(x[:, cs.rows - 1:cs.rows], x.shape)
    return _mm_sel(x, cs.last_t)


def _expand_lhs(cs, x):
    if cs.ns == 1:
        return x
    reps = x.shape[-2] // cs.rows
    mask = cs.expand_mask if reps == 1 else jnp.concatenate([cs.expand_mask] * reps, axis=0)
    return jnp.concatenate([x] * cs.ns, axis=-1) * mask


def _stack_t(cs, x):
    xt = _mm_nt(cs.eye, x)
    if cs.ns == 1:
        return xt
    return jnp.concatenate([xt] * cs.ns, axis=0) * cs.stack_mask


def _stack_scalar(cs, col):
    if cs.ns == 1:
        return col[cs.rows - 1:cs.rows, :]
    return _sel_mm(cs.stack_last, jnp.broadcast_to(col, (cs.rows, HEAD_DIM)))[:, 0:1]


_BATCH_NN = (((2,), (1,)), ((0,), (0,)))
_BATCH_NT = (((2,), (2,)), ((0,), (0,)))
_BATCH_TN = (((1,), (1,)), ((0,), (0,)))


def _bmm(a, b, dims=_BATCH_NN):
    return lax.dot_general(a.astype(BF16), b.astype(BF16), dims, preferred_element_type=F32)


def _split_lhs(a):
    hi = a.astype(BF16).astype(F32)
    return jnp.concatenate([hi, a - hi, hi], axis=-1).astype(BF16)


def _split_rhs(b):
    hi = b.astype(BF16).astype(F32)
    return jnp.concatenate([hi, hi, b - hi], axis=-2).astype(BF16)


def _bmm_split(a_cat, b_cat):
    return lax.dot_general(a_cat, b_cat, _BATCH_NN, preferred_element_type=F32)


def _unit_lower_inverse_off(cs, a):
    rows = cs.rows
    n = -a
    steps = int(math.log2(cs.lt)) - 1
    if steps == 0:
        return n
    pt = jnp.concatenate([_bmm_split(_split_lhs(n), _split_rhs(n)), n], axis=-1)
    right = lax.broadcasted_iota(jnp.int32, (rows, 2 * rows), 1) >= rows
    for _ in range(steps):
        hi = pt.astype(BF16).astype(F32)
        lo = pt - hi
        p_cat = jnp.concatenate([hi[..., :rows], lo[..., :rows], hi[..., :rows]], axis=-1).astype(BF16)
        pt_cat = jnp.concatenate([hi, hi, lo], axis=-2).astype(BF16)
        p_both = jnp.concatenate([pt[..., :rows], pt[..., :rows]], axis=-1)
        pt = _bmm_split(p_cat, pt_cat) + jnp.where(right, pt + p_both, 0.0)
    return pt[..., rows:]


def _gdn_block(cs, nchunks, xc, gate, sm, smt, prm_row, prm_col, gng, s_heads, write_out, pieces=None):
    rows = cs.rows
    hq = N_HEADS * HEAD_DIM
    pieces = pieces or _Pieces()
    pairs = [(c, h) for c in range(nchunks) for h in range(N_HEADS)]
    rs = lambda c: slice(c * rows, (c + 1) * rows)

    def heads(x, col0):
        return jnp.stack([x[rs(c), col0 + h * HEAD_DIM:col0 + (h + 1) * HEAD_DIM] for c, h in pairs])

    def cols(x, lane0):
        return jnp.stack([x[rs(c), lane0 + h:lane0 + h + 1] for c, h in pairs])

    beta_all = _sigmoid(sm)
    g_c = -prm_row[0:1, :] * _softplus(sm + prm_row[1:2, :])
    g_r = -prm_col[:, 0:1] * _softplus(smt + prm_col[:, 1:2])
    gc_c = jnp.concatenate([_sel_mm(cs.lower_f, g_c[rs(c)]) for c in range(nchunks)], axis=0)
    gc_r = jnp.concatenate([_mm_sel(g_r[:, rs(c)], cs.upper_f) for c in range(nchunks)], axis=1)
    gl_c = jnp.concatenate([_last_rows(cs, gc_c[rs(c)]) for c in range(nchunks)], axis=0)
    egc = jnp.exp(gc_c)
    ekd = jnp.exp(gl_c - gc_c)
    pieces.take()

    q = _l2norm(heads(xc, 0)) * (HEAD_DIM ** -0.5)
    k = _l2norm(heads(xc, hq))
    v = heads(xc, 2 * hq)
    beta = cols(beta_all, 0)
    gcol = cols(gc_c, N_HEADS)
    eg = cols(egc, N_HEADS)
    grow = jnp.stack([gc_r[N_HEADS + h:N_HEADS + h + 1, rs(c)] for c, h in pairs])
    decay = jnp.where(cs.causal, jnp.exp(jnp.where(cs.causal, gcol - grow, 0.0)), 0.0)
    kb = k * beta
    a = jnp.where(cs.strict, _bmm(kb, k, _BATCH_NT) * decay, 0.0)
    toff = _unit_lower_inverse_off(cs, a)
    rhs = jnp.concatenate([v * beta, kb * eg], axis=-1)
    sol = rhs + _bmm_split(_split_lhs(toff), _split_rhs(rhs))
    u_val, w_k = sol[..., :HEAD_DIM], sol[..., HEAD_DIM:]
    pieces.take()
    qk = jnp.where(cs.causal, _bmm(q, k, _BATCH_NT) * decay, 0.0)
    q_dec = q * eg
    k_dec = k * cols(ekd, N_HEADS)

    s = s_heads
    for c in range(nchunks):
        pieces.take()
        hs = slice(c * N_HEADS, (c + 1) * N_HEADS)
        p = _bmm(_expand_lhs(cs, jnp.concatenate([w_k[hs], q_dec[hs]], axis=1)), s)
        v_new = u_val[hs] - p[:, :rows]
        o = p[:, rows:] + _bmm(qk[hs], v_new)
        gl_cols = [gl_c[rs(c), N_HEADS + h:N_HEADS + h + 1] for h in range(N_HEADS)]
        g_last = jnp.exp(jnp.stack([_stack_scalar(cs, col) for col in gl_cols]))
        if cs.ns == 1:
            upd = _bmm(k_dec[hs], v_new, _BATCH_TN)
        else:
            kt = _bmm(jnp.broadcast_to(cs.eye, (N_HEADS, HEAD_DIM, HEAD_DIM)), k_dec[hs], _BATCH_NT)
            upd = _bmm(jnp.concatenate([kt] * cs.ns, axis=1) * cs.stack_mask, v_new)
        s = s * g_last + upd
        for h in range(N_HEADS):
            write_out(c, h, _rms(o[h], gng) * _silu(gate[rs(c), h * HEAD_DIM:(h + 1) * HEAD_DIM]))
    return s


PROJ_PIECE_COLS = 256


class _Pieces:
    def __init__(self, thunks=()):
        self._thunks = list(thunks)

    def take(self, count=1):
        for _ in range(count):
            if self._thunks:
                self._thunks.pop(0)()

    def flush(self):
        self.take(len(self._thunks))


def _project_pieces(x_ref, g_ref, w_ref, wst_ref, proj_w, smt_w):
    hn = _rms(x_ref[...], g_ref[...]).astype(BF16)
    n = proj_w.shape[1]

    def piece(lo, hi):
        def run():
            proj_w[:, lo:hi] = jnp.dot(hn, w_ref[0, :, lo:hi], preferred_element_type=F32)
        return run

    def gates_t():
        smt_w[...] = lax.dot_general(wst_ref[0], hn, (((1,), (1,)), ((), ())), preferred_element_type=F32)

    cols = range(0, n, PROJ_PIECE_COLS)
    return _Pieces([gates_t] + [piece(lo, min(lo + PROJ_PIECE_COLS, n)) for lo in cols])


def _project_block(x_ref, g_ref, w_ref, wst_ref, proj_w, smt_w):
    _project_pieces(x_ref, g_ref, w_ref, wst_ref, proj_w, smt_w).flush()


def _pipelined_steps(step):
    i = pl.program_id(0)
    for parity in range(2):
        @pl.when(lax.rem(i, 2) == parity)
        def _():
            step(parity)


def _even_prompt_kernel(x_ref, g_ref, w_ref, wst_ref, cw_ref, prow_ref, pcol_ref, gng_ref, s0_ref, buf_ref,
                        sw_ref, sb_ref, sg_ref, o_ref, sout_ref, bufout_ref,
                        proj_a, proj_b, smt_a, smt_b, s_scr, xp_scr, *, tb, nt):
    i = pl.program_id(0)
    t = lax.rem(jnp.maximum(i - 1, 0), nt)
    hq = N_HEADS * HEAD_DIM
    cch = 3 * hq
    cs = _chunk_structs(CHUNK_ROWS, 1)

    @pl.when(i == 0)
    def _():
        proj_b[...] = jnp.zeros_like(proj_b)
        smt_b[...] = jnp.zeros_like(smt_b)

    @pl.when(t == 0)
    def _():
        s_scr[...] = s0_ref[0]
        xp_scr[...] = buf_ref[0]

    def write_delta(c, h, o):
        o_ref[c * CHUNK_ROWS:(c + 1) * CHUNK_ROWS, h * HEAD_DIM:(h + 1) * HEAD_DIM] = o

    def write_gating(rs, g, val):
        o_ref[rs, hq + g * HEAD_DIM:hq + (g + 1) * HEAD_DIM] = val

    def step(parity):
        proj_w, smt_w, proj_r, smt_r = ((proj_a, smt_a, proj_b, smt_b) if parity == 0
                                        else (proj_b, smt_b, proj_a, smt_a))
        pieces = _project_pieces(x_ref, g_ref, w_ref, wst_ref, proj_w, smt_w)
        pieces.take(3)
        x = proj_r[:, 0:cch]
        xp = jnp.concatenate([xp_scr[...], x], axis=0)
        y = cw_ref[CONV_W - 1:CONV_W, :] * x
        for back in range(1, CONV_W):
            y = y + cw_ref[CONV_W - 1 - back:CONV_W - back, :] * pltpu.roll(xp, back, axis=0)[8:]
        xp_scr[...] = x[tb - 8:]
        pieces.take(2)
        s_scr[...] = _gdn_block(cs, tb // CHUNK_ROWS, _silu(y), proj_r[:, cch:cch + hq],
                                proj_r[:, cch + 3 * hq:cch + 3 * hq + SMALL_COLS], smt_r[...],
                                prow_ref[...], pcol_ref[...], gng_ref[...], s_scr[...], write_delta, pieces)
        pieces.flush()
        _sgu_apply(proj_r[:, cch + hq:cch + 2 * hq], proj_r[:, cch + 2 * hq:cch + 3 * hq],
                   sw_ref, sb_ref[...], sg_ref[...], SGU_CHUNK, 1, write_gating, None)

    _pipelined_steps(step)

    @pl.when(t == nt - 1)
    def _():
        sout_ref[0] = s_scr[...]
        bufout_ref[0] = xp_scr[...]


def _even_prompt(x2, g, w_main_all, w_small_t_all, e, conv_w, prm_row, prm_col, gng, s0, buf8,
                 sgu_w, sgu_b_cols, sgu_g, bsz, seq, tb):
    t, d = x2.shape
    hq = N_HEADS * HEAD_DIM
    cch = 3 * hq
    n_proj = w_main_all.shape[2]
    n = t // tb
    nt = seq // tb
    blk = lambda i: jnp.minimum(i, n - 1)
    seq_of = lambda i: jnp.maximum(i - 1, 0) // nt
    const2 = lambda i: (0, 0)
    return pl.pallas_call(
        functools.partial(_even_prompt_kernel, tb=tb, nt=nt),
        grid=(n + 1,),
        in_specs=[pl.BlockSpec((tb, d), lambda i: (blk(i), 0)),
                  pl.BlockSpec((1, d), const2),
                  pl.BlockSpec((1, d, n_proj), lambda i: (e, 0, 0)),
                  pl.BlockSpec((1, SMALL_ROWS, d), lambda i: (e, 0, 0)),
                  pl.BlockSpec((CONV_W, cch), const2),
                  pl.BlockSpec((2, SMALL_COLS), const2),
                  pl.BlockSpec((SMALL_ROWS, 2), const2),
                  pl.BlockSpec((1, HEAD_DIM), const2),
                  pl.BlockSpec((1, N_HEADS, HEAD_DIM, HEAD_DIM), lambda i: (seq_of(i), 0, 0, 0)),
                  pl.BlockSpec((1, 8, cch), lambda i: (seq_of(i), 0, 0)),
                  pl.BlockSpec((N_HEADS, SGU_CHUNK, SGU_CHUNK), lambda i: (0, 0, 0)),
                  pl.BlockSpec((SGU_CHUNK, SMALL_COLS), const2),
                  pl.BlockSpec((N_HEADS, HEAD_DIM), const2)],
        out_specs=[pl.BlockSpec((tb, 2 * hq), lambda i: (jnp.maximum(i - 1, 0), 0)),
                   pl.BlockSpec((1, N_HEADS, HEAD_DIM, HEAD_DIM), lambda i: (seq_of(i), 0, 0, 0)),
                   pl.BlockSpec((1, 8, cch), lambda i: (seq_of(i), 0, 0))],
        out_shape=[jax.ShapeDtypeStruct((t, 2 * hq), F32),
                   jax.ShapeDtypeStruct((bsz, N_HEADS, HEAD_DIM, HEAD_DIM), F32),
                   jax.ShapeDtypeStruct((bsz, 8, cch), F32)],
        scratch_shapes=[pltpu.VMEM((tb, n_proj), F32), pltpu.VMEM((tb, n_proj), F32),
                        pltpu.VMEM((SMALL_ROWS, tb), F32), pltpu.VMEM((SMALL_ROWS, tb), F32),
                        pltpu.VMEM((N_HEADS, HEAD_DIM, HEAD_DIM), F32),
                        pltpu.VMEM((8, cch), F32)],
        compiler_params=_cparams(("arbitrary",)),
        name="even_prompt",
    )(x2, g.reshape(1, d), w_main_all, w_small_t_all, conv_w, prm_row, prm_col,
      gng.reshape(1, HEAD_DIM), s0, buf8, sgu_w, sgu_b_cols, sgu_g)


def _conv_sample_kernel(xcat_ref, cw_ref, o_ref, *, lt, cch):
    for t in range(lt):
        y = None
        for j in range(CONV_W):
            term = cw_ref[j:j + 1, :] * xcat_ref[:, (t + j) * cch:(t + j + 1) * cch]
            y = term if y is None else y + term
        o_ref[:, t * cch:(t + 1) * cch] = _silu(y)


def _conv_sample(xcat, conv_w, lt, cch):
    nseq = xcat.shape[0]
    return pl.pallas_call(
        functools.partial(_conv_sample_kernel, lt=lt, cch=cch),
        grid=(1,),
        in_specs=[pl.BlockSpec(xcat.shape, lambda i: (0, 0)),
                  pl.BlockSpec((CONV_W, cch), lambda i: (0, 0))],
        out_specs=pl.BlockSpec((nseq, lt * cch), lambda i: (0, 0)),
        out_shape=jax.ShapeDtypeStruct((nseq, lt * cch), F32),
        compiler_params=_cparams(("arbitrary",)),
        name="conv_sample",
    )(xcat, conv_w)


def _gdn_sample_kernel(xc_ref, gate_ref, sm_ref, smt_ref, prow_ref, pcol_ref, gng_ref, s0_ref,
                       *rest, ns):
    o_ref, sout_ref = rest[-2:]
    cs = _chunk_structs(CHUNK_ROWS, ns)
    s_heads = jnp.stack([s0_ref[0, :, h].reshape(ns * HEAD_DIM, HEAD_DIM) for h in range(N_HEADS)])

    def write_out(c, h, o):
        o_ref[:, h * HEAD_DIM:(h + 1) * HEAD_DIM] = o

    s_new = _gdn_block(cs, 1, xc_ref[...], gate_ref[...], sm_ref[...], smt_ref[0],
                       prow_ref[...], pcol_ref[...], gng_ref[...], s_heads, write_out)
    for h in range(N_HEADS):
        sout_ref[0, :, h] = s_new[h].reshape(ns, HEAD_DIM, HEAD_DIM)


def _gdn_sample(xc, proj, smt, prm_row, prm_col, gng, s_all, layer, s_out_prev, lt):
    hq = N_HEADS * HEAD_DIM
    cch = 3 * hq
    ns = CHUNK_ROWS // lt
    nchunk = xc.shape[0] // CHUNK_ROWS
    small_blk = (cch + 3 * hq) // SMALL_COLS
    state_spec = pl.BlockSpec((1, ns, N_HEADS, HEAD_DIM, HEAD_DIM), lambda c: (layer, c, 0, 0, 0))
    in_specs = [pl.BlockSpec((CHUNK_ROWS, cch), lambda c: (c, 0)),
                pl.BlockSpec((CHUNK_ROWS, hq), lambda c: (c, cch // hq)),
                pl.BlockSpec((CHUNK_ROWS, SMALL_COLS), lambda c: (c, small_blk)),
                pl.BlockSpec((1, SMALL_ROWS, CHUNK_ROWS), lambda c: (c, 0, 0)),
                pl.BlockSpec((2, SMALL_COLS), lambda c: (0, 0)),
                pl.BlockSpec((SMALL_ROWS, 2), lambda c: (0, 0)),
                pl.BlockSpec((1, HEAD_DIM), lambda c: (0, 0)),
                state_spec,
                pl.BlockSpec(memory_space=pl.ANY)]
    args = [xc, proj, proj, smt, prm_row, prm_col, gng.reshape(1, HEAD_DIM), s_all, s_out_prev]
    return pl.pallas_call(
        functools.partial(_gdn_sample_kernel, ns=ns),
        grid=(nchunk,),
        in_specs=in_specs,
        out_specs=[pl.BlockSpec((CHUNK_ROWS, hq), lambda c: (c, 0)), state_spec],
        out_shape=[jax.ShapeDtypeStruct((xc.shape[0], hq), F32),
                   jax.ShapeDtypeStruct(s_all.shape, F32)],
        input_output_aliases={len(args) - 1: 1},
        compiler_params=_cparams(("parallel",)),
        name="gdn_sample",
    )(*args)


def _sgu_apply(u_pre, v_pre, w_ref, b_cols, g_rows, rows, ns, write_o, write_vb):
    lt = rows // ns
    ri = lax.broadcasted_iota(jnp.int32, (rows, rows), 0)
    ci = lax.broadcasted_iota(jnp.int32, (rows, rows), 1)
    keep = ri >= ci
    if ns > 1:
        keep = keep & (_div_pow2(ri, lt) == _div_pow2(ci, lt))
    for g in range(N_HEADS):
        lo = g * HEAD_DIM
        w = jnp.where(keep, w_ref[g], 0.0).astype(BF16)
        u = _gelu(u_pre[:, lo:lo + HEAD_DIM])
        vb = _rms(_gelu(v_pre[:, lo:lo + HEAD_DIM]), g_rows[g:g + 1, :])
        if write_vb is not None:
            write_vb(g, vb)
        for c in range(u_pre.shape[0] // rows):
            rs = slice(c * rows, (c + 1) * rows)
            write_o(rs, g, u[rs] * (_mm(w, vb[rs]) + b_cols[:, g:g + 1]))


def _sgu_kernel(u_ref, v_ref, w_ref, b_ref, g_ref, o_ref, vb_ref, *, rows, ns):
    def write_o(rs, g, val):
        o_ref[rs, g * HEAD_DIM:(g + 1) * HEAD_DIM] = val

    def write_vb(g, val):
        vb_ref[:, g * HEAD_DIM:(g + 1) * HEAD_DIM] = val

    _sgu_apply(u_ref[...], v_ref[...], w_ref, b_ref[...], g_ref[...], rows, ns, write_o, write_vb)


def _sgu(proj, w_tiles, b_cols, norm_g, rows, ns, nchunks):
    hq = N_HEADS * HEAD_DIM
    t = proj.shape[0]
    u_blk = (3 * hq + hq) // hq
    blk = rows * nchunks
    return pl.pallas_call(
        functools.partial(_sgu_kernel, rows=rows, ns=ns),
        grid=(t // blk,),
        in_specs=[pl.BlockSpec((blk, hq), lambda i: (i, u_blk)),
                  pl.BlockSpec((blk, hq), lambda i: (i, u_blk + 1)),
                  pl.BlockSpec((N_HEADS, rows, rows), lambda i: (0, 0, 0)),
                  pl.BlockSpec((rows, SMALL_COLS), lambda i: (0, 0)),
                  pl.BlockSpec((N_HEADS, HEAD_DIM), lambda i: (0, 0))],
        out_specs=[pl.BlockSpec((blk, hq), lambda i: (i, 0)),
                   pl.BlockSpec((blk, hq), lambda i: (i, 0))],
        out_shape=[jax.ShapeDtypeStruct((t, hq), F32), jax.ShapeDtypeStruct((t, hq), F32)],
        compiler_params=_cparams(("parallel",)),
        name="sgu",
    )(proj, proj, w_tiles, b_cols, norm_g)


def _logsigmoid(x):
    return jnp.minimum(x, 0.0) - jnp.log1p(jnp.exp(-jnp.abs(x)))


def _mlstm_chunk(cs, q_all, k_all, v_all, opre, sm, smt, gb_row, gb_col, mng, c_list, n_rows, m_rows):
    neg_inf = -jnp.inf
    pre_c = sm + gb_row
    pre_r = smt + gb_col
    b_c = _sel_mm(cs.lower_f, _logsigmoid(pre_c))
    b_r = _mm_sel(_logsigmoid(pre_r), cs.upper_f)
    bl_c = _last_rows(cs, b_c)
    bl_r = _last_lanes(cs, b_r)
    res = []
    for h in range(N_HEADS):
        lo = h * HEAD_DIM
        vo = h * MLSTM_DV
        q = q_all[:, lo:lo + HEAD_DIM] * (HEAD_DIM ** -0.5)
        k = k_all[:, lo:lo + HEAD_DIM]
        v = v_all[:, vo:vo + MLSTM_DV]
        bcol = b_c[:, N_HEADS + h:N_HEADS + h + 1]
        brow = b_r[N_HEADS + h:N_HEADS + h + 1, :]
        icol = pre_c[:, h:h + 1]
        irow = pre_r[h:h + 1, :]
        blcol = bl_c[:, N_HEADS + h:N_HEADS + h + 1]
        blrow = bl_r[N_HEADS + h:N_HEADS + h + 1, :]
        mrow = m_rows[:, h:h + 1]
        inter = bcol + mrow
        dmat = jnp.where(cs.causal, bcol - brow + irow, neg_inf)
        m_t = jnp.maximum(inter, jnp.max(dmat, axis=-1, keepdims=True))
        w_intra = jnp.exp(dmat - m_t)
        w_inter = jnp.exp(inter - m_t)
        s = _mm_nt(q, k) * w_intra
        c_old = c_list[h]
        num = w_inter * _mm(_expand_lhs(cs, q), c_old) + _mm(s, v)
        qn = jnp.sum(q * n_rows[:, lo:lo + HEAD_DIM], axis=-1, keepdims=True)
        den = w_inter * qn + jnp.sum(s, axis=-1, keepdims=True)
        hval = num / jnp.maximum(jnp.abs(den), jnp.exp(-m_t))
        out = _rms(hval, mng) * _sigmoid(opre[:, vo:vo + MLSTM_DV])
        logw_row = blrow - brow + irow
        if cs.ns == 1:
            seq_max = jnp.max(logw_row, axis=-1, keepdims=True)
        else:
            seq_max = jnp.max(jnp.where(cs.same, logw_row, neg_inf), axis=-1, keepdims=True)
        m_new = jnp.maximum(blcol + mrow, seq_max)
        keep = jnp.exp(blcol + mrow - m_new)
        wk = jnp.exp(blcol - bcol + icol - m_new)
        kw = k * wk
        c_new = c_old * _stack_scalar(cs, keep) + _mm(_stack_t(cs, kw), v)
        res.append((out, c_new, keep, m_new, kw))
    return res


def _mlstm_prompt_block(cs, nchunks, q_all, k_all, v_all, opre, sm, smt, gb_row, gb_col, mng,
                        c_heads, n_heads, m_heads, write_out, pieces):
    rows = cs.rows
    pairs = [(c, h) for c in range(nchunks) for h in range(N_HEADS)]
    rs = lambda c: slice(c * rows, (c + 1) * rows)

    def heads(x, width):
        return jnp.stack([x[rs(c), h * width:(h + 1) * width] for c, h in pairs])

    def cols(x, lane0):
        return jnp.stack([x[rs(c), lane0 + h:lane0 + h + 1] for c, h in pairs])

    def lanes(x, row0):
        return jnp.stack([x[row0 + h:row0 + h + 1, rs(c)] for c, h in pairs])

    pre_c = sm + gb_row
    pre_r = smt + gb_col
    lf_c = _logsigmoid(pre_c)
    lf_r = _logsigmoid(pre_r)
    b_c = jnp.concatenate([_sel_mm(cs.lower_f, lf_c[rs(c)]) for c in range(nchunks)], axis=0)
    b_r = jnp.concatenate([_mm_sel(lf_r[:, rs(c)], cs.upper_f) for c in range(nchunks)], axis=1)
    pieces.take()

    q = heads(q_all, HEAD_DIM) * (HEAD_DIM ** -0.5)
    k = heads(k_all, HEAD_DIM)
    v = heads(v_all, MLSTM_DV)
    bcol = cols(b_c, N_HEADS)
    icol = cols(pre_c, 0)
    brow = lanes(b_r, N_HEADS)
    irow = lanes(pre_r, 0)
    blast = bcol[:, rows - 1:rows, :]
    dmat = jnp.where(cs.causal, bcol - brow + irow, -jnp.inf)
    rmax = jnp.max(dmat, axis=-1, keepdims=True)
    sp = _bmm(q, k, _BATCH_NT) * jnp.exp(dmat - rmax)
    pieces.take()
    sv = _bmm(sp, v)
    ssum = jnp.sum(sp, axis=-1, keepdims=True)
    pieces.take()
    lmax = jnp.max(blast - brow + irow, axis=-1, keepdims=True)
    kwp = k * jnp.exp(blast - bcol + icol - lmax)
    upd = _bmm(kwp, v, _BATCH_TN)
    nsum = jnp.sum(kwp, axis=1, keepdims=True)
    pieces.take()

    cst, n, m = c_heads, n_heads, m_heads
    for c in range(nchunks):
        pieces.take()
        hs = slice(c * N_HEADS, (c + 1) * N_HEADS)
        inter = bcol[hs] + m
        m_t = jnp.maximum(inter, rmax[hs])
        w_inter = jnp.exp(inter - m_t)
        w_intra = jnp.exp(rmax[hs] - m_t)
        num = w_inter * _bmm(q[hs], cst) + w_intra * sv[hs]
        den = (w_inter * jnp.sum(q[hs] * n, axis=-1, keepdims=True) + w_intra * ssum[hs])
        hval = num / jnp.maximum(jnp.abs(den), jnp.exp(-m_t))
        for h in range(N_HEADS):
            write_out(c, h, _rms(hval[h], mng) * _sigmoid(opre[rs(c), h * MLSTM_DV:(h + 1) * MLSTM_DV]))
        m_new = jnp.maximum(blast[hs] + m, lmax[hs])
        keep = jnp.exp(blast[hs] + m - m_new)
        fresh = jnp.exp(lmax[hs] - m_new)
        cst = keep * cst + fresh * upd[hs]
        n = keep * n + fresh * nsum[hs]
        m = m_new
    return cst, n, m


def _odd_prompt_kernel(x_ref, g_ref, w_ref, wst_ref, gbr_ref, gbc_ref, mng_ref, c0_ref, n0_ref, m0_ref,
                       o_ref, cout_ref, nout_ref, mout_ref,
                       proj_a, proj_b, smt_a, smt_b, c_scr, n_scr, m_scr, *, tb, nt):
    i = pl.program_id(0)
    t = lax.rem(jnp.maximum(i - 1, 0), nt)
    hq = N_HEADS * HEAD_DIM
    hv = N_HEADS * MLSTM_DV
    cs = _chunk_structs(CHUNK_ROWS, 1)
    lane = lax.broadcasted_iota(jnp.int32, (1, SMALL_COLS), 1)

    @pl.when(i == 0)
    def _():
        proj_b[...] = jnp.zeros_like(proj_b)
        smt_b[...] = jnp.zeros_like(smt_b)

    @pl.when(t == 0)
    def _():
        c_scr[...] = c0_ref[0]
        n_scr[...] = n0_ref[0]
        m_scr[...] = m0_ref[0]

    def write_out(c, h, o):
        o_ref[c * CHUNK_ROWS:(c + 1) * CHUNK_ROWS, h * MLSTM_DV:(h + 1) * MLSTM_DV] = o

    def step(parity):
        proj_w, smt_w, proj_r, smt_r = ((proj_a, smt_a, proj_b, smt_b) if parity == 0
                                        else (proj_b, smt_b, proj_a, smt_a))
        pieces = _project_pieces(x_ref, g_ref, w_ref, wst_ref, proj_w, smt_w)
        pieces.take(2)
        n_cur = n_scr[...]
        m_cur = m_scr[...]
        n_heads = jnp.stack([n_cur[:, h * HEAD_DIM:(h + 1) * HEAD_DIM] for h in range(N_HEADS)])
        m_heads = jnp.stack([m_cur[:, h:h + 1] for h in range(N_HEADS)])
        c_new, n_heads, m_heads = _mlstm_prompt_block(
            cs, tb // CHUNK_ROWS, proj_r[:, 0:hq], proj_r[:, hq:2 * hq], proj_r[:, 2 * hq:2 * hq + hv],
            proj_r[:, 2 * hq + hv:2 * hq + 2 * hv],
            proj_r[:, 2 * hq + 2 * hv:2 * hq + 2 * hv + SMALL_COLS], smt_r[...],
            gbr_ref[...], gbc_ref[...], mng_ref[...], c_scr[...], n_heads, m_heads, write_out, pieces)
        pieces.flush()
        c_scr[...] = c_new
        n_scr[...] = jnp.concatenate([n_heads[h] for h in range(N_HEADS)], axis=1)
        for h in range(N_HEADS):
            m_cur = jnp.where(lane == h, m_heads[h], m_cur)
        m_scr[...] = m_cur

    _pipelined_steps(step)

    @pl.when(t == nt - 1)
    def _():
        cout_ref[0] = c_scr[...]
        nout_ref[0] = n_scr[...]
        mout_ref[0] = m_scr[...]


def _odd_prompt(x2, g, w_main_all, w_small_t_all, o, gb_row, gb_col, mng, c0, n0, m0, bsz, seq, tb):
    t, d = x2.shape
    hq = N_HEADS * HEAD_DIM
    hv = N_HEADS * MLSTM_DV
    n_proj = w_main_all.shape[2]
    n = t // tb
    nt = seq // tb
    blk = lambda i: jnp.minimum(i, n - 1)
    seq_of = lambda i: jnp.maximum(i - 1, 0) // nt
    const2 = lambda i: (0, 0)
    state_c = pl.BlockSpec((1, N_HEADS, HEAD_DIM, MLSTM_DV), lambda i: (seq_of(i), 0, 0, 0))
    state_n = pl.BlockSpec((1, 1, hq), lambda i: (seq_of(i), 0, 0))
    state_m = pl.BlockSpec((1, 1, SMALL_COLS), lambda i: (seq_of(i), 0, 0))
    return pl.pallas_call(
        functools.partial(_odd_prompt_kernel, tb=tb, nt=nt),
        grid=(n + 1,),
        in_specs=[pl.BlockSpec((tb, d), lambda i: (blk(i), 0)),
                  pl.BlockSpec((1, d), const2),
                  pl.BlockSpec((1, d, n_proj), lambda i: (o, 0, 0)),
                  pl.BlockSpec((1, SMALL_ROWS, d), lambda i: (o, 0, 0)),
                  pl.BlockSpec((1, SMALL_COLS), const2),
                  pl.BlockSpec((SMALL_ROWS, 1), const2),
                  pl.BlockSpec((1, MLSTM_DV), const2),
                  state_c, state_n, state_m],
        out_specs=[pl.BlockSpec((tb, hv), lambda i: (jnp.maximum(i - 1, 0), 0)),
                   state_c, state_n, state_m],
        out_shape=[jax.ShapeDtypeStruct((t, hv), F32),
                   jax.ShapeDtypeStruct(c0.shape, F32),
                   jax.ShapeDtypeStruct(n0.shape, F32),
                   jax.ShapeDtypeStruct(m0.shape, F32)],
        scratch_shapes=[pltpu.VMEM((tb, n_proj), F32), pltpu.VMEM((tb, n_proj), F32),
                        pltpu.VMEM((SMALL_ROWS, tb), F32), pltpu.VMEM((SMALL_ROWS, tb), F32),
                        pltpu.VMEM((N_HEADS, HEAD_DIM, MLSTM_DV), F32),
                        pltpu.VMEM((1, hq), F32),
                        pltpu.VMEM((1, SMALL_COLS), F32)],
        compiler_params=_cparams(("arbitrary",)),
        name="odd_prompt",
    )(x2, g.reshape(1, d), w_main_all, w_small_t_all, gb_row, gb_col, mng.reshape(1, MLSTM_DV), c0, n0, m0)


def _mlstm_sample_kernel(q_ref, k_ref, v_ref, op_ref, sm_ref, smt_ref, gbr_ref, gbc_ref, mng_ref,
                         c0_ref, n0_ref, m0_ref, *rest, ns):
    o_ref, cout_ref, nout_ref, mout_ref = rest[-4:]
    cs = _chunk_structs(CHUNK_ROWS, ns)
    lane = lax.broadcasted_iota(jnp.int32, (1, SMALL_COLS), 1)
    c_list = [c0_ref[0, :, h].reshape(ns * HEAD_DIM, MLSTM_DV) for h in range(N_HEADS)]
    n0 = n0_ref[...]
    m0 = m0_ref[...]
    n_rows = _sel_mm(cs.seq_expand, jnp.concatenate(
        [n0, jnp.zeros((HEAD_DIM - ns, n0.shape[1]), F32)], axis=0))
    m_rows = _sel_mm(cs.seq_expand, jnp.concatenate(
        [m0, jnp.zeros((HEAD_DIM - ns, m0.shape[1]), F32)], axis=0))
    res = _mlstm_chunk(cs, q_ref[...], k_ref[...], v_ref[...], op_ref[...], sm_ref[...], smt_ref[0],
                       gbr_ref[...], gbc_ref[...], mng_ref[...], c_list, n_rows, m_rows)
    m_all = jnp.zeros((CHUNK_ROWS, SMALL_COLS), F32)
    for h, (out, c_new, keep, m_new, kw) in enumerate(res):
        o_ref[:, h * MLSTM_DV:(h + 1) * MLSTM_DV] = out
        cout_ref[0, :, h] = c_new.reshape(ns, HEAD_DIM, MLSTM_DV)
        keep_seq = _sel_mm(cs.seq_last, jnp.broadcast_to(keep, (CHUNK_ROWS, HEAD_DIM)))
        nout_ref[:, h * HEAD_DIM:(h + 1) * HEAD_DIM] = (
            keep_seq * n0[:, h * HEAD_DIM:(h + 1) * HEAD_DIM] + _sel_mm(cs.seq_sum, kw))
        m_all = jnp.where(lane == h, m_new, m_all)
    mout_ref[...] = _sel_mm(cs.seq_last, m_all)


def _mlstm_sample(proj, smt, gb_row, gb_col, mng, c_all, layer, c_out_prev, n0, m0, lt):
    hq = N_HEADS * HEAD_DIM
    hv = N_HEADS * MLSTM_DV
    ns = CHUNK_ROWS // lt
    t = proj.shape[0]
    small_blk = (2 * hq + 2 * hv) // SMALL_COLS
    state_c = pl.BlockSpec((1, ns, N_HEADS, HEAD_DIM, MLSTM_DV), lambda c: (layer, c, 0, 0, 0))
    state_n = pl.BlockSpec((ns, hq), lambda c: (c, 0))
    state_m = pl.BlockSpec((ns, SMALL_COLS), lambda c: (c, 0))
    in_specs = [pl.BlockSpec((CHUNK_ROWS, hq), lambda c: (c, 0)),
                pl.BlockSpec((CHUNK_ROWS, hq), lambda c: (c, 1)),
                pl.BlockSpec((CHUNK_ROWS, hv), lambda c: (c, 2 * hq // hv)),
                pl.BlockSpec((CHUNK_ROWS, hv), lambda c: (c, 2 * hq // hv + 1)),
                pl.BlockSpec((CHUNK_ROWS, SMALL_COLS), lambda c: (c, small_blk)),
                pl.BlockSpec((1, SMALL_ROWS, CHUNK_ROWS), lambda c: (c, 0, 0)),
                pl.BlockSpec((1, SMALL_COLS), lambda c: (0, 0)),
                pl.BlockSpec((SMALL_ROWS, 1), lambda c: (0, 0)),
                pl.BlockSpec((1, MLSTM_DV), lambda c: (0, 0)),
                state_c, state_n, state_m,
                pl.BlockSpec(memory_space=pl.ANY)]
    args = [proj, proj, proj, proj, proj, smt, gb_row, gb_col, mng.reshape(1, MLSTM_DV), c_all, n0, m0,
            c_out_prev]
    return pl.pallas_call(
        functools.partial(_mlstm_sample_kernel, ns=ns),
        grid=(t // CHUNK_ROWS,),
        in_specs=in_specs,
        out_specs=[pl.BlockSpec((CHUNK_ROWS, hv), lambda c: (c, 0)), state_c, state_n, state_m],
        out_shape=[jax.ShapeDtypeStruct((t, hv), F32),
                   jax.ShapeDtypeStruct(c_all.shape, F32),
                   jax.ShapeDtypeStruct(n0.shape, F32),
                   jax.ShapeDtypeStruct(m0.shape, F32)],
        input_output_aliases={len(args) - 1: 1},
        compiler_params=_cparams(("parallel",)),
        name="mlstm_sample",
    )(*args)


def _pad_cols(w, n):
    return jnp.pad(w, ((0, 0), (0, n - w.shape[1])))


def _split_in_weight_kernel(w_ref, main_ref, st_ref, *, n_a, n_small):
    w = w_ref[0]
    rows = w.shape[0]
    n_b = w.shape[1] - n_a - n_small
    main_ref[0, :, 0:n_a] = w[:, 0:n_a].astype(BF16)
    main_ref[0, :, n_a:n_a + n_b] = w[:, n_a + n_small:].astype(BF16)
    small = jnp.concatenate([w[:, n_a:n_a + n_small], jnp.zeros((rows, SMALL_COLS - n_small), F32)],
                            axis=1)
    main_ref[0, :, n_a + n_b:] = small.astype(BF16)
    st_ref[0] = small.T[0:SMALL_ROWS, :].astype(BF16)


def _split_in_weight(w, n_a, n_small):
    layers, d, n_tot = w.shape
    n_main = n_tot - n_small + SMALL_COLS
    rows = 256
    return pl.pallas_call(
        functools.partial(_split_in_weight_kernel, n_a=n_a, n_small=n_small),
        grid=(layers, d // rows),
        in_specs=[pl.BlockSpec((1, rows, n_tot), lambda l, r: (l, r, 0))],
        out_specs=[pl.BlockSpec((1, rows, n_main), lambda l, r: (l, r, 0)),
                   pl.BlockSpec((1, SMALL_ROWS, rows), lambda l, r: (l, 0, r))],
        out_shape=[jax.ShapeDtypeStruct((layers, d, n_main), BF16),
                   jax.ShapeDtypeStruct((layers, SMALL_ROWS, d), BF16)],
        compiler_params=_cparams(("parallel", "parallel")),
        name="split_in_weight",
    )(w)


def _chunk_lanes(smt):
    rows, t = smt.shape
    return smt.reshape(rows, t // CHUNK_ROWS, CHUNK_ROWS).transpose(1, 0, 2)


def _row_col_params(vals):
    row = jnp.stack([jnp.pad(v, (0, SMALL_COLS - v.shape[0])) for v in vals]).astype(F32)
    col = jnp.stack([jnp.pad(v, (0, SMALL_ROWS - v.shape[0])) for v in vals], axis=1).astype(F32)
    return row, col


def _trunk(x, is_prompt, s_gdn, s_conv, s_c, s_n, s_m, norm_g, w_in_even, conv_w, a_log, dt_bias,
           gdn_norm_g, sgu_norm_g, sgu_w, sgu_b, w_out_even, w_in_odd, gate_b_odd, mlstm_norm_g,
           w_out_odd, w_ff1, w_ff2):
    bsz, seq, d = x.shape
    t = bsz * seq
    depth = norm_g.shape[0]
    hq = N_HEADS * HEAD_DIM
    cch = 3 * hq
    tm = 1024 if t % 1024 == 0 else 512
    tb = min(512, seq)
    zeros_h = jnp.zeros((N_HEADS,), F32)
    x2 = x.reshape(t, d)
    new_gdn, new_conv, new_v, new_c, new_n, new_m = [], [], [], [], [], []
    gdn_out = None if is_prompt else jnp.zeros_like(s_gdn)
    c_out = None if is_prompt else jnp.zeros_like(s_c)
    for l in range(depth):
        gl = norm_g[l]
        if l % 2 == 0:
            e = l // 2
            prm_row, prm_col = _row_col_params([jnp.concatenate([zeros_h, jnp.exp(a_log[e])]),
                                                jnp.concatenate([zeros_h, dt_bias[e]])])
            if is_prompt:
                buf8 = jnp.pad(s_conv[e], ((0, 0), (8 - (CONV_W - 1), 0), (0, 0)))
                o_mix, s_new, tail = _even_prompt(
                    x2, gl[0], *w_in_even, e, conv_w[e], prm_row, prm_col, gdn_norm_g[e], s_gdn[e], buf8,
                    sgu_w[e], _pad_cols(sgu_b[e].T, SMALL_COLS), sgu_norm_g[e], bsz, seq, tb)
                buf_new = tail[:, 8 - (CONV_W - 1):, :]
                new_gdn.append(s_new)
                mix_acts, mix_idx = [o_mix], [(e, 0)]
            else:
                proj, smt = _proj_in(x2, gl[0], *w_in_even, e, 512)
                qkv = proj[:, :cch].reshape(bsz, seq, cch)
                xcat = jnp.concatenate([s_conv[e], qkv], axis=1)
                buf_new = xcat[:, seq:, :]
                xc = _conv_sample(xcat.reshape(bsz, (CONV_W - 1 + seq) * cch), conv_w[e], seq, cch)
                o_a, gdn_out = _gdn_sample(xc.reshape(t, cch), proj, _chunk_lanes(smt), prm_row, prm_col,
                                           gdn_norm_g[e], s_gdn, e, gdn_out, seq)
                ns = CHUNK_ROWS // seq
                w_tiles = jnp.tile(sgu_w[e][:, :seq, :seq], (1, ns, ns))
                b_cols = _pad_cols(jnp.tile(sgu_b[e][:, :seq].T, (ns, 1)), SMALL_COLS)
                o_b, vb = _sgu(proj, w_tiles, b_cols, sgu_norm_g[e], CHUNK_ROWS, ns, 1)
                new_v.append(vb.reshape(bsz, seq, hq))
                mix_acts, mix_idx = [o_a, o_b], [(e, 0), (e, 1)]
            new_conv.append(buf_new)
            w_out_all = w_out_even
        else:
            o = l // 2
            gb_row, gb_col = _row_col_params([gate_b_odd[o]])
            if is_prompt:
                n0 = s_n[o].reshape(bsz, 1, hq)
                m0 = _pad_cols(s_m[o], SMALL_COLS).reshape(bsz, 1, SMALL_COLS)
                hh, c_new, n_new, m_new = _odd_prompt(x2, gl[0], *w_in_odd, o, gb_row, gb_col,
                                                      mlstm_norm_g[o], s_c[o], n0, m0, bsz, seq, tb)
                m_new = m_new.reshape(bsz, SMALL_COLS)
                new_c.append(c_new)
            else:
                proj, smt = _proj_in(x2, gl[0], *w_in_odd, o, 512)
                n0 = s_n[o].reshape(bsz, hq)
                m0 = _pad_cols(s_m[o], SMALL_COLS)
                hh, c_out, n_new, m_new = _mlstm_sample(proj, _chunk_lanes(smt), gb_row, gb_col,
                                                        mlstm_norm_g[o], s_c, o, c_out, n0, m0, seq)
            new_n.append(n_new.reshape(bsz, N_HEADS, HEAD_DIM))
            new_m.append(m_new[:, :N_HEADS])
            mix_acts, mix_idx, w_out_all = [hh], [(o, 0)], w_out_odd
        x2 = _mix_ffn(x2, gl[1:4], mix_acts, w_out_all, mix_idx, w_ff1, w_ff2, l, tm, 1024)
    return (x2.reshape(bsz, seq, d),
            jnp.stack(new_gdn) if is_prompt else gdn_out, jnp.stack(new_conv),
            None if is_prompt else jnp.stack(new_v),
            jnp.stack(new_c) if is_prompt else c_out, jnp.stack(new_n), jnp.stack(new_m))


def kernel(x_prompt, x_sample, state_gdn, state_gdn_conv, state_mlstm_c, state_mlstm_n, state_mlstm_m,
           norm_g, w_in_even, conv_w, a_log, dt_bias, gdn_norm_g, sgu_norm_g, sgu_w, sgu_b, w_out_even,
           w_in_odd, gate_b_odd, mlstm_norm_g, w_out_odd, w_ff1, w_ff2):
    hq = N_HEADS * HEAD_DIM
    weights = (norm_g, _split_in_weight(w_in_even, 4 * hq, 2 * N_HEADS), conv_w, a_log, dt_bias,
               gdn_norm_g, sgu_norm_g, sgu_w, sgu_b, w_out_even.astype(BF16),
               _split_in_weight(w_in_odd, 2 * hq + N_HEADS * MLSTM_DV, 2 * N_HEADS), gate_b_odd,
               mlstm_norm_g, w_out_odd.astype(BF16), w_ff1.astype(BF16), w_ff2.astype(BF16))
    bp = x_prompt.shape[0]
    n_even, n_odd = state_gdn.shape[0], state_mlstm_c.shape[0]
    y_prompt, p_gdn, p_conv, _, p_c, p_n, p_m = _trunk(
        x_prompt, True,
        jnp.zeros((n_even, bp) + state_gdn.shape[2:], F32),
        jnp.zeros((n_even, bp) + state_gdn_conv.shape[2:], x_prompt.dtype),
        jnp.zeros((n_odd, bp) + state_mlstm_c.shape[2:], F32),
        jnp.zeros((n_odd, bp) + state_mlstm_n.shape[2:], F32),
        jnp.zeros((n_odd, bp) + state_mlstm_m.shape[2:], F32),
        *weights)
    y_sample, s_gdn, s_conv, s_v, s_c, s_n, s_m = _trunk(
        x_sample, False, state_gdn, state_gdn_conv, state_mlstm_c, state_mlstm_n, state_mlstm_m,
        *weights)
    return (y_prompt, y_sample, p_gdn, s_gdn, p_conv, s_conv, s_v, p_c, s_c, p_n, s_n, p_m, s_m)
```

```python
import functools
import math
from types import SimpleNamespace

import jax
import jax.numpy as jnp
from jax import lax
from jax.experimental import pallas as pl
from jax.experimental.pallas import tpu as pltpu

F32 = jnp.float32
BF16 = jnp.bfloat16

EPS = 1e-6
N_HEADS = 4
HEAD_DIM = 128
MLSTM_DV = 256
CONV_W = 4
CHUNK_ROWS = 64
SGU_CHUNK = 128
SMALL_COLS = 128
SMALL_ROWS = 16
VMEM_LIMIT = 56 * 1024 * 1024


def _cparams(sem):
    return pltpu.CompilerParams(dimension_semantics=sem, vmem_limit_bytes=VMEM_LIMIT)


def _mm(a, b):
    return jnp.dot(a.astype(BF16), b.astype(BF16), preferred_element_type=F32)


def _mm_nt(a, b):
    return lax.dot_general(a.astype(BF16), b.astype(BF16), (((1,), (1,)), ((), ())),
                           preferred_element_type=F32)


def _trunc_bf16(x):
    bits = lax.bitcast_convert_type(x, jnp.uint32) & jnp.uint32(0xFFFF0000)
    return lax.bitcast_convert_type(bits, F32)


def _split3(x):
    hi = _trunc_bf16(x)
    rest = x - hi
    mid = _trunc_bf16(rest)
    return hi, mid, rest - mid


def _sel_mm(sel, x):
    pieces = jnp.concatenate(_split3(x), axis=0).astype(BF16)
    return jnp.dot(jnp.concatenate([sel.astype(BF16)] * 3, axis=1), pieces, preferred_element_type=F32)


def _mm_sel(x, sel):
    pieces = jnp.concatenate(_split3(x), axis=1).astype(BF16)
    return jnp.dot(pieces, jnp.concatenate([sel.astype(BF16)] * 3, axis=0), preferred_element_type=F32)


def _rms(x, g):
    return x * lax.rsqrt(jnp.mean(x * x, axis=-1, keepdims=True) + EPS) * g


def _l2norm(x):
    return x * lax.rsqrt(jnp.sum(x * x, axis=-1, keepdims=True) + EPS)


def _softplus(x):
    return jnp.maximum(x, 0.0) + jnp.log1p(jnp.exp(-jnp.abs(x)))


def _sigmoid(x):
    return 0.5 * jnp.tanh(0.5 * x) + 0.5


def _silu(x):
    return x * _sigmoid(x)


def _gelu(x):
    return 0.5 * x * (1.0 + lax.erf(x * (2.0 ** -0.5)))


def _proj_in_kernel(x_ref, g_ref, w_ref, wst_ref, o_ref, ot_ref):
    _project_block(x_ref, g_ref, w_ref, wst_ref, o_ref, ot_ref)


def _proj_in(x2d, g, w_main_all, w_small_t_all, layer, tm):
    t, d = x2d.shape
    n = w_main_all.shape[2]
    return pl.pallas_call(
        _proj_in_kernel,
        grid=(t // tm,),
        in_specs=[pl.BlockSpec((tm, d), lambda i: (i, 0)),
                  pl.BlockSpec((1, d), lambda i: (0, 0)),
                  pl.BlockSpec((1, d, n), lambda i: (layer, 0, 0)),
                  pl.BlockSpec((1, SMALL_ROWS, d), lambda i: (layer, 0, 0))],
        out_specs=[pl.BlockSpec((tm, n), lambda i: (i, 0)),
                   pl.BlockSpec((SMALL_ROWS, tm), lambda i: (0, i))],
        out_shape=[jax.ShapeDtypeStruct((t, n), F32),
                   jax.ShapeDtypeStruct((SMALL_ROWS, t), F32)],
        compiler_params=_cparams(("parallel",)),
        name="proj_in",
    )(x2d, g.reshape(1, d), w_main_all, w_small_t_all)


def _mix_ffn_kernel(*refs, n_act, nk):
    x_ref, g_ref = refs[0], refs[1]
    acts = refs[2:2 + n_act]
    wos = refs[2 + n_act:2 + 2 * n_act]
    w1_ref, w2_ref, o_ref, h_ref, acc_ref = refs[2 + 2 * n_act:]
    k = pl.program_id(1)
    tm = x_ref.shape[0]
    halves = [slice(0, tm // 2), slice(tm // 2, tm)]

    def prologue(rs):
        mix = None
        for a_ref, w_ref in zip(acts, wos):
            p = jnp.dot(a_ref[rs, :].astype(BF16), w_ref[0], preferred_element_type=F32)
            mix = p if mix is None else mix + p
        x1 = x_ref[rs, :] + _rms(mix, g_ref[0:1, :])
        o_ref[rs, :] = x1
        h_ref[rs, :] = _rms(x1, g_ref[1:2, :]).astype(BF16)

    def chunk(rs, first):
        a = jnp.dot(h_ref[rs, :], w1_ref[0], preferred_element_type=F32)
        a = jnp.square(jnp.maximum(a, 0.0)).astype(BF16)
        part = jnp.dot(a, w2_ref[0], preferred_element_type=F32)
        acc_ref[rs, :] = part if first else acc_ref[rs, :] + part

    def epilogue(rs):
        o_ref[rs, :] = o_ref[rs, :] + _rms(acc_ref[rs, :], g_ref[2:3, :])

    @pl.when(k == 0)
    def _():
        for q in range(4):
            prologue(slice(q * tm // 4, (q + 1) * tm // 4))
        chunk(slice(0, tm), True)
        if nk == 1:
            for rs in halves:
                epilogue(rs)

    @pl.when((k > 0) & (k < nk - 1))
    def _():
        chunk(slice(0, tm), False)

    @pl.when((k > 0) & (k == nk - 1))
    def _():
        for rs in halves:
            chunk(rs, False)
            epilogue(rs)


def _mix_ffn(x2d, g3rows, acts, w_out_all, w_out_idx, w1_all, w2_all, layer, tm, tf):
    t, d = x2d.shape
    ff = w1_all.shape[2]
    n_act = len(acts)
    in_specs = [pl.BlockSpec((tm, d), lambda i, k: (i, 0)), pl.BlockSpec((3, d), lambda i, k: (0, 0))]
    in_specs += [pl.BlockSpec((tm, a.shape[1]), lambda i, k: (i, 0)) for a in acts]
    in_specs += [pl.BlockSpec((1, a.shape[1], d), lambda i, k, li=li, ri=ri: (li, ri, 0))
                 for a, (li, ri) in zip(acts, w_out_idx)]
    in_specs += [pl.BlockSpec((1, d, tf), lambda i, k: (layer, 0, k)),
                 pl.BlockSpec((1, tf, d), lambda i, k: (layer, k, 0))]
    return pl.pallas_call(
        functools.partial(_mix_ffn_kernel, n_act=n_act, nk=ff // tf),
        grid=(t // tm, ff // tf),
        in_specs=in_specs,
        out_specs=pl.BlockSpec((tm, d), lambda i, k: (i, 0)),
        out_shape=jax.ShapeDtypeStruct((t, d), F32),
        scratch_shapes=[pltpu.VMEM((tm, d), BF16), pltpu.VMEM((tm, d), F32)],
        compiler_params=_cparams(("parallel", "arbitrary")),
        name="mix_ffn",
    )(x2d, g3rows, *acts, *([w_out_all] * n_act), w1_all, w2_all)


def _ind(mask):
    return jnp.where(mask, 1.0, 0.0).astype(F32)


def _div_pow2(x, d):
    return jnp.right_shift(x, int(math.log2(d)))


def _chunk_structs(rows, ns):
    lt = rows // ns
    ri = lax.broadcasted_iota(jnp.int32, (rows, rows), 0)
    ci = lax.broadcasted_iota(jnp.int32, (rows, rows), 1)
    cs = SimpleNamespace(rows=rows, ns=ns, lt=lt)
    if ns == 1:
        cs.same = None
        cs.causal = ri >= ci
        cs.strict = ri > ci
        cs.upper = ri <= ci
    else:
        rseq = _div_pow2(ri, lt)
        cseq = _div_pow2(ci, lt)
        cs.same = rseq == cseq
        cs.causal = cs.same & (ri >= ci)
        cs.strict = cs.same & (ri > ci)
        cs.upper = cs.same & (ri <= ci)
        cs.last = _ind(ci == rseq * lt + (lt - 1))
        cs.last_t = _ind(ri == cseq * lt + (lt - 1))
        sr = _div_pow2(lax.broadcasted_iota(jnp.int32, (ns * HEAD_DIM, rows), 0), HEAD_DIM)
        sc = lax.broadcasted_iota(jnp.int32, (ns * HEAD_DIM, rows), 1)
        cs.stack_last = _ind(sc == sr * lt + (lt - 1))
        cs.stack_mask = _ind(_div_pow2(sc, lt) == sr)
        er = _div_pow2(lax.broadcasted_iota(jnp.int32, (rows, ns * HEAD_DIM), 0), lt)
        ec = _div_pow2(lax.broadcasted_iota(jnp.int32, (rows, ns * HEAD_DIM), 1), HEAD_DIM)
        cs.expand_mask = _ind(er == ec)
        kr = lax.broadcasted_iota(jnp.int32, (ns, rows), 0)
        kc = lax.broadcasted_iota(jnp.int32, (ns, rows), 1)
        cs.seq_last = _ind(kc == kr * lt + (lt - 1))
        cs.seq_sum = _ind(_div_pow2(kc, lt) == kr)
        pr = _div_pow2(lax.broadcasted_iota(jnp.int32, (rows, HEAD_DIM), 0), lt)
        pc = lax.broadcasted_iota(jnp.int32, (rows, HEAD_DIM), 1)
        cs.seq_expand = _ind(pr == pc)
    cs.lower_f = _ind(cs.causal)
    cs.upper_f = _ind(cs.upper)
    eye_r = lax.broadcasted_iota(jnp.int32, (HEAD_DIM, HEAD_DIM), 0)
    eye_c = lax.broadcasted_iota(jnp.int32, (HEAD_DIM, HEAD_DIM), 1)
    cs.eye = _ind(eye_r == eye_c)
    cs.eye_rows = _ind(ri == ci)
    return cs


def _last_rows(cs, x):
    if cs.ns == 1:
        return jnp.broadcast_to(x[cs.rows - 1:cs.rows, :], x.shape)
    return _sel_mm(cs.last, x)


def _last_lanes(cs, x):
    if cs.ns == 1:
        return jnp.broadcast_to(x[:, cs.rows - 1:cs.rows], x.shape)
    return _mm_sel(x, cs.last_t)


def _expand_lhs(cs, x):
    if cs.ns == 1:
        return x
    reps = x.shape[-2] // cs.rows
    mask = cs.expand_mask if reps == 1 else jnp.concatenate([cs.expand_mask] * reps, axis=0)
    return jnp.concatenate([x] * cs.ns, axis=-1) * mask


def _stack_t(cs, x):
    xt = _mm_nt(cs.eye, x)
    if cs.ns == 1:
        return xt
    return jnp.concatenate([xt] * cs.ns, axis=0) * cs.stack_mask


def _stack_scalar(cs, col):
    if cs.ns == 1:
        return col[cs.rows - 1:cs.rows, :]
    return _sel_mm(cs.stack_last, jnp.broadcast_to(col, (cs.rows, HEAD_DIM)))[:, 0:1]


_BATCH_NN = (((2,), (1,)), ((0,), (0,)))
_BATCH_NT = (((2,), (2,)), ((0,), (0,)))
_BATCH_TN = (((1,), (1,)), ((0,), (0,)))


def _bmm(a, b, dims=_BATCH_NN):
    return lax.dot_general(a.astype(BF16), b.astype(BF16), dims, preferred_element_type=F32)


def _split_lhs(a):
    hi = a.astype(BF16).astype(F32)
    return jnp.concatenate([hi, a - hi, hi], axis=-1).astype(BF16)


def _split_rhs(b):
    hi = b.astype(BF16).astype(F32)
    return jnp.concatenate([hi, hi, b - hi], axis=-2).astype(BF16)


def _bmm_split(a_cat, b_cat):
    return lax.dot_general(a_cat, b_cat, _BATCH_NN, preferred_element_type=F32)


def _unit_lower_inverse_off(cs, a):
    rows = cs.rows
    n = -a
    steps = int(math.log2(cs.lt)) - 1
    if steps == 0:
        return n
    pt = jnp.concatenate([_bmm_split(_split_lhs(n), _split_rhs(n)), n], axis=-1)
    right = lax.broadcasted_iota(jnp.int32, (rows, 2 * rows), 1) >= rows
    for _ in range(steps):
        hi = pt.astype(BF16).astype(F32)
        lo = pt - hi
        p_cat = jnp.concatenate([hi[..., :rows], lo[..., :rows], hi[..., :rows]], axis=-1).astype(BF16)
        pt_cat = jnp.concatenate([hi, hi, lo], axis=-2).astype(BF16)
        p_both = jnp.concatenate([pt[..., :rows], pt[..., :rows]], axis=-1)
        pt = _bmm_split(p_cat, pt_cat) + jnp.where(right, pt + p_both, 0.0)
    return pt[..., rows:]


def _gdn_block(cs, nchunks, xc, gate, sm, smt, prm_row, prm_col, gng, s_heads, write_out, pieces=None):
    rows = cs.rows
    hq = N_HEADS * HEAD_DIM
    pieces = pieces or _Pieces()
    pairs = [(c, h) for c in range(nchunks) for h in range(N_HEADS)]
    rs = lambda c: slice(c * rows, (c + 1) * rows)

    def heads(x, col0):
        return jnp.stack([x[rs(c), col0 + h * HEAD_DIM:col0 + (h + 1) * HEAD_DIM] for c, h in pairs])

    def cols(x, lane0):
        return jnp.stack([x[rs(c), lane0 + h:lane0 + h + 1] for c, h in pairs])

    beta_all = _sigmoid(sm)
    g_c = -prm_row[0:1, :] * _softplus(sm + prm_row[1:2, :])
    g_r = -prm_col[:, 0:1] * _softplus(smt + prm_col[:, 1:2])
    gc_c = jnp.concatenate([_sel_mm(cs.lower_f, g_c[rs(c)]) for c in range(nchunks)], axis=0)
    gc_r = jnp.concatenate([_mm_sel(g_r[:, rs(c)], cs.upper_f) for c in range(nchunks)], axis=1)
    gl_c = jnp.concatenate([_last_rows(cs, gc_c[rs(c)]) for c in range(nchunks)], axis=0)
    egc = jnp.exp(gc_c)
    ekd = jnp.exp(gl_c - gc_c)
    pieces.take()

    q = _l2norm(heads(xc, 0)) * (HEAD_DIM ** -0.5)
    k = _l2norm(heads(xc, hq))
    v = heads(xc, 2 * hq)
    beta = cols(beta_all, 0)
    gcol = cols(gc_c, N_HEADS)
    eg = cols(egc, N_HEADS)
    grow = jnp.stack([gc_r[N_HEADS + h:N_HEADS + h + 1, rs(c)] for c, h in pairs])
    decay = jnp.where(cs.causal, jnp.exp(jnp.where(cs.causal, gcol - grow, 0.0)), 0.0)
    kb = k * beta
    a = jnp.where(cs.strict, _bmm(kb, k, _BATCH_NT) * decay, 0.0)
    toff = _unit_lower_inverse_off(cs, a)
    rhs = jnp.concatenate([v * beta, kb * eg], axis=-1)
    sol = rhs + _bmm_split(_split_lhs(toff), _split_rhs(rhs))
    u_val, w_k = sol[..., :HEAD_DIM], sol[..., HEAD_DIM:]
    pieces.take()
    qk = jnp.where(cs.causal, _bmm(q, k, _BATCH_NT) * decay, 0.0)
    q_dec = q * eg
    k_dec = k * cols(ekd, N_HEADS)

    s = s_heads
    for c in range(nchunks):
        pieces.take()
        hs = slice(c * N_HEADS, (c + 1) * N_HEADS)
        p = _bmm(_expand_lhs(cs, jnp.concatenate([w_k[hs], q_dec[hs]], axis=1)), s)
        v_new = u_val[hs] - p[:, :rows]
        o = p[:, rows:] + _bmm(qk[hs], v_new)
        gl_cols = [gl_c[rs(c), N_HEADS + h:N_HEADS + h + 1] for h in range(N_HEADS)]
        g_last = jnp.exp(jnp.stack([_stack_scalar(cs, col) for col in gl_cols]))
        if cs.ns == 1:
            upd = _bmm(k_dec[hs], v_new, _BATCH_TN)
        else:
            kt = _bmm(jnp.broadcast_to(cs.eye, (N_HEADS, HEAD_DIM, HEAD_DIM)), k_dec[hs], _BATCH_NT)
            upd = _bmm(jnp.concatenate([kt] * cs.ns, axis=1) * cs.stack_mask, v_new)
        s = s * g_last + upd
        for h in range(N_HEADS):
            write_out(c, h, _rms(o[h], gng) * _silu(gate[rs(c), h * HEAD_DIM:(h + 1) * HEAD_DIM]))
    return s


PROJ_PIECE_COLS = 256


class _Pieces:
    def __init__(self, thunks=()):
        self._thunks = list(thunks)

    def take(self, count=1):
        for _ in range(count):
            if self._thunks:
                self._thunks.pop(0)()

    def flush(self):
        self.take(len(self._thunks))


def _project_pieces(x_ref, g_ref, w_ref, wst_ref, proj_w, smt_w):
    hn = _rms(x_ref[...], g_ref[...]).astype(BF16)
    n = proj_w.shape[1]

    def piece(lo, hi):
        def run():
            proj_w[:, lo:hi] = jnp.dot(hn, w_ref[0, :, lo:hi], preferred_element_type=F32)
        return run

    def gates_t():
        smt_w[...] = lax.dot_general(wst_ref[0], hn, (((1,), (1,)), ((), ())), preferred_element_type=F32)

    cols = range(0, n, PROJ_PIECE_COLS)
    return _Pieces([gates_t] + [piece(lo, min(lo + PROJ_PIECE_COLS, n)) for lo in cols])


def _project_block(x_ref, g_ref, w_ref, wst_ref, proj_w, smt_w):
    _project_pieces(x_ref, g_ref, w_ref, wst_ref, proj_w, smt_w).flush()


def _pipelined_steps(step):
    i = pl.program_id(0)
    for parity in range(2):
        @pl.when(lax.rem(i, 2) == parity)
        def _():
            step(parity)


def _even_prompt_kernel(x_ref, g_ref, w_ref, wst_ref, cw_ref, prow_ref, pcol_ref, gng_ref, s0_ref, buf_ref,
                        sw_ref, sb_ref, sg_ref, o_ref, sout_ref, bufout_ref,
                        proj_a, proj_b, smt_a, smt_b, s_scr, xp_scr, *, tb, nt):
    i = pl.program_id(0)
    t = lax.rem(jnp.maximum(i - 1, 0), nt)
    hq = N_HEADS * HEAD_DIM
    cch = 3 * hq
    cs = _chunk_structs(CHUNK_ROWS, 1)

    @pl.when(i == 0)
    def _():
        proj_b[...] = jnp.zeros_like(proj_b)
        smt_b[...] = jnp.zeros_like(smt_b)

    @pl.when(t == 0)
    def _():
        s_scr[...] = s0_ref[0]
        xp_scr[...] = buf_ref[0]

    def write_delta(c, h, o):
        o_ref[c * CHUNK_ROWS:(c + 1) * CHUNK_ROWS, h * HEAD_DIM:(h + 1) * HEAD_DIM] = o

    def write_gating(rs, g, val):
        o_ref[rs, hq + g * HEAD_DIM:hq + (g + 1) * HEAD_DIM] = val

    def step(parity):
        proj_w, smt_w, proj_r, smt_r = ((proj_a, smt_a, proj_b, smt_b) if parity == 0
                                        else (proj_b, smt_b, proj_a, smt_a))
        pieces = _project_pieces(x_ref, g_ref, w_ref, wst_ref, proj_w, smt_w)
        pieces.take(3)
        x = proj_r[:, 0:cch]
        xp = jnp.concatenate([xp_scr[...], x], axis=0)
        y = cw_ref[CONV_W - 1:CONV_W, :] * x
        for back in range(1, CONV_W):
            y = y + cw_ref[CONV_W - 1 - back:CONV_W - back, :] * pltpu.roll(xp, back, axis=0)[8:]
        xp_scr[...] = x[tb - 8:]
        pieces.take(2)
        s_scr[...] = _gdn_block(cs, tb // CHUNK_ROWS, _silu(y), proj_r[:, cch:cch + hq],
                                proj_r[:, cch + 3 * hq:cch + 3 * hq + SMALL_COLS], smt_r[...],
                                prow_ref[...], pcol_ref[...], gng_ref[...], s_scr[...], write_delta, pieces)
        pieces.flush()
        _sgu_apply(proj_r[:, cch + hq:cch + 2 * hq], proj_r[:, cch + 2 * hq:cch + 3 * hq],
                   sw_ref, sb_ref[...], sg_ref[...], SGU_CHUNK, 1, write_gating, None)

    _pipelined_steps(step)

    @pl.when(t == nt - 1)
    def _():
        sout_ref[0] = s_scr[...]
        bufout_ref[0] = xp_scr[...]


def _even_prompt(x2, g, w_main_all, w_small_t_all, e, conv_w, prm_row, prm_col, gng, s0, buf8,
                 sgu_w, sgu_b_cols, sgu_g, bsz, seq, tb):
    t, d = x2.shape
    hq = N_HEADS * HEAD_DIM
    cch = 3 * hq
    n_proj = w_main_all.shape[2]
    n = t // tb
    nt = seq // tb
    blk = lambda i: jnp.minimum(i, n - 1)
    seq_of = lambda i: jnp.maximum(i - 1, 0) // nt
    const2 = lambda i: (0, 0)
    return pl.pallas_call(
        functools.partial(_even_prompt_kernel, tb=tb, nt=nt),
        grid=(n + 1,),
        in_specs=[pl.BlockSpec((tb, d), lambda i: (blk(i), 0)),
                  pl.BlockSpec((1, d), const2),
                  pl.BlockSpec((1, d, n_proj), lambda i: (e, 0, 0)),
                  pl.BlockSpec((1, SMALL_ROWS, d), lambda i: (e, 0, 0)),
                  pl.BlockSpec((CONV_W, cch), const2),
                  pl.BlockSpec((2, SMALL_COLS), const2),
                  pl.BlockSpec((SMALL_ROWS, 2), const2),
                  pl.BlockSpec((1, HEAD_DIM), const2),
                  pl.BlockSpec((1, N_HEADS, HEAD_DIM, HEAD_DIM), lambda i: (seq_of(i), 0, 0, 0)),
                  pl.BlockSpec((1, 8, cch), lambda i: (seq_of(i), 0, 0)),
                  pl.BlockSpec((N_HEADS, SGU_CHUNK, SGU_CHUNK), lambda i: (0, 0, 0)),
                  pl.BlockSpec((SGU_CHUNK, SMALL_COLS), const2),
                  pl.BlockSpec((N_HEADS, HEAD_DIM), const2)],
        out_specs=[pl.BlockSpec((tb, 2 * hq), lambda i: (jnp.maximum(i - 1, 0), 0)),
                   pl.BlockSpec((1, N_HEADS, HEAD_DIM, HEAD_DIM), lambda i: (seq_of(i), 0, 0, 0)),
                   pl.BlockSpec((1, 8, cch), lambda i: (seq_of(i), 0, 0))],
        out_shape=[jax.ShapeDtypeStruct((t, 2 * hq), F32),
                   jax.ShapeDtypeStruct((bsz, N_HEADS, HEAD_DIM, HEAD_DIM), F32),
                   jax.ShapeDtypeStruct((bsz, 8, cch), F32)],
        scratch_shapes=[pltpu.VMEM((tb, n_proj), F32), pltpu.VMEM((tb, n_proj), F32),
                        pltpu.VMEM((SMALL_ROWS, tb), F32), pltpu.VMEM((SMALL_ROWS, tb), F32),
                        pltpu.VMEM((N_HEADS, HEAD_DIM, HEAD_DIM), F32),
                        pltpu.VMEM((8, cch), F32)],
        compiler_params=_cparams(("arbitrary",)),
        name="even_prompt",
    )(x2, g.reshape(1, d), w_main_all, w_small_t_all, conv_w, prm_row, prm_col,
      gng.reshape(1, HEAD_DIM), s0, buf8, sgu_w, sgu_b_cols, sgu_g)


def _conv_sample_kernel(xcat_ref, cw_ref, o_ref, *, lt, cch):
    for t in range(lt):
        y = None
        for j in range(CONV_W):
            term = cw_ref[j:j + 1, :] * xcat_ref[:, (t + j) * cch:(t + j + 1) * cch]
            y = term if y is None else y + term
        o_ref[:, t * cch:(t + 1) * cch] = _silu(y)


def _conv_sample(xcat, conv_w, lt, cch):
    nseq = xcat.shape[0]
    return pl.pallas_call(
        functools.partial(_conv_sample_kernel, lt=lt, cch=cch),
        grid=(1,),
        in_specs=[pl.BlockSpec(xcat.shape, lambda i: (0, 0)),
                  pl.BlockSpec((CONV_W, cch), lambda i: (0, 0))],
        out_specs=pl.BlockSpec((nseq, lt * cch), lambda i: (0, 0)),
        out_shape=jax.ShapeDtypeStruct((nseq, lt * cch), F32),
        compiler_params=_cparams(("arbitrary",)),
        name="conv_sample",
    )(xcat, conv_w)


def _gdn_sample_kernel(xc_ref, gate_ref, sm_ref, smt_ref, prow_ref, pcol_ref, gng_ref, s0_ref,
                       *rest, ns):
    o_ref, sout_ref = rest[-2:]
    cs = _chunk_structs(CHUNK_ROWS, ns)
    s_heads = jnp.stack([s0_ref[0, :, h].reshape(ns * HEAD_DIM, HEAD_DIM) for h in range(N_HEADS)])

    def write_out(c, h, o):
        o_ref[:, h * HEAD_DIM:(h + 1) * HEAD_DIM] = o

    s_new = _gdn_block(cs, 1, xc_ref[...], gate_ref[...], sm_ref[...], smt_ref[0],
                       prow_ref[...], pcol_ref[...], gng_ref[...], s_heads, write_out)
    for h in range(N_HEADS):
        sout_ref[0, :, h] = s_new[h].reshape(ns, HEAD_DIM, HEAD_DIM)


def _gdn_sample(xc, proj, smt, prm_row, prm_col, gng, s_all, layer, s_out_prev, lt):
    hq = N_HEADS * HEAD_DIM
    cch = 3 * hq
    ns = CHUNK_ROWS // lt
    nchunk = xc.shape[0] // CHUNK_ROWS
    small_blk = (cch + 3 * hq) // SMALL_COLS
    state_spec = pl.BlockSpec((1, ns, N_HEADS, HEAD_DIM, HEAD_DIM), lambda c: (layer, c, 0, 0, 0))
    in_specs = [pl.BlockSpec((CHUNK_ROWS, cch), lambda c: (c, 0)),
                pl.BlockSpec((CHUNK_ROWS, hq), lambda c: (c, cch // hq)),
                pl.BlockSpec((CHUNK_ROWS, SMALL_COLS), lambda c: (c, small_blk)),
                pl.BlockSpec((1, SMALL_ROWS, CHUNK_ROWS), lambda c: (c, 0, 0)),
                pl.BlockSpec((2, SMALL_COLS), lambda c: (0, 0)),
                pl.BlockSpec((SMALL_ROWS, 2), lambda c: (0, 0)),
                pl.BlockSpec((1, HEAD_DIM), lambda c: (0, 0)),
                state_spec,
                pl.BlockSpec(memory_space=pl.ANY)]
    args = [xc, proj, proj, smt, prm_row, prm_col, gng.reshape(1, HEAD_DIM), s_all, s_out_prev]
    return pl.pallas_call(
        functools.partial(_gdn_sample_kernel, ns=ns),
        grid=(nchunk,),
        in_specs=in_specs,
        out_specs=[pl.BlockSpec((CHUNK_ROWS, hq), lambda c: (c, 0)), state_spec],
        out_shape=[jax.ShapeDtypeStruct((xc.shape[0], hq), F32),
                   jax.ShapeDtypeStruct(s_all.shape, F32)],
        input_output_aliases={len(args) - 1: 1},
        compiler_params=_cparams(("parallel",)),
        name="gdn_sample",
    )(*args)


def _sgu_apply(u_pre, v_pre, w_ref, b_cols, g_rows, rows, ns, write_o, write_vb):
    lt = rows // ns
    ri = lax.broadcasted_iota(jnp.int32, (rows, rows), 0)
    ci = lax.broadcasted_iota(jnp.int32, (rows, rows), 1)
    keep = ri >= ci
    if ns > 1:
        keep = keep & (_div_pow2(ri, lt) == _div_pow2(ci, lt))
    for g in range(N_HEADS):
        lo = g * HEAD_DIM
        w = jnp.where(keep, w_ref[g], 0.0).astype(BF16)
        u = _gelu(u_pre[:, lo:lo + HEAD_DIM])
        vb = _rms(_gelu(v_pre[:, lo:lo + HEAD_DIM]), g_rows[g:g + 1, :])
        if write_vb is not None:
            write_vb(g, vb)
        for c in range(u_pre.shape[0] // rows):
            rs = slice(c * rows, (c + 1) * rows)
            write_o(rs, g, u[rs] * (_mm(w, vb[rs]) + b_cols[:, g:g + 1]))


def _sgu_kernel(u_ref, v_ref, w_ref, b_ref, g_ref, o_ref, vb_ref, *, rows, ns):
    def write_o(rs, g, val):
        o_ref[rs, g * HEAD_DIM:(g + 1) * HEAD_DIM] = val

    def write_vb(g, val):
        vb_ref[:, g * HEAD_DIM:(g + 1) * HEAD_DIM] = val

    _sgu_apply(u_ref[...], v_ref[...], w_ref, b_ref[...], g_ref[...], rows, ns, write_o, write_vb)


def _sgu(proj, w_tiles, b_cols, norm_g, rows, ns, nchunks):
    hq = N_HEADS * HEAD_DIM
    t = proj.shape[0]
    u_blk = (3 * hq + hq) // hq
    blk = rows * nchunks
    return pl.pallas_call(
        functools.partial(_sgu_kernel, rows=rows, ns=ns),
        grid=(t // blk,),
        in_specs=[pl.BlockSpec((blk, hq), lambda i: (i, u_blk)),
                  pl.BlockSpec((blk, hq), lambda i: (i, u_blk + 1)),
                  pl.BlockSpec((N_HEADS, rows, rows), lambda i: (0, 0, 0)),
                  pl.BlockSpec((rows, SMALL_COLS), lambda i: (0, 0)),
                  pl.BlockSpec((N_HEADS, HEAD_DIM), lambda i: (0, 0))],
        out_specs=[pl.BlockSpec((blk, hq), lambda i: (i, 0)),
                   pl.BlockSpec((blk, hq), lambda i: (i, 0))],
        out_shape=[jax.ShapeDtypeStruct((t, hq), F32), jax.ShapeDtypeStruct((t, hq), F32)],
        compiler_params=_cparams(("parallel",)),
        name="sgu",
    )(proj, proj, w_tiles, b_cols, norm_g)


def _logsigmoid(x):
    return jnp.minimum(x, 0.0) - jnp.log1p(jnp.exp(-jnp.abs(x)))


def _mlstm_chunk(cs, q_all, k_all, v_all, opre, sm, smt, gb_row, gb_col, mng, c_list, n_rows, m_rows):
    neg_inf = -jnp.inf
    pre_c = sm + gb_row
    pre_r = smt + gb_col
    b_c = _sel_mm(cs.lower_f, _logsigmoid(pre_c))
    b_r = _mm_sel(_logsigmoid(pre_r), cs.upper_f)
    bl_c = _last_rows(cs, b_c)
    bl_r = _last_lanes(cs, b_r)
    res = []
    for h in range(N_HEADS):
        lo = h * HEAD_DIM
        vo = h * MLSTM_DV
        q = q_all[:, lo:lo + HEAD_DIM] * (HEAD_DIM ** -0.5)
        k = k_all[:, lo:lo + HEAD_DIM]
        v = v_all[:, vo:vo + MLSTM_DV]
        bcol = b_c[:, N_HEADS + h:N_HEADS + h + 1]
        brow = b_r[N_HEADS + h:N_HEADS + h + 1, :]
        icol = pre_c[:, h:h + 1]
        irow = pre_r[h:h + 1, :]
        blcol = bl_c[:, N_HEADS + h:N_HEADS + h + 1]
        blrow = bl_r[N_HEADS + h:N_HEADS + h + 1, :]
        mrow = m_rows[:, h:h + 1]
        inter = bcol + mrow
        dmat = jnp.where(cs.causal, bcol - brow + irow, neg_inf)
        m_t = jnp.maximum(inter, jnp.max(dmat, axis=-1, keepdims=True))
        w_intra = jnp.exp(dmat - m_t)
        w_inter = jnp.exp(inter - m_t)
        s = _mm_nt(q, k) * w_intra
        c_old = c_list[h]
        num = w_inter * _mm(_expand_lhs(cs, q), c_old) + _mm(s, v)
        qn = jnp.sum(q * n_rows[:, lo:lo + HEAD_DIM], axis=-1, keepdims=True)
        den = w_inter * qn + jnp.sum(s, axis=-1, keepdims=True)
        hval = num / jnp.maximum(jnp.abs(den), jnp.exp(-m_t))
        out = _rms(hval, mng) * _sigmoid(opre[:, vo:vo + MLSTM_DV])
        logw_row = blrow - brow + irow
        if cs.ns == 1:
            seq_max = jnp.max(logw_row, axis=-1, keepdims=True)
        else:
            seq_max = jnp.max(jnp.where(cs.same, logw_row, neg_inf), axis=-1, keepdims=True)
        m_new = jnp.maximum(blcol + mrow, seq_max)
        keep = jnp.exp(blcol + mrow - m_new)
        wk = jnp.exp(blcol - bcol + icol - m_new)
        kw = k * wk
        c_new = c_old * _stack_scalar(cs, keep) + _mm(_stack_t(cs, kw), v)
        res.append((out, c_new, keep, m_new, kw))
    return res


def _mlstm_prompt_block(cs, nchunks, q_all, k_all, v_all, opre, sm, smt, gb_row, gb_col, mng,
                        c_heads, n_heads, m_heads, write_out, pieces):
    rows = cs.rows
    pairs = [(c, h) for c in range(nchunks) for h in range(N_HEADS)]
    rs = lambda c: slice(c * rows, (c + 1) * rows)

    def heads(x, width):
        return jnp.stack([x[rs(c), h * width:(h + 1) * width] for c, h in pairs])

    def cols(x, lane0):
        return jnp.stack([x[rs(c), lane0 + h:lane0 + h + 1] for c, h in pairs])

    def lanes(x, row0):
        return jnp.stack([x[row0 + h:row0 + h + 1, rs(c)] for c, h in pairs])

    pre_c = sm + gb_row
    pre_r = smt + gb_col
    lf_c = _logsigmoid(pre_c)
    lf_r = _logsigmoid(pre_r)
    b_c = jnp.concatenate([_sel_mm(cs.lower_f, lf_c[rs(c)]) for c in range(nchunks)], axis=0)
    b_r = jnp.concatenate([_mm_sel(lf_r[:, rs(c)], cs.upper_f) for c in range(nchunks)], axis=1)
    pieces.take(3)

    q = heads(q_all, HEAD_DIM) * (HEAD_DIM ** -0.5)
    k = heads(k_all, HEAD_DIM)
    v = heads(v_all, MLSTM_DV)
    bcol = cols(b_c, N_HEADS)
    icol = cols(pre_c, 0)
    brow = lanes(b_r, N_HEADS)
    irow = lanes(pre_r, 0)
    blast = bcol[:, rows - 1:rows, :]
    dmat = jnp.where(cs.causal, bcol - brow + irow, -jnp.inf)
    rmax = jnp.max(dmat, axis=-1, keepdims=True)
    sp = _bmm(q, k, _BATCH_NT) * jnp.exp(dmat - rmax)
    pieces.take(3)
    sv = _bmm(sp, v)
    ssum = jnp.sum(sp, axis=-1, keepdims=True)
    pieces.take()
    lmax = jnp.max(blast - brow + irow, axis=-1, keepdims=True)
    kwp = k * jnp.exp(blast - bcol + icol - lmax)
    upd = _bmm(kwp, v, _BATCH_TN)
    nsum = jnp.sum(kwp, axis=1, keepdims=True)
    pieces.take()

    cst, n, m = c_heads, n_heads, m_heads
    for c in range(nchunks):
        pieces.take()
        hs = slice(c * N_HEADS, (c + 1) * N_HEADS)
        inter = bcol[hs] + m
        m_t = jnp.maximum(inter, rmax[hs])
        w_inter = jnp.exp(inter - m_t)
        w_intra = jnp.exp(rmax[hs] - m_t)
        num = w_inter * _bmm(q[hs], cst) + w_intra * sv[hs]
        den = (w_inter * jnp.sum(q[hs] * n, axis=-1, keepdims=True) + w_intra * ssum[hs])
        hval = num / jnp.maximum(jnp.abs(den), jnp.exp(-m_t))
        for h in range(N_HEADS):
            write_out(c, h, _rms(hval[h], mng) * _sigmoid(opre[rs(c), h * MLSTM_DV:(h + 1) * MLSTM_DV]))
        m_new = jnp.maximum(blast[hs] + m, lmax[hs])
        keep = jnp.exp(blast[hs] + m - m_new)
        fresh = jnp.exp(lmax[hs] - m_new)
        cst = keep * cst + fresh * upd[hs]
        n = keep * n + fresh * nsum[hs]
        m = m_new
    return cst, n, m


def _odd_prompt_kernel(x_ref, g_ref, w_ref, wst_ref, gbr_ref, gbc_ref, mng_ref, c0_ref, n0_ref, m0_ref,
                       o_ref, cout_ref, nout_ref, mout_ref,
                       proj_a, proj_b, smt_a, smt_b, c_scr, n_scr, m_scr, *, tb, nt):
    i = pl.program_id(0)
    t = lax.rem(jnp.maximum(i - 1, 0), nt)
    hq = N_HEADS * HEAD_DIM
    hv = N_HEADS * MLSTM_DV
    cs = _chunk_structs(CHUNK_ROWS, 1)
    lane = lax.broadcasted_iota(jnp.int32, (1, SMALL_COLS), 1)

    @pl.when(i == 0)
    def _():
        proj_b[...] = jnp.zeros_like(proj_b)
        smt_b[...] = jnp.zeros_like(smt_b)

    @pl.when(t == 0)
    def _():
        c_scr[...] = c0_ref[0]
        n_scr[...] = n0_ref[0]
        m_scr[...] = m0_ref[0]

    def write_out(c, h, o):
        o_ref[c * CHUNK_ROWS:(c + 1) * CHUNK_ROWS, h * MLSTM_DV:(h + 1) * MLSTM_DV] = o

    def step(parity):
        proj_w, smt_w, proj_r, smt_r = ((proj_a, smt_a, proj_b, smt_b) if parity == 0
                                        else (proj_b, smt_b, proj_a, smt_a))
        pieces = _project_pieces(x_ref, g_ref, w_ref, wst_ref, proj_w, smt_w)
        pieces.take(2)
        n_cur = n_scr[...]
        m_cur = m_scr[...]
        n_heads = jnp.stack([n_cur[:, h * HEAD_DIM:(h + 1) * HEAD_DIM] for h in range(N_HEADS)])
        m_heads = jnp.stack([m_cur[:, h:h + 1] for h in range(N_HEADS)])
        c_new, n_heads, m_heads = _mlstm_prompt_block(
            cs, tb // CHUNK_ROWS, proj_r[:, 0:hq], proj_r[:, hq:2 * hq], proj_r[:, 2 * hq:2 * hq + hv],
            proj_r[:, 2 * hq + hv:2 * hq + 2 * hv],
            proj_r[:, 2 * hq + 2 * hv:2 * hq + 2 * hv + SMALL_COLS], smt_r[...],
            gbr_ref[...], gbc_ref[...], mng_ref[...], c_scr[...], n_heads, m_heads, write_out, pieces)
        pieces.flush()
        c_scr[...] = c_new
        n_scr[...] = jnp.concatenate([n_heads[h] for h in range(N_HEADS)], axis=1)
        for h in range(N_HEADS):
            m_cur = jnp.where(lane == h, m_heads[h], m_cur)
        m_scr[...] = m_cur

    _pipelined_steps(step)

    @pl.when(t == nt - 1)
    def _():
        cout_ref[0] = c_scr[...]
        nout_ref[0] = n_scr[...]
        mout_ref[0] = m_scr[...]


def _odd_prompt(x2, g, w_main_all, w_small_t_all, o, gb_row, gb_col, mng, c0, n0, m0, bsz, seq, tb):
    t, d = x2.shape
    hq = N_HEADS * HEAD_DIM
    hv = N_HEADS * MLSTM_DV
    n_proj = w_main_all.shape[2]
    n = t // tb
    nt = seq // tb
    blk = lambda i: jnp.minimum(i, n - 1)
    seq_of = lambda i: jnp.maximum(i - 1, 0) // nt
    const2 = lambda i: (0, 0)
    state_c = pl.BlockSpec((1, N_HEADS, HEAD_DIM, MLSTM_DV), lambda i: (seq_of(i), 0, 0, 0))
    state_n = pl.BlockSpec((1, 1, hq), lambda i: (seq_of(i), 0, 0))
    state_m = pl.BlockSpec((1, 1, SMALL_COLS), lambda i: (seq_of(i), 0, 0))
    return pl.pallas_call(
        functools.partial(_odd_prompt_kernel, tb=tb, nt=nt),
        grid=(n + 1,),
        in_specs=[pl.BlockSpec((tb, d), lambda i: (blk(i), 0)),
                  pl.BlockSpec((1, d), const2),
                  pl.BlockSpec((1, d, n_proj), lambda i: (o, 0, 0)),
                  pl.BlockSpec((1, SMALL_ROWS, d), lambda i: (o, 0, 0)),
                  pl.BlockSpec((1, SMALL_COLS), const2),
                  pl.BlockSpec((SMALL_ROWS, 1), const2),
                  pl.BlockSpec((1, MLSTM_DV), const2),
                  state_c, state_n, state_m],
        out_specs=[pl.BlockSpec((tb, hv), lambda i: (jnp.maximum(i - 1, 0), 0)),
                   state_c, state_n, state_m],
        out_shape=[jax.ShapeDtypeStruct((t, hv), F32),
                   jax.ShapeDtypeStruct(c0.shape, F32),
                   jax.ShapeDtypeStruct(n0.shape, F32),
                   jax.ShapeDtypeStruct(m0.shape, F32)],
        scratch_shapes=[pltpu.VMEM((tb, n_proj), F32), pltpu.VMEM((tb, n_proj), F32),
                        pltpu.VMEM((SMALL_ROWS, tb), F32), pltpu.VMEM((SMALL_ROWS, tb), F32),
                        pltpu.VMEM((N_HEADS, HEAD_DIM, MLSTM_DV), F32),
                        pltpu.VMEM((1, hq), F32),
                        pltpu.VMEM((1, SMALL_COLS), F32)],
        compiler_params=_cparams(("arbitrary",)),
        name="odd_prompt",
    )(x2, g.reshape(1, d), w_main_all, w_small_t_all, gb_row, gb_col, mng.reshape(1, MLSTM_DV), c0, n0, m0)


def _mlstm_sample_kernel(q_ref, k_ref, v_ref, op_ref, sm_ref, smt_ref, gbr_ref, gbc_ref, mng_ref,
                         c0_ref, n0_ref, m0_ref, *rest, ns):
    o_ref, cout_ref, nout_ref, mout_ref = rest[-4:]
    cs = _chunk_structs(CHUNK_ROWS, ns)
    lane = lax.broadcasted_iota(jnp.int32, (1, SMALL_COLS), 1)
    c_list = [c0_ref[0, :, h].reshape(ns * HEAD_DIM, MLSTM_DV) for h in range(N_HEADS)]
    n0 = n0_ref[...]
    m0 = m0_ref[...]
    n_rows = _sel_mm(cs.seq_expand, jnp.concatenate(
        [n0, jnp.zeros((HEAD_DIM - ns, n0.shape[1]), F32)], axis=0))
    m_rows = _sel_mm(cs.seq_expand, jnp.concatenate(
        [m0, jnp.zeros((HEAD_DIM - ns, m0.shape[1]), F32)], axis=0))
    res = _mlstm_chunk(cs, q_ref[...], k_ref[...], v_ref[...], op_ref[...], sm_ref[...], smt_ref[0],
                       gbr_ref[...], gbc_ref[...], mng_ref[...], c_list, n_rows, m_rows)
    m_all = jnp.zeros((CHUNK_ROWS, SMALL_COLS), F32)
    for h, (out, c_new, keep, m_new, kw) in enumerate(res):
        o_ref[:, h * MLSTM_DV:(h + 1) * MLSTM_DV] = out
        cout_ref[0, :, h] = c_new.reshape(ns, HEAD_DIM, MLSTM_DV)
        keep_seq = _sel_mm(cs.seq_last, jnp.broadcast_to(keep, (CHUNK_ROWS, HEAD_DIM)))
        nout_ref[:, h * HEAD_DIM:(h + 1) * HEAD_DIM] = (
            keep_seq * n0[:, h * HEAD_DIM:(h + 1) * HEAD_DIM] + _sel_mm(cs.seq_sum, kw))
        m_all = jnp.where(lane == h, m_new, m_all)
    mout_ref[...] = _sel_mm(cs.seq_last, m_all)


def _mlstm_sample(proj, smt, gb_row, gb_col, mng, c_all, layer, c_out_prev, n0, m0, lt):
    hq = N_HEADS * HEAD_DIM
    hv = N_HEADS * MLSTM_DV
    ns = CHUNK_ROWS // lt
    t = proj.shape[0]
    small_blk = (2 * hq + 2 * hv) // SMALL_COLS
    state_c = pl.BlockSpec((1, ns, N_HEADS, HEAD_DIM, MLSTM_DV), lambda c: (layer, c, 0, 0, 0))
    state_n = pl.BlockSpec((ns, hq), lambda c: (c, 0))
    state_m = pl.BlockSpec((ns, SMALL_COLS), lambda c: (c, 0))
    in_specs = [pl.BlockSpec((CHUNK_ROWS, hq), lambda c: (c, 0)),
                pl.BlockSpec((CHUNK_ROWS, hq), lambda c: (c, 1)),
                pl.BlockSpec((CHUNK_ROWS, hv), lambda c: (c, 2 * hq // hv)),
                pl.BlockSpec((CHUNK_ROWS, hv), lambda c: (c, 2 * hq // hv + 1)),
                pl.BlockSpec((CHUNK_ROWS, SMALL_COLS), lambda c: (c, small_blk)),
                pl.BlockSpec((1, SMALL_ROWS, CHUNK_ROWS), lambda c: (c, 0, 0)),
                pl.BlockSpec((1, SMALL_COLS), lambda c: (0, 0)),
                pl.BlockSpec((SMALL_ROWS, 1), lambda c: (0, 0)),
                pl.BlockSpec((1, MLSTM_DV), lambda c: (0, 0)),
                state_c, state_n, state_m,
                pl.BlockSpec(memory_space=pl.ANY)]
    args = [proj, proj, proj, proj, proj, smt, gb_row, gb_col, mng.reshape(1, MLSTM_DV), c_all, n0, m0,
            c_out_prev]
    return pl.pallas_call(
        functools.partial(_mlstm_sample_kernel, ns=ns),
        grid=(t // CHUNK_ROWS,),
        in_specs=in_specs,
        out_specs=[pl.BlockSpec((CHUNK_ROWS, hv), lambda c: (c, 0)), state_c, state_n, state_m],
        out_shape=[jax.ShapeDtypeStruct((t, hv), F32),
                   jax.ShapeDtypeStruct(c_all.shape, F32),
                   jax.ShapeDtypeStruct(n0.shape, F32),
                   jax.ShapeDtypeStruct(m0.shape, F32)],
        input_output_aliases={len(args) - 1: 1},
        compiler_params=_cparams(("parallel",)),
        name="mlstm_sample",
    )(*args)


def _pad_cols(w, n):
    return jnp.pad(w, ((0, 0), (0, n - w.shape[1])))


def _split_in_weight_kernel(w_ref, main_ref, st_ref, *, n_a, n_small):
    w = w_ref[0]
    rows = w.shape[0]
    n_b = w.shape[1] - n_a - n_small
    main_ref[0, :, 0:n_a] = w[:, 0:n_a].astype(BF16)
    main_ref[0, :, n_a:n_a + n_b] = w[:, n_a + n_small:].astype(BF16)
    small = jnp.concatenate([w[:, n_a:n_a + n_small], jnp.zeros((rows, SMALL_COLS - n_small), F32)],
                            axis=1)
    main_ref[0, :, n_a + n_b:] = small.astype(BF16)
    st_ref[0] = small.T[0:SMALL_ROWS, :].astype(BF16)


def _split_in_weight(w, n_a, n_small):
    layers, d, n_tot = w.shape
    n_main = n_tot - n_small + SMALL_COLS
    rows = 256
    return pl.pallas_call(
        functools.partial(_split_in_weight_kernel, n_a=n_a, n_small=n_small),
        grid=(layers, d // rows),
        in_specs=[pl.BlockSpec((1, rows, n_tot), lambda l, r: (l, r, 0))],
        out_specs=[pl.BlockSpec((1, rows, n_main), lambda l, r: (l, r, 0)),
                   pl.BlockSpec((1, SMALL_ROWS, rows), lambda l, r: (l, 0, r))],
        out_shape=[jax.ShapeDtypeStruct((layers, d, n_main), BF16),
                   jax.ShapeDtypeStruct((layers, SMALL_ROWS, d), BF16)],
        compiler_params=_cparams(("parallel", "parallel")),
        name="split_in_weight",
    )(w)


def _chunk_lanes(smt):
    rows, t = smt.shape
    return smt.reshape(rows, t // CHUNK_ROWS, CHUNK_ROWS).transpose(1, 0, 2)


def _row_col_params(vals):
    row = jnp.stack([jnp.pad(v, (0, SMALL_COLS - v.shape[0])) for v in vals]).astype(F32)
    col = jnp.stack([jnp.pad(v, (0, SMALL_ROWS - v.shape[0])) for v in vals], axis=1).astype(F32)
    return row, col


def _trunk(x, is_prompt, s_gdn, s_conv, s_c, s_n, s_m, norm_g, w_in_even, conv_w, a_log, dt_bias,
           gdn_norm_g, sgu_norm_g, sgu_w, sgu_b, w_out_even, w_in_odd, gate_b_odd, mlstm_norm_g,
           w_out_odd, w_ff1, w_ff2):
    bsz, seq, d = x.shape
    t = bsz * seq
    depth = norm_g.shape[0]
    hq = N_HEADS * HEAD_DIM
    cch = 3 * hq
    tm = 1024 if t % 1024 == 0 else 512
    tb = min(512, seq)
    zeros_h = jnp.zeros((N_HEADS,), F32)
    x2 = x.reshape(t, d)
    new_gdn, new_conv, new_v, new_c, new_n, new_m = [], [], [], [], [], []
    gdn_out = None if is_prompt else jnp.zeros_like(s_gdn)
    c_out = None if is_prompt else jnp.zeros_like(s_c)
    for l in range(depth):
        gl = norm_g[l]
        if l % 2 == 0:
            e = l // 2
            prm_row, prm_col = _row_col_params([jnp.concatenate([zeros_h, jnp.exp(a_log[e])]),
                                                jnp.concatenate([zeros_h, dt_bias[e]])])
            if is_prompt:
                buf8 = jnp.pad(s_conv[e], ((0, 0), (8 - (CONV_W - 1), 0), (0, 0)))
                o_mix, s_new, tail = _even_prompt(
                    x2, gl[0], *w_in_even, e, conv_w[e], prm_row, prm_col, gdn_norm_g[e], s_gdn[e], buf8,
                    sgu_w[e], _pad_cols(sgu_b[e].T, SMALL_COLS), sgu_norm_g[e], bsz, seq, tb)
                buf_new = tail[:, 8 - (CONV_W - 1):, :]
                new_gdn.append(s_new)
                mix_acts, mix_idx = [o_mix], [(e, 0)]
            else:
                proj, smt = _proj_in(x2, gl[0], *w_in_even, e, 512)
                qkv = proj[:, :cch].reshape(bsz, seq, cch)
                xcat = jnp.concatenate([s_conv[e], qkv], axis=1)
                buf_new = xcat[:, seq:, :]
                xc = _conv_sample(xcat.reshape(bsz, (CONV_W - 1 + seq) * cch), conv_w[e], seq, cch)
                o_a, gdn_out = _gdn_sample(xc.reshape(t, cch), proj, _chunk_lanes(smt), prm_row, prm_col,
                                           gdn_norm_g[e], s_gdn, e, gdn_out, seq)
                ns = CHUNK_ROWS // seq
                w_tiles = jnp.tile(sgu_w[e][:, :seq, :seq], (1, ns, ns))
                b_cols = _pad_cols(jnp.tile(sgu_b[e][:, :seq].T, (ns, 1)), SMALL_COLS)
                o_b, vb = _sgu(proj, w_tiles, b_cols, sgu_norm_g[e], CHUNK_ROWS, ns, 1)
                new_v.append(vb.reshape(bsz, seq, hq))
                mix_acts, mix_idx = [o_a, o_b], [(e, 0), (e, 1)]
            new_conv.append(buf_new)
            w_out_all = w_out_even
        else:
            o = l // 2
            gb_row, gb_col = _row_col_params([gate_b_odd[o]])
            if is_prompt:
                n0 = s_n[o].reshape(bsz, 1, hq)
                m0 = _pad_cols(s_m[o], SMALL_COLS).reshape(bsz, 1, SMALL_COLS)
                hh, c_new, n_new, m_new = _odd_prompt(x2, gl[0], *w_in_odd, o, gb_row, gb_col,
                                                      mlstm_norm_g[o], s_c[o], n0, m0, bsz, seq, tb)
                m_new = m_new.reshape(bsz, SMALL_COLS)
                new_c.append(c_new)
            else:
                proj, smt = _proj_in(x2, gl[0], *w_in_odd, o, 512)
                n0 = s_n[o].reshape(bsz, hq)
                m0 = _pad_cols(s_m[o], SMALL_COLS)
                hh, c_out, n_new, m_new = _mlstm_sample(proj, _chunk_lanes(smt), gb_row, gb_col,
                                                        mlstm_norm_g[o], s_c, o, c_out, n0, m0, seq)
            new_n.append(n_new.reshape(bsz, N_HEADS, HEAD_DIM))
            new_m.append(m_new[:, :N_HEADS])
            mix_acts, mix_idx, w_out_all = [hh], [(o, 0)], w_out_odd
        x2 = _mix_ffn(x2, gl[1:4], mix_acts, w_out_all, mix_idx, w_ff1, w_ff2, l, tm, 1024)
    return (x2.reshape(bsz, seq, d),
            jnp.stack(new_gdn) if is_prompt else gdn_out, jnp.stack(new_conv),
            None if is_prompt else jnp.stack(new_v),
            jnp.stack(new_c) if is_prompt else c_out, jnp.stack(new_n), jnp.stack(new_m))


def kernel(x_prompt, x_sample, state_gdn, state_gdn_conv, state_mlstm_c, state_mlstm_n, state_mlstm_m,
           norm_g, w_in_even, conv_w, a_log, dt_bias, gdn_norm_g, sgu_norm_g, sgu_w, sgu_b, w_out_even,
           w_in_odd, gate_b_odd, mlstm_norm_g, w_out_odd, w_ff1, w_ff2):
    hq = N_HEADS * HEAD_DIM
    weights = (norm_g, _split_in_weight(w_in_even, 4 * hq, 2 * N_HEADS), conv_w, a_log, dt_bias,
               gdn_norm_g, sgu_norm_g, sgu_w, sgu_b, w_out_even.astype(BF16),
               _split_in_weight(w_in_odd, 2 * hq + N_HEADS * MLSTM_DV, 2 * N_HEADS), gate_b_odd,
               mlstm_norm_g, w_out_odd.astype(BF16), w_ff1.astype(BF16), w_ff2.astype(BF16))
    bp = x_prompt.shape[0]
    n_even, n_odd = state_gdn.shape[0], state_mlstm_c.shape[0]
    y_prompt, p_gdn, p_conv, _, p_c, p_n, p_m = _trunk(
        x_prompt, True,
        jnp.zeros((n_even, bp) + state_gdn.shape[2:], F32),
        jnp.zeros((n_even, bp) + state_gdn_conv.shape[2:], x_prompt.dtype),
        jnp.zeros((n_odd, bp) + state_mlstm_c.shape[2:], F32),
        jnp.zeros((n_odd, bp) + state_mlstm_n.shape[2:], F32),
        jnp.zeros((n_odd, bp) + state_mlstm_m.shape[2:], F32),
        *weights)
    y_sample, s_gdn, s_conv, s_v, s_c, s_n, s_m = _trunk(
        x_sample, False, state_gdn, state_gdn_conv, state_mlstm_c, state_mlstm_n, state_mlstm_m,
        *weights)
    return (y_prompt, y_sample, p_gdn, s_gdn, p_conv, s_conv, s_v, p_c, s_c, p_n, s_n, p_m, s_m)
```

```python
import functools
import math
from types import SimpleNamespace

import jax
import jax.numpy as jnp
from jax import lax
from jax.experimental import pallas as pl
from jax.experimental.pallas import tpu as pltpu

F32 = jnp.float32
BF16 = jnp.bfloat16

EPS = 1e-6
N_HEADS = 4
HEAD_DIM = 128
MLSTM_DV = 256
CONV_W = 4
CHUNK_ROWS = 64
SGU_CHUNK = 128
SMALL_COLS = 128
SMALL_ROWS = 16
VMEM_LIMIT = 56 * 1024 * 1024


def _cparams(sem):
    return pltpu.CompilerParams(dimension_semantics=sem, vmem_limit_bytes=VMEM_LIMIT)


def _mm(a, b):
    return jnp.dot(a.astype(BF16), b.astype(BF16), preferred_element_type=F32)


def _mm_nt(a, b):
    return lax.dot_general(a.astype(BF16), b.astype(BF16), (((1,), (1,)), ((), ())),
                           preferred_element_type=F32)


def _trunc_bf16(x):
    bits = lax.bitcast_convert_type(x, jnp.uint32) & jnp.uint32(0xFFFF0000)
    return lax.bitcast_convert_type(bits, F32)


def _split3(x):
    hi = _trunc_bf16(x)
    rest = x - hi
    mid = _trunc_bf16(rest)
    return hi, mid, rest - mid


def _sel_mm(sel, x):
    pieces = jnp.concatenate(_split3(x), axis=0).astype(BF16)
    return jnp.dot(jnp.concatenate([sel.astype(BF16)] * 3, axis=1), pieces, preferred_element_type=F32)


def _mm_sel(x, sel):
    pieces = jnp.concatenate(_split3(x), axis=1).astype(BF16)
    return jnp.dot(pieces, jnp.concatenate([sel.astype(BF16)] * 3, axis=0), preferred_element_type=F32)


def _rms(x, g):
    return x * lax.rsqrt(jnp.mean(x * x, axis=-1, keepdims=True) + EPS) * g


def _l2norm(x):
    return x * lax.rsqrt(jnp.sum(x * x, axis=-1, keepdims=True) + EPS)


def _softplus(x):
    return jnp.maximum(x, 0.0) + jnp.log1p(jnp.exp(-jnp.abs(x)))


def _sigmoid(x):
    return 0.5 * jnp.tanh(0.5 * x) + 0.5


def _silu(x):
    return x * _sigmoid(x)


def _gelu(x):
    return 0.5 * x * (1.0 + lax.erf(x * (2.0 ** -0.5)))


def _proj_in_kernel(x_ref, g_ref, w_ref, wst_ref, o_ref, ot_ref):
    _project_block(x_ref, g_ref, w_ref, wst_ref, o_ref, ot_ref)


def _proj_in(x2d, g, w_main_all, w_small_t_all, layer, tm):
    t, d = x2d.shape
    n = w_main_all.shape[2]
    return pl.pallas_call(
        _proj_in_kernel,
        grid=(t // tm,),
        in_specs=[pl.BlockSpec((tm, d), lambda i: (i, 0)),
                  pl.BlockSpec((1, d), lambda i: (0, 0)),
                  pl.BlockSpec((1, d, n), lambda i: (layer, 0, 0)),
                  pl.BlockSpec((1, SMALL_ROWS, d), lambda i: (layer, 0, 0))],
        out_specs=[pl.BlockSpec((tm, n), lambda i: (i, 0)),
                   pl.BlockSpec((SMALL_ROWS, tm), lambda i: (0, i))],
        out_shape=[jax.ShapeDtypeStruct((t, n), F32),
                   jax.ShapeDtypeStruct((SMALL_ROWS, t), F32)],
        compiler_params=_cparams(("parallel",)),
        name="proj_in",
    )(x2d, g.reshape(1, d), w_main_all, w_small_t_all)


def _mix_ffn_kernel(*refs, n_act, nk):
    x_ref, g_ref = refs[0], refs[1]
    acts = refs[2:2 + n_act]
    wos = refs[2 + n_act:2 + 2 * n_act]
    w1_ref, w2_ref, o_ref, h_ref, acc_ref = refs[2 + 2 * n_act:]
    k = pl.program_id(1)
    tm = x_ref.shape[0]
    halves = [slice(0, tm // 2), slice(tm // 2, tm)]

    def prologue(rs):
        mix = None
        for a_ref, w_ref in zip(acts, wos):
            p = jnp.dot(a_ref[rs, :].astype(BF16), w_ref[0], preferred_element_type=F32)
            mix = p if mix is None else mix + p
        x1 = x_ref[rs, :] + _rms(mix, g_ref[0:1, :])
        o_ref[rs, :] = x1
        h_ref[rs, :] = _rms(x1, g_ref[1:2, :]).astype(BF16)

    def chunk(rs, first):
        a = jnp.dot(h_ref[rs, :], w1_ref[0], preferred_element_type=F32)
        a = jnp.square(jnp.maximum(a, 0.0)).astype(BF16)
        part = jnp.dot(a, w2_ref[0], preferred_element_type=F32)
        acc_ref[rs, :] = part if first else acc_ref[rs, :] + part

    def epilogue(rs):
        o_ref[rs, :] = o_ref[rs, :] + _rms(acc_ref[rs, :], g_ref[2:3, :])

    @pl.when(k == 0)
    def _():
        for q in range(4):
            prologue(slice(q * tm // 4, (q + 1) * tm // 4))
        chunk(slice(0, tm), True)
        if nk == 1:
            for rs in halves:
                epilogue(rs)

    @pl.when((k > 0) & (k < nk - 1))
    def _():
        chunk(slice(0, tm), False)

    @pl.when((k > 0) & (k == nk - 1))
    def _():
        for rs in halves:
            chunk(rs, False)
            epilogue(rs)


def _mix_ffn(x2d, g3rows, acts, w_out_all, w_out_idx, w1_all, w2_all, layer, tm, tf):
    t, d = x2d.shape
    ff = w1_all.shape[2]
    n_act = len(acts)
    in_specs = [pl.BlockSpec((tm, d), lambda i, k: (i, 0)), pl.BlockSpec((3, d), lambda i, k: (0, 0))]
    in_specs += [pl.BlockSpec((tm, a.shape[1]), lambda i, k: (i, 0)) for a in acts]
    in_specs += [pl.BlockSpec((1, a.shape[1], d), lambda i, k, li=li, ri=ri: (li, ri, 0))
                 for a, (li, ri) in zip(acts, w_out_idx)]
    in_specs += [pl.BlockSpec((1, d, tf), lambda i, k: (layer, 0, k)),
                 pl.BlockSpec((1, tf, d), lambda i, k: (layer, k, 0))]
    return pl.pallas_call(
        functools.partial(_mix_ffn_kernel, n_act=n_act, nk=ff // tf),
        grid=(t // tm, ff // tf),
        in_specs=in_specs,
        out_specs=pl.BlockSpec((tm, d), lambda i, k: (i, 0)),
        out_shape=jax.ShapeDtypeStruct((t, d), F32),
        scratch_shapes=[pltpu.VMEM((tm, d), BF16), pltpu.VMEM((tm, d), F32)],
        compiler_params=_cparams(("parallel", "arbitrary")),
        name="mix_ffn",
    )(x2d, g3rows, *acts, *([w_out_all] * n_act), w1_all, w2_all)


def _ind(mask):
    return jnp.where(mask, 1.0, 0.0).astype(F32)


def _div_pow2(x, d):
    return jnp.right_shift(x, int(math.log2(d)))


def _chunk_structs(rows, ns):
    lt = rows // ns
    ri = lax.broadcasted_iota(jnp.int32, (rows, rows), 0)
    ci = lax.broadcasted_iota(jnp.int32, (rows, rows), 1)
    cs = SimpleNamespace(rows=rows, ns=ns, lt=lt)
    if ns == 1:
        cs.same = None
        cs.causal = ri >= ci
        cs.strict = ri > ci
        cs.upper = ri <= ci
    else:
        rseq = _div_pow2(ri, lt)
        cseq = _div_pow2(ci, lt)
        cs.same = rseq == cseq
        cs.causal = cs.same & (ri >= ci)
        cs.strict = cs.same & (ri > ci)
        cs.upper = cs.same & (ri <= ci)
        cs.last = _ind(ci == rseq * lt + (lt - 1))
        cs.last_t = _ind(ri == cseq * lt + (lt - 1))
        sr = _div_pow2(lax.broadcasted_iota(jnp.int32, (ns * HEAD_DIM, rows), 0), HEAD_DIM)
        sc = lax.broadcasted_iota(jnp.int32, (ns * HEAD_DIM, rows), 1)
        cs.stack_last = _ind(sc == sr * lt + (lt - 1))
        cs.stack_mask = _ind(_div_pow2(sc, lt) == sr)
        er = _div_pow2(lax.broadcasted_iota(jnp.int32, (rows, ns * HEAD_DIM), 0), lt)
        ec = _div_pow2(lax.broadcasted_iota(jnp.int32, (rows, ns * HEAD_DIM), 1), HEAD_DIM)
        cs.expand_mask = _ind(er == ec)
        kr = lax.broadcasted_iota(jnp.int32, (ns, rows), 0)
        kc = lax.broadcasted_iota(jnp.int32, (ns, rows), 1)
        cs.seq_last = _ind(kc == kr * lt + (lt - 1))
        cs.seq_sum = _ind(_div_pow2(kc, lt) == kr)
        pr = _div_pow2(lax.broadcasted_iota(jnp.int32, (rows, HEAD_DIM), 0), lt)
        pc = lax.broadcasted_iota(jnp.int32, (rows, HEAD_DIM), 1)
        cs.seq_expand = _ind(pr == pc)
    cs.lower_f = _ind(cs.causal)
    cs.upper_f = _ind(cs.upper)
    eye_r = lax.broadcasted_iota(jnp.int32, (HEAD_DIM, HEAD_DIM), 0)
    eye_c = lax.broadcasted_iota(jnp.int32, (HEAD_DIM, HEAD_DIM), 1)
    cs.eye = _ind(eye_r == eye_c)
    cs.eye_rows = _ind(ri == ci)
    return cs


def _last_rows(cs, x):
    if cs.ns == 1:
        return jnp.broadcast_to(x[cs.rows - 1:cs.rows, :], x.shape)
    return _sel_mm(cs.last, x)


def _last_lanes(cs, x):
    if cs.ns == 1:
        return jnp.broadcast_to(x[:, cs.rows - 1:cs.rows], x.shape)
    return _mm_sel(x, cs.last_t)


def _expand_lhs(cs, x):
    if cs.ns == 1:
        return x
    reps = x.shape[-2] // cs.rows
    mask = cs.expand_mask if reps == 1 else jnp.concatenate([cs.expand_mask] * reps, axis=0)
    return jnp.concatenate([x] * cs.ns, axis=-1) * mask


def _stack_t(cs, x):
    xt = _mm_nt(cs.eye, x)
    if cs.ns == 1:
        return xt
    return jnp.concatenate([xt] * cs.ns, axis=0) * cs.stack_mask


def _stack_scalar(cs, col):
    if cs.ns == 1:
        return col[cs.rows - 1:cs.rows, :]
    return _sel_mm(cs.stack_last, jnp.broadcast_to(col, (cs.rows, HEAD_DIM)))[:, 0:1]


_BATCH_NN = (((2,), (1,)), ((0,), (0,)))
_BATCH_NT = (((2,), (2,)), ((0,), (0,)))
_BATCH_TN = (((1,), (1,)), ((0,), (0,)))


def _bmm(a, b, dims=_BATCH_NN):
    return lax.dot_general(a.astype(BF16), b.astype(BF16), dims, preferred_element_type=F32)


def _split_lhs(a):
    hi = a.astype(BF16).astype(F32)
    return jnp.concatenate([hi, a - hi, hi], axis=-1).astype(BF16)


def _split_rhs(b):
    hi = b.astype(BF16).astype(F32)
    return jnp.concatenate([hi, hi, b - hi], axis=-2).astype(BF16)


def _bmm_split(a_cat, b_cat):
    return lax.dot_general(a_cat, b_cat, _BATCH_NN, preferred_element_type=F32)


def _unit_lower_inverse_off(cs, a):
    rows = cs.rows
    n = -a
    steps = int(math.log2(cs.lt)) - 1
    if steps == 0:
        return n
    pt = jnp.concatenate([_bmm_split(_split_lhs(n), _split_rhs(n)), n], axis=-1)
    right = lax.broadcasted_iota(jnp.int32, (rows, 2 * rows), 1) >= rows
    for _ in range(steps):
        hi = pt.astype(BF16).astype(F32)
        lo = pt - hi
        p_cat = jnp.concatenate([hi[..., :rows], lo[..., :rows], hi[..., :rows]], axis=-1).astype(BF16)
        pt_cat = jnp.concatenate([hi, hi, lo], axis=-2).astype(BF16)
        p_both = jnp.concatenate([pt[..., :rows], pt[..., :rows]], axis=-1)
        pt = _bmm_split(p_cat, pt_cat) + jnp.where(right, pt + p_both, 0.0)
    return pt[..., rows:]


def _gdn_block(cs, nchunks, xc, gate, sm, smt, prm_row, prm_col, gng, s_heads, write_out, pieces=None):
    rows = cs.rows
    hq = N_HEADS * HEAD_DIM
    pieces = pieces or _Pieces()
    pairs = [(c, h) for c in range(nchunks) for h in range(N_HEADS)]
    rs = lambda c: slice(c * rows, (c + 1) * rows)

    def heads(x, col0):
        return jnp.stack([x[rs(c), col0 + h * HEAD_DIM:col0 + (h + 1) * HEAD_DIM] for c, h in pairs])

    def cols(x, lane0):
        return jnp.stack([x[rs(c), lane0 + h:lane0 + h + 1] for c, h in pairs])

    beta_all = _sigmoid(sm)
    g_c = -prm_row[0:1, :] * _softplus(sm + prm_row[1:2, :])
    g_r = -prm_col[:, 0:1] * _softplus(smt + prm_col[:, 1:2])
    gc_c = jnp.concatenate([_sel_mm(cs.lower_f, g_c[rs(c)]) for c in range(nchunks)], axis=0)
    gc_r = jnp.concatenate([_mm_sel(g_r[:, rs(c)], cs.upper_f) for c in range(nchunks)], axis=1)
    gl_c = jnp.concatenate([_last_rows(cs, gc_c[rs(c)]) for c in range(nchunks)], axis=0)
    egc = jnp.exp(gc_c)
    ekd = jnp.exp(gl_c - gc_c)
    pieces.take()

    q = _l2norm(heads(xc, 0)) * (HEAD_DIM ** -0.5)
    k = _l2norm(heads(xc, hq))
    v = heads(xc, 2 * hq)
    beta = cols(beta_all, 0)
    gcol = cols(gc_c, N_HEADS)
    eg = cols(egc, N_HEADS)
    grow = jnp.stack([gc_r[N_HEADS + h:N_HEADS + h + 1, rs(c)] for c, h in pairs])
    decay = jnp.where(cs.causal, jnp.exp(jnp.where(cs.causal, gcol - grow, 0.0)), 0.0)
    kb = k * beta
    a = jnp.where(cs.strict, _bmm(kb, k, _BATCH_NT) * decay, 0.0)
    toff = _unit_lower_inverse_off(cs, a)
    rhs = jnp.concatenate([v * beta, kb * eg], axis=-1)
    sol = rhs + _bmm_split(_split_lhs(toff), _split_rhs(rhs))
    u_val, w_k = sol[..., :HEAD_DIM], sol[..., HEAD_DIM:]
    pieces.take()
    qk = jnp.where(cs.causal, _bmm(q, k, _BATCH_NT) * decay, 0.0)
    q_dec = q * eg
    k_dec = k * cols(ekd, N_HEADS)

    s = s_heads
    for c in range(nchunks):
        pieces.take()
        hs = slice(c * N_HEADS, (c + 1) * N_HEADS)
        p = _bmm(_expand_lhs(cs, jnp.concatenate([w_k[hs], q_dec[hs]], axis=1)), s)
        v_new = u_val[hs] - p[:, :rows]
        o = p[:, rows:] + _bmm(qk[hs], v_new)
        gl_cols = [gl_c[rs(c), N_HEADS + h:N_HEADS + h + 1] for h in range(N_HEADS)]
        g_last = jnp.exp(jnp.stack([_stack_scalar(cs, col) for col in gl_cols]))
        if cs.ns == 1:
            upd = _bmm(k_dec[hs], v_new, _BATCH_TN)
        else:
            kt = _bmm(jnp.broadcast_to(cs.eye, (N_HEADS, HEAD_DIM, HEAD_DIM)), k_dec[hs], _BATCH_NT)
            upd = _bmm(jnp.concatenate([kt] * cs.ns, axis=1) * cs.stack_mask, v_new)
        s = s * g_last + upd
        for h in range(N_HEADS):
            write_out(c, h, _rms(o[h], gng) * _silu(gate[rs(c), h * HEAD_DIM:(h + 1) * HEAD_DIM]))
    return s


PROJ_PIECE_COLS = 256


class _Pieces:
    def __init__(self, thunks=()):
        self._thunks = list(thunks)

    def take(self, count=1):
        for _ in range(count):
            if self._thunks:
                self._thunks.pop(0)()

    def flush(self):
        self.take(len(self._thunks))


def _project_pieces(x_ref, g_ref, w_ref, wst_ref, proj_w, smt_w, piece_cols=PROJ_PIECE_COLS):
    hn = _rms(x_ref[...], g_ref[...]).astype(BF16)
    n = proj_w.shape[1]

    def piece(lo, hi):
        def run():
            proj_w[:, lo:hi] = jnp.dot(hn, w_ref[0, :, lo:hi], preferred_element_type=F32)
        return run

    def gates_t():
        smt_w[...] = lax.dot_general(wst_ref[0], hn, (((1,), (1,)), ((), ())), preferred_element_type=F32)

    cols = range(0, n, piece_cols)
    return _Pieces([gates_t] + [piece(lo, min(lo + piece_cols, n)) for lo in cols])


def _project_block(x_ref, g_ref, w_ref, wst_ref, proj_w, smt_w):
    _project_pieces(x_ref, g_ref, w_ref, wst_ref, proj_w, smt_w).flush()


def _pipelined_steps(step):
    i = pl.program_id(0)
    for parity in range(2):
        @pl.when(lax.rem(i, 2) == parity)
        def _():
            step(parity)


def _even_prompt_kernel(x_ref, g_ref, w_ref, wst_ref, cw_ref, prow_ref, pcol_ref, gng_ref, s0_ref, buf_ref,
                        sw_ref, sb_ref, sg_ref, o_ref, sout_ref, bufout_ref,
                        proj_a, proj_b, smt_a, smt_b, s_scr, xp_scr, *, tb, nt):
    i = pl.program_id(0)
    t = lax.rem(jnp.maximum(i - 1, 0), nt)
    hq = N_HEADS * HEAD_DIM
    cch = 3 * hq
    cs = _chunk_structs(CHUNK_ROWS, 1)

    @pl.when(i == 0)
    def _():
        proj_b[...] = jnp.zeros_like(proj_b)
        smt_b[...] = jnp.zeros_like(smt_b)

    @pl.when(t == 0)
    def _():
        s_scr[...] = s0_ref[0]
        xp_scr[...] = buf_ref[0]

    def write_delta(c, h, o):
        o_ref[c * CHUNK_ROWS:(c + 1) * CHUNK_ROWS, h * HEAD_DIM:(h + 1) * HEAD_DIM] = o

    def write_gating(rs, g, val):
        o_ref[rs, hq + g * HEAD_DIM:hq + (g + 1) * HEAD_DIM] = val

    def step(parity):
        proj_w, smt_w, proj_r, smt_r = ((proj_a, smt_a, proj_b, smt_b) if parity == 0
                                        else (proj_b, smt_b, proj_a, smt_a))
        pieces = _project_pieces(x_ref, g_ref, w_ref, wst_ref, proj_w, smt_w)
        pieces.take(3)
        x = proj_r[:, 0:cch]
        xp = jnp.concatenate([xp_scr[...], x], axis=0)
        y = cw_ref[CONV_W - 1:CONV_W, :] * x
        for back in range(1, CONV_W):
            y = y + cw_ref[CONV_W - 1 - back:CONV_W - back, :] * pltpu.roll(xp, back, axis=0)[8:]
        xp_scr[...] = x[tb - 8:]
        pieces.take(2)
        s_scr[...] = _gdn_block(cs, tb // CHUNK_ROWS, _silu(y), proj_r[:, cch:cch + hq],
                                proj_r[:, cch + 3 * hq:cch + 3 * hq + SMALL_COLS], smt_r[...],
                                prow_ref[...], pcol_ref[...], gng_ref[...], s_scr[...], write_delta, pieces)
        pieces.flush()
        _sgu_apply(proj_r[:, cch + hq:cch + 2 * hq], proj_r[:, cch + 2 * hq:cch + 3 * hq],
                   sw_ref, sb_ref[...], sg_ref[...], SGU_CHUNK, 1, write_gating, None)

    _pipelined_steps(step)

    @pl.when(t == nt - 1)
    def _():
        sout_ref[0] = s_scr[...]
        bufout_ref[0] = xp_scr[...]


def _even_prompt(x2, g, w_main_all, w_small_t_all, e, conv_w, prm_row, prm_col, gng, s0, buf8,
                 sgu_w, sgu_b_cols, sgu_g, bsz, seq, tb):
    t, d = x2.shape
    hq = N_HEADS * HEAD_DIM
    cch = 3 * hq
    n_proj = w_main_all.shape[2]
    n = t // tb
    nt = seq // tb
    blk = lambda i: jnp.minimum(i, n - 1)
    seq_of = lambda i: jnp.maximum(i - 1, 0) // nt
    const2 = lambda i: (0, 0)
    return pl.pallas_call(
        functools.partial(_even_prompt_kernel, tb=tb, nt=nt),
        grid=(n + 1,),
        in_specs=[pl.BlockSpec((tb, d), lambda i: (blk(i), 0)),
                  pl.BlockSpec((1, d), const2),
                  pl.BlockSpec((1, d, n_proj), lambda i: (e, 0, 0)),
                  pl.BlockSpec((1, SMALL_ROWS, d), lambda i: (e, 0, 0)),
                  pl.BlockSpec((CONV_W, cch), const2),
                  pl.BlockSpec((2, SMALL_COLS), const2),
                  pl.BlockSpec((SMALL_ROWS, 2), const2),
                  pl.BlockSpec((1, HEAD_DIM), const2),
                  pl.BlockSpec((1, N_HEADS, HEAD_DIM, HEAD_DIM), lambda i: (seq_of(i), 0, 0, 0)),
                  pl.BlockSpec((1, 8, cch), lambda i: (seq_of(i), 0, 0)),
                  pl.BlockSpec((N_HEADS, SGU_CHUNK, SGU_CHUNK), lambda i: (0, 0, 0)),
                  pl.BlockSpec((SGU_CHUNK, SMALL_COLS), const2),
                  pl.BlockSpec((N_HEADS, HEAD_DIM), const2)],
        out_specs=[pl.BlockSpec((tb, 2 * hq), lambda i: (jnp.maximum(i - 1, 0), 0)),
                   pl.BlockSpec((1, N_HEADS, HEAD_DIM, HEAD_DIM), lambda i: (seq_of(i), 0, 0, 0)),
                   pl.BlockSpec((1, 8, cch), lambda i: (seq_of(i), 0, 0))],
        out_shape=[jax.ShapeDtypeStruct((t, 2 * hq), F32),
                   jax.ShapeDtypeStruct((bsz, N_HEADS, HEAD_DIM, HEAD_DIM), F32),
                   jax.ShapeDtypeStruct((bsz, 8, cch), F32)],
        scratch_shapes=[pltpu.VMEM((tb, n_proj), F32), pltpu.VMEM((tb, n_proj), F32),
                        pltpu.VMEM((SMALL_ROWS, tb), F32), pltpu.VMEM((SMALL_ROWS, tb), F32),
                        pltpu.VMEM((N_HEADS, HEAD_DIM, HEAD_DIM), F32),
                        pltpu.VMEM((8, cch), F32)],
        compiler_params=_cparams(("arbitrary",)),
        name="even_prompt",
    )(x2, g.reshape(1, d), w_main_all, w_small_t_all, conv_w, prm_row, prm_col,
      gng.reshape(1, HEAD_DIM), s0, buf8, sgu_w, sgu_b_cols, sgu_g)


def _conv_sample_kernel(xcat_ref, cw_ref, o_ref, *, lt, cch):
    for t in range(lt):
        y = None
        for j in range(CONV_W):
            term = cw_ref[j:j + 1, :] * xcat_ref[:, (t + j) * cch:(t + j + 1) * cch]
            y = term if y is None else y + term
        o_ref[:, t * cch:(t + 1) * cch] = _silu(y)


def _conv_sample(xcat, conv_w, lt, cch):
    nseq = xcat.shape[0]
    return pl.pallas_call(
        functools.partial(_conv_sample_kernel, lt=lt, cch=cch),
        grid=(1,),
        in_specs=[pl.BlockSpec(xcat.shape, lambda i: (0, 0)),
                  pl.BlockSpec((CONV_W, cch), lambda i: (0, 0))],
        out_specs=pl.BlockSpec((nseq, lt * cch), lambda i: (0, 0)),
        out_shape=jax.ShapeDtypeStruct((nseq, lt * cch), F32),
        compiler_params=_cparams(("arbitrary",)),
        name="conv_sample",
    )(xcat, conv_w)


def _gdn_sample_kernel(xc_ref, gate_ref, sm_ref, smt_ref, prow_ref, pcol_ref, gng_ref, s0_ref,
                       *rest, ns):
    o_ref, sout_ref = rest[-2:]
    cs = _chunk_structs(CHUNK_ROWS, ns)
    s_heads = jnp.stack([s0_ref[0, :, h].reshape(ns * HEAD_DIM, HEAD_DIM) for h in range(N_HEADS)])

    def write_out(c, h, o):
        o_ref[:, h * HEAD_DIM:(h + 1) * HEAD_DIM] = o

    s_new = _gdn_block(cs, 1, xc_ref[...], gate_ref[...], sm_ref[...], smt_ref[0],
                       prow_ref[...], pcol_ref[...], gng_ref[...], s_heads, write_out)
    for h in range(N_HEADS):
        sout_ref[0, :, h] = s_new[h].reshape(ns, HEAD_DIM, HEAD_DIM)


def _gdn_sample(xc, proj, smt, prm_row, prm_col, gng, s_all, layer, s_out_prev, lt):
    hq = N_HEADS * HEAD_DIM
    cch = 3 * hq
    ns = CHUNK_ROWS // lt
    nchunk = xc.shape[0] // CHUNK_ROWS
    small_blk = (cch + 3 * hq) // SMALL_COLS
    state_spec = pl.BlockSpec((1, ns, N_HEADS, HEAD_DIM, HEAD_DIM), lambda c: (layer, c, 0, 0, 0))
    in_specs = [pl.BlockSpec((CHUNK_ROWS, cch), lambda c: (c, 0)),
                pl.BlockSpec((CHUNK_ROWS, hq), lambda c: (c, cch // hq)),
                pl.BlockSpec((CHUNK_ROWS, SMALL_COLS), lambda c: (c, small_blk)),
                pl.BlockSpec((1, SMALL_ROWS, CHUNK_ROWS), lambda c: (c, 0, 0)),
                pl.BlockSpec((2, SMALL_COLS), lambda c: (0, 0)),
                pl.BlockSpec((SMALL_ROWS, 2), lambda c: (0, 0)),
                pl.BlockSpec((1, HEAD_DIM), lambda c: (0, 0)),
                state_spec,
                pl.BlockSpec(memory_space=pl.ANY)]
    args = [xc, proj, proj, smt, prm_row, prm_col, gng.reshape(1, HEAD_DIM), s_all, s_out_prev]
    return pl.pallas_call(
        functools.partial(_gdn_sample_kernel, ns=ns),
        grid=(nchunk,),
        in_specs=in_specs,
        out_specs=[pl.BlockSpec((CHUNK_ROWS, hq), lambda c: (c, 0)), state_spec],
        out_shape=[jax.ShapeDtypeStruct((xc.shape[0], hq), F32),
                   jax.ShapeDtypeStruct(s_all.shape, F32)],
        input_output_aliases={len(args) - 1: 1},
        compiler_params=_cparams(("parallel",)),
        name="gdn_sample",
    )(*args)


def _sgu_apply(u_pre, v_pre, w_ref, b_cols, g_rows, rows, ns, write_o, write_vb):
    lt = rows // ns
    ri = lax.broadcasted_iota(jnp.int32, (rows, rows), 0)
    ci = lax.broadcasted_iota(jnp.int32, (rows, rows), 1)
    keep = ri >= ci
    if ns > 1:
        keep = keep & (_div_pow2(ri, lt) == _div_pow2(ci, lt))
    for g in range(N_HEADS):
        lo = g * HEAD_DIM
        w = jnp.where(keep, w_ref[g], 0.0).astype(BF16)
        u = _gelu(u_pre[:, lo:lo + HEAD_DIM])
        vb = _rms(_gelu(v_pre[:, lo:lo + HEAD_DIM]), g_rows[g:g + 1, :])
        if write_vb is not None:
            write_vb(g, vb)
        for c in range(u_pre.shape[0] // rows):
            rs = slice(c * rows, (c + 1) * rows)
            write_o(rs, g, u[rs] * (_mm(w, vb[rs]) + b_cols[:, g:g + 1]))


def _sgu_kernel(u_ref, v_ref, w_ref, b_ref, g_ref, o_ref, vb_ref, *, rows, ns):
    def write_o(rs, g, val):
        o_ref[rs, g * HEAD_DIM:(g + 1) * HEAD_DIM] = val

    def write_vb(g, val):
        vb_ref[:, g * HEAD_DIM:(g + 1) * HEAD_DIM] = val

    _sgu_apply(u_ref[...], v_ref[...], w_ref, b_ref[...], g_ref[...], rows, ns, write_o, write_vb)


def _sgu(proj, w_tiles, b_cols, norm_g, rows, ns, nchunks):
    hq = N_HEADS * HEAD_DIM
    t = proj.shape[0]
    u_blk = (3 * hq + hq) // hq
    blk = rows * nchunks
    return pl.pallas_call(
        functools.partial(_sgu_kernel, rows=rows, ns=ns),
        grid=(t // blk,),
        in_specs=[pl.BlockSpec((blk, hq), lambda i: (i, u_blk)),
                  pl.BlockSpec((blk, hq), lambda i: (i, u_blk + 1)),
                  pl.BlockSpec((N_HEADS, rows, rows), lambda i: (0, 0, 0)),
                  pl.BlockSpec((rows, SMALL_COLS), lambda i: (0, 0)),
                  pl.BlockSpec((N_HEADS, HEAD_DIM), lambda i: (0, 0))],
        out_specs=[pl.BlockSpec((blk, hq), lambda i: (i, 0)),
                   pl.BlockSpec((blk, hq), lambda i: (i, 0))],
        out_shape=[jax.ShapeDtypeStruct((t, hq), F32), jax.ShapeDtypeStruct((t, hq), F32)],
        compiler_params=_cparams(("parallel",)),
        name="sgu",
    )(proj, proj, w_tiles, b_cols, norm_g)


def _logsigmoid(x):
    return jnp.minimum(x, 0.0) - jnp.log1p(jnp.exp(-jnp.abs(x)))


def _mlstm_chunk(cs, q_all, k_all, v_all, opre, sm, smt, gb_row, gb_col, mng, c_list, n_rows, m_rows):
    neg_inf = -jnp.inf
    pre_c = sm + gb_row
    pre_r = smt + gb_col
    b_c = _sel_mm(cs.lower_f, _logsigmoid(pre_c))
    b_r = _mm_sel(_logsigmoid(pre_r), cs.upper_f)
    bl_c = _last_rows(cs, b_c)
    bl_r = _last_lanes(cs, b_r)
    res = []
    for h in range(N_HEADS):
        lo = h * HEAD_DIM
        vo = h * MLSTM_DV
        q = q_all[:, lo:lo + HEAD_DIM] * (HEAD_DIM ** -0.5)
        k = k_all[:, lo:lo + HEAD_DIM]
        v = v_all[:, vo:vo + MLSTM_DV]
        bcol = b_c[:, N_HEADS + h:N_HEADS + h + 1]
        brow = b_r[N_HEADS + h:N_HEADS + h + 1, :]
        icol = pre_c[:, h:h + 1]
        irow = pre_r[h:h + 1, :]
        blcol = bl_c[:, N_HEADS + h:N_HEADS + h + 1]
        blrow = bl_r[N_HEADS + h:N_HEADS + h + 1, :]
        mrow = m_rows[:, h:h + 1]
        inter = bcol + mrow
        dmat = jnp.where(cs.causal, bcol - brow + irow, neg_inf)
        m_t = jnp.maximum(inter, jnp.max(dmat, axis=-1, keepdims=True))
        w_intra = jnp.exp(dmat - m_t)
        w_inter = jnp.exp(inter - m_t)
        s = _mm_nt(q, k) * w_intra
        c_old = c_list[h]
        num = w_inter * _mm(_expand_lhs(cs, q), c_old) + _mm(s, v)
        qn = jnp.sum(q * n_rows[:, lo:lo + HEAD_DIM], axis=-1, keepdims=True)
        den = w_inter * qn + jnp.sum(s, axis=-1, keepdims=True)
        hval = num / jnp.maximum(jnp.abs(den), jnp.exp(-m_t))
        out = _rms(hval, mng) * _sigmoid(opre[:, vo:vo + MLSTM_DV])
        logw_row = blrow - brow + irow
        if cs.ns == 1:
            seq_max = jnp.max(logw_row, axis=-1, keepdims=True)
        else:
            seq_max = jnp.max(jnp.where(cs.same, logw_row, neg_inf), axis=-1, keepdims=True)
        m_new = jnp.maximum(blcol + mrow, seq_max)
        keep = jnp.exp(blcol + mrow - m_new)
        wk = jnp.exp(blcol - bcol + icol - m_new)
        kw = k * wk
        c_new = c_old * _stack_scalar(cs, keep) + _mm(_stack_t(cs, kw), v)
        res.append((out, c_new, keep, m_new, kw))
    return res


def _mlstm_prompt_block(cs, nchunks, q_all, k_all, v_all, opre, sm, smt, gb_row, gb_col, mng,
                        c_heads, n_heads, m_heads, write_out, pieces):
    rows = cs.rows
    pairs = [(c, h) for c in range(nchunks) for h in range(N_HEADS)]
    rs = lambda c: slice(c * rows, (c + 1) * rows)

    def heads(x, width):
        return jnp.stack([x[rs(c), h * width:(h + 1) * width] for c, h in pairs])

    def cols(x, lane0):
        return jnp.stack([x[rs(c), lane0 + h:lane0 + h + 1] for c, h in pairs])

    def lanes(x, row0):
        return jnp.stack([x[row0 + h:row0 + h + 1, rs(c)] for c, h in pairs])

    pre_c = sm + gb_row
    pre_r = smt + gb_col
    lf_c = _logsigmoid(pre_c)
    lf_r = _logsigmoid(pre_r)
    b_c = jnp.concatenate([_sel_mm(cs.lower_f, lf_c[rs(c)]) for c in range(nchunks)], axis=0)
    b_r = jnp.concatenate([_mm_sel(lf_r[:, rs(c)], cs.upper_f) for c in range(nchunks)], axis=1)
    pieces.take(3)

    q = heads(q_all, HEAD_DIM) * (HEAD_DIM ** -0.5)
    k = heads(k_all, HEAD_DIM)
    v = heads(v_all, MLSTM_DV)
    bcol = cols(b_c, N_HEADS)
    icol = cols(pre_c, 0)
    brow = lanes(b_r, N_HEADS)
    irow = lanes(pre_r, 0)
    blast = bcol[:, rows - 1:rows, :]
    dmat = jnp.where(cs.causal, bcol - brow + irow, -jnp.inf)
    rmax = jnp.max(dmat, axis=-1, keepdims=True)
    sp = _bmm(q, k, _BATCH_NT) * jnp.exp(dmat - rmax)
    pieces.take(3)
    sv = _bmm(sp, v)
    ssum = jnp.sum(sp, axis=-1, keepdims=True)
    pieces.take()
    lmax = jnp.max(blast - brow + irow, axis=-1, keepdims=True)
    kwp = k * jnp.exp(blast - bcol + icol - lmax)
    upd = _bmm(kwp, v, _BATCH_TN)
    nsum = jnp.sum(kwp, axis=1, keepdims=True)
    pieces.take()

    cst, n, m = c_heads, n_heads, m_heads
    for c in range(nchunks):
        pieces.take()
        hs = slice(c * N_HEADS, (c + 1) * N_HEADS)
        inter = bcol[hs] + m
        m_t = jnp.maximum(inter, rmax[hs])
        w_inter = jnp.exp(inter - m_t)
        w_intra = jnp.exp(rmax[hs] - m_t)
        num = w_inter * _bmm(q[hs], cst) + w_intra * sv[hs]
        den = (w_inter * jnp.sum(q[hs] * n, axis=-1, keepdims=True) + w_intra * ssum[hs])
        hval = num / jnp.maximum(jnp.abs(den), jnp.exp(-m_t))
        for h in range(N_HEADS):
            write_out(c, h, _rms(hval[h], mng) * _sigmoid(opre[rs(c), h * MLSTM_DV:(h + 1) * MLSTM_DV]))
        m_new = jnp.maximum(blast[hs] + m, lmax[hs])
        keep = jnp.exp(blast[hs] + m - m_new)
        fresh = jnp.exp(lmax[hs] - m_new)
        cst = keep * cst + fresh * upd[hs]
        n = keep * n + fresh * nsum[hs]
        m = m_new
    return cst, n, m


def _odd_prompt_kernel(x_ref, g_ref, w_ref, wst_ref, gbr_ref, gbc_ref, mng_ref, c0_ref, n0_ref, m0_ref,
                       o_ref, cout_ref, nout_ref, mout_ref,
                       proj_a, proj_b, smt_a, smt_b, c_scr, n_scr, m_scr, *, tb, nt):
    i = pl.program_id(0)
    t = lax.rem(jnp.maximum(i - 1, 0), nt)
    hq = N_HEADS * HEAD_DIM
    hv = N_HEADS * MLSTM_DV
    cs = _chunk_structs(CHUNK_ROWS, 1)
    lane = lax.broadcasted_iota(jnp.int32, (1, SMALL_COLS), 1)

    @pl.when(i == 0)
    def _():
        proj_b[...] = jnp.zeros_like(proj_b)
        smt_b[...] = jnp.zeros_like(smt_b)

    @pl.when(t == 0)
    def _():
        c_scr[...] = c0_ref[0]
        n_scr[...] = n0_ref[0]
        m_scr[...] = m0_ref[0]

    def write_out(c, h, o):
        o_ref[c * CHUNK_ROWS:(c + 1) * CHUNK_ROWS, h * MLSTM_DV:(h + 1) * MLSTM_DV] = o

    def step(parity):
        proj_w, smt_w, proj_r, smt_r = ((proj_a, smt_a, proj_b, smt_b) if parity == 0
                                        else (proj_b, smt_b, proj_a, smt_a))
        pieces = _project_pieces(x_ref, g_ref, w_ref, wst_ref, proj_w, smt_w, 2 * PROJ_PIECE_COLS)
        pieces.take(2)
        n_cur = n_scr[...]
        m_cur = m_scr[...]
        n_heads = jnp.stack([n_cur[:, h * HEAD_DIM:(h + 1) * HEAD_DIM] for h in range(N_HEADS)])
        m_heads = jnp.stack([m_cur[:, h:h + 1] for h in range(N_HEADS)])
        c_new, n_heads, m_heads = _mlstm_prompt_block(
            cs, tb // CHUNK_ROWS, proj_r[:, 0:hq], proj_r[:, hq:2 * hq], proj_r[:, 2 * hq:2 * hq + hv],
            proj_r[:, 2 * hq + hv:2 * hq + 2 * hv],
            proj_r[:, 2 * hq + 2 * hv:2 * hq + 2 * hv + SMALL_COLS], smt_r[...],
            gbr_ref[...], gbc_ref[...], mng_ref[...], c_scr[...], n_heads, m_heads, write_out, pieces)
        pieces.flush()
        c_scr[...] = c_new
        n_scr[...] = jnp.concatenate([n_heads[h] for h in range(N_HEADS)], axis=1)
        for h in range(N_HEADS):
            m_cur = jnp.where(lane == h, m_heads[h], m_cur)
        m_scr[...] = m_cur

    _pipelined_steps(step)

    @pl.when(t == nt - 1)
    def _():
        cout_ref[0] = c_scr[...]
        nout_ref[0] = n_scr[...]
        mout_ref[0] = m_scr[...]


def _odd_prompt(x2, g, w_main_all, w_small_t_all, o, gb_row, gb_col, mng, c0, n0, m0, bsz, seq, tb):
    t, d = x2.shape
    hq = N_HEADS * HEAD_DIM
    hv = N_HEADS * MLSTM_DV
    n_proj = w_main_all.shape[2]
    n = t // tb
    nt = seq // tb
    blk = lambda i: jnp.minimum(i, n - 1)
    seq_of = lambda i: jnp.maximum(i - 1, 0) // nt
    const2 = lambda i: (0, 0)
    state_c = pl.BlockSpec((1, N_HEADS, HEAD_DIM, MLSTM_DV), lambda i: (seq_of(i), 0, 0, 0))
    state_n = pl.BlockSpec((1, 1, hq), lambda i: (seq_of(i), 0, 0))
    state_m = pl.BlockSpec((1, 1, SMALL_COLS), lambda i: (seq_of(i), 0, 0))
    return pl.pallas_call(
        functools.partial(_odd_prompt_kernel, tb=tb, nt=nt),
        grid=(n + 1,),
        in_specs=[pl.BlockSpec((tb, d), lambda i: (blk(i), 0)),
                  pl.BlockSpec((1, d), const2),
                  pl.BlockSpec((1, d, n_proj), lambda i: (o, 0, 0)),
                  pl.BlockSpec((1, SMALL_ROWS, d), lambda i: (o, 0, 0)),
                  pl.BlockSpec((1, SMALL_COLS), const2),
                  pl.BlockSpec((SMALL_ROWS, 1), const2),
                  pl.BlockSpec((1, MLSTM_DV), const2),
                  state_c, state_n, state_m],
        out_specs=[pl.BlockSpec((tb, hv), lambda i: (jnp.maximum(i - 1, 0), 0)),
                   state_c, state_n, state_m],
        out_shape=[jax.ShapeDtypeStruct((t, hv), F32),
                   jax.ShapeDtypeStruct(c0.shape, F32),
                   jax.ShapeDtypeStruct(n0.shape, F32),
                   jax.ShapeDtypeStruct(m0.shape, F32)],
        scratch_shapes=[pltpu.VMEM((tb, n_proj), F32), pltpu.VMEM((tb, n_proj), F32),
                        pltpu.VMEM((SMALL_ROWS, tb), F32), pltpu.VMEM((SMALL_ROWS, tb), F32),
                        pltpu.VMEM((N_HEADS, HEAD_DIM, MLSTM_DV), F32),
                        pltpu.VMEM((1, hq), F32),
                        pltpu.VMEM((1, SMALL_COLS), F32)],
        compiler_params=_cparams(("arbitrary",)),
        name="odd_prompt",
    )(x2, g.reshape(1, d), w_main_all, w_small_t_all, gb_row, gb_col, mng.reshape(1, MLSTM_DV), c0, n0, m0)


def _mlstm_sample_kernel(q_ref, k_ref, v_ref, op_ref, sm_ref, smt_ref, gbr_ref, gbc_ref, mng_ref,
                         c0_ref, n0_ref, m0_ref, *rest, ns):
    o_ref, cout_ref, nout_ref, mout_ref = rest[-4:]
    cs = _chunk_structs(CHUNK_ROWS, ns)
    lane = lax.broadcasted_iota(jnp.int32, (1, SMALL_COLS), 1)
    c_list = [c0_ref[0, :, h].reshape(ns * HEAD_DIM, MLSTM_DV) for h in range(N_HEADS)]
    n0 = n0_ref[...]
    m0 = m0_ref[...]
    n_rows = _sel_mm(cs.seq_expand, jnp.concatenate(
        [n0, jnp.zeros((HEAD_DIM - ns, n0.shape[1]), F32)], axis=0))
    m_rows = _sel_mm(cs.seq_expand, jnp.concatenate(
        [m0, jnp.zeros((HEAD_DIM - ns, m0.shape[1]), F32)], axis=0))
    res = _mlstm_chunk(cs, q_ref[...], k_ref[...], v_ref[...], op_ref[...], sm_ref[...], smt_ref[0],
                       gbr_ref[...], gbc_ref[...], mng_ref[...], c_list, n_rows, m_rows)
    m_all = jnp.zeros((CHUNK_ROWS, SMALL_COLS), F32)
    for h, (out, c_new, keep, m_new, kw) in enumerate(res):
        o_ref[:, h * MLSTM_DV:(h + 1) * MLSTM_DV] = out
        cout_ref[0, :, h] = c_new.reshape(ns, HEAD_DIM, MLSTM_DV)
        keep_seq = _sel_mm(cs.seq_last, jnp.broadcast_to(keep, (CHUNK_ROWS, HEAD_DIM)))
        nout_ref[:, h * HEAD_DIM:(h + 1) * HEAD_DIM] = (
            keep_seq * n0[:, h * HEAD_DIM:(h + 1) * HEAD_DIM] + _sel_mm(cs.seq_sum, kw))
        m_all = jnp.where(lane == h, m_new, m_all)
    mout_ref[...] = _sel_mm(cs.seq_last, m_all)


def _mlstm_sample(proj, smt, gb_row, gb_col, mng, c_all, layer, c_out_prev, n0, m0, lt):
    hq = N_HEADS * HEAD_DIM
    hv = N_HEADS * MLSTM_DV
    ns = CHUNK_ROWS // lt
    t = proj.shape[0]
    small_blk = (2 * hq + 2 * hv) // SMALL_COLS
    state_c = pl.BlockSpec((1, ns, N_HEADS, HEAD_DIM, MLSTM_DV), lambda c: (layer, c, 0, 0, 0))
    state_n = pl.BlockSpec((ns, hq), lambda c: (c, 0))
    state_m = pl.BlockSpec((ns, SMALL_COLS), lambda c: (c, 0))
    in_specs = [pl.BlockSpec((CHUNK_ROWS, hq), lambda c: (c, 0)),
                pl.BlockSpec((CHUNK_ROWS, hq), lambda c: (c, 1)),
                pl.BlockSpec((CHUNK_ROWS, hv), lambda c: (c, 2 * hq // hv)),
                pl.BlockSpec((CHUNK_ROWS, hv), lambda c: (c, 2 * hq // hv + 1)),
                pl.BlockSpec((CHUNK_ROWS, SMALL_COLS), lambda c: (c, small_blk)),
                pl.BlockSpec((1, SMALL_ROWS, CHUNK_ROWS), lambda c: (c, 0, 0)),
                pl.BlockSpec((1, SMALL_COLS), lambda c: (0, 0)),
                pl.BlockSpec((SMALL_ROWS, 1), lambda c: (0, 0)),
                pl.BlockSpec((1, MLSTM_DV), lambda c: (0, 0)),
                state_c, state_n, state_m,
                pl.BlockSpec(memory_space=pl.ANY)]
    args = [proj, proj, proj, proj, proj, smt, gb_row, gb_col, mng.reshape(1, MLSTM_DV), c_all, n0, m0,
            c_out_prev]
    return pl.pallas_call(
        functools.partial(_mlstm_sample_kernel, ns=ns),
        grid=(t // CHUNK_ROWS,),
        in_specs=in_specs,
        out_specs=[pl.BlockSpec((CHUNK_ROWS, hv), lambda c: (c, 0)), state_c, state_n, state_m],
        out_shape=[jax.ShapeDtypeStruct((t, hv), F32),
                   jax.ShapeDtypeStruct(c_all.shape, F32),
                   jax.ShapeDtypeStruct(n0.shape, F32),
                   jax.ShapeDtypeStruct(m0.shape, F32)],
        input_output_aliases={len(args) - 1: 1},
        compiler_params=_cparams(("parallel",)),
        name="mlstm_sample",
    )(*args)


def _pad_cols(w, n):
    return jnp.pad(w, ((0, 0), (0, n - w.shape[1])))


def _split_in_weight_kernel(w_ref, main_ref, st_ref, *, n_a, n_small):
    w = w_ref[0]
    rows = w.shape[0]
    n_b = w.shape[1] - n_a - n_small
    main_ref[0, :, 0:n_a] = w[:, 0:n_a].astype(BF16)
    main_ref[0, :, n_a:n_a + n_b] = w[:, n_a + n_small:].astype(BF16)
    small = jnp.concatenate([w[:, n_a:n_a + n_small], jnp.zeros((rows, SMALL_COLS - n_small), F32)],
                            axis=1)
    main_ref[0, :, n_a + n_b:] = small.astype(BF16)
    st_ref[0] = small.T[0:SMALL_ROWS, :].astype(BF16)


def _split_in_weight(w, n_a, n_small):
    layers, d, n_tot = w.shape
    n_main = n_tot - n_small + SMALL_COLS
    rows = 256
    return pl.pallas_call(
        functools.partial(_split_in_weight_kernel, n_a=n_a, n_small=n_small),
        grid=(layers, d // rows),
        in_specs=[pl.BlockSpec((1, rows, n_tot), lambda l, r: (l, r, 0))],
        out_specs=[pl.BlockSpec((1, rows, n_main), lambda l, r: (l, r, 0)),
                   pl.BlockSpec((1, SMALL_ROWS, rows), lambda l, r: (l, 0, r))],
        out_shape=[jax.ShapeDtypeStruct((layers, d, n_main), BF16),
                   jax.ShapeDtypeStruct((layers, SMALL_ROWS, d), BF16)],
        compiler_params=_cparams(("parallel", "parallel")),
        name="split_in_weight",
    )(w)


def _chunk_lanes(smt):
    rows, t = smt.shape
    return smt.reshape(rows, t // CHUNK_ROWS, CHUNK_ROWS).transpose(1, 0, 2)


def _row_col_params(vals):
    row = jnp.stack([jnp.pad(v, (0, SMALL_COLS - v.shape[0])) for v in vals]).astype(F32)
    col = jnp.stack([jnp.pad(v, (0, SMALL_ROWS - v.shape[0])) for v in vals], axis=1).astype(F32)
    return row, col


def _trunk(x, is_prompt, s_gdn, s_conv, s_c, s_n, s_m, norm_g, w_in_even, conv_w, a_log, dt_bias,
           gdn_norm_g, sgu_norm_g, sgu_w, sgu_b, w_out_even, w_in_odd, gate_b_odd, mlstm_norm_g,
           w_out_odd, w_ff1, w_ff2):
    bsz, seq, d = x.shape
    t = bsz * seq
    depth = norm_g.shape[0]
    hq = N_HEADS * HEAD_DIM
    cch = 3 * hq
    tm = 1024 if t % 1024 == 0 else 512
    tb = min(512, seq)
    zeros_h = jnp.zeros((N_HEADS,), F32)
    x2 = x.reshape(t, d)
    new_gdn, new_conv, new_v, new_c, new_n, new_m = [], [], [], [], [], []
    gdn_out = None if is_prompt else jnp.zeros_like(s_gdn)
    c_out = None if is_prompt else jnp.zeros_like(s_c)
    for l in range(depth):
        gl = norm_g[l]
        if l % 2 == 0:
            e = l // 2
            prm_row, prm_col = _row_col_params([jnp.concatenate([zeros_h, jnp.exp(a_log[e])]),
                                                jnp.concatenate([zeros_h, dt_bias[e]])])
            if is_prompt:
                buf8 = jnp.pad(s_conv[e], ((0, 0), (8 - (CONV_W - 1), 0), (0, 0)))
                o_mix, s_new, tail = _even_prompt(
                    x2, gl[0], *w_in_even, e, conv_w[e], prm_row, prm_col, gdn_norm_g[e], s_gdn[e], buf8,
                    sgu_w[e], _pad_cols(sgu_b[e].T, SMALL_COLS), sgu_norm_g[e], bsz, seq, tb)
                buf_new = tail[:, 8 - (CONV_W - 1):, :]
                new_gdn.append(s_new)
                mix_acts, mix_idx = [o_mix], [(e, 0)]
            else:
                proj, smt = _proj_in(x2, gl[0], *w_in_even, e, 512)
                qkv = proj[:, :cch].reshape(bsz, seq, cch)
                xcat = jnp.concatenate([s_conv[e], qkv], axis=1)
                buf_new = xcat[:, seq:, :]
                xc = _conv_sample(xcat.reshape(bsz, (CONV_W - 1 + seq) * cch), conv_w[e], seq, cch)
                o_a, gdn_out = _gdn_sample(xc.reshape(t, cch), proj, _chunk_lanes(smt), prm_row, prm_col,
                                           gdn_norm_g[e], s_gdn, e, gdn_out, seq)
                ns = CHUNK_ROWS // seq
                w_tiles = jnp.tile(sgu_w[e][:, :seq, :seq], (1, ns, ns))
                b_cols = _pad_cols(jnp.tile(sgu_b[e][:, :seq].T, (ns, 1)), SMALL_COLS)
                o_b, vb = _sgu(proj, w_tiles, b_cols, sgu_norm_g[e], CHUNK_ROWS, ns, 1)
                new_v.append(vb.reshape(bsz, seq, hq))
                mix_acts, mix_idx = [o_a, o_b], [(e, 0), (e, 1)]
            new_conv.append(buf_new)
            w_out_all = w_out_even
        else:
            o = l // 2
            gb_row, gb_col = _row_col_params([gate_b_odd[o]])
            if is_prompt:
                n0 = s_n[o].reshape(bsz, 1, hq)
                m0 = _pad_cols(s_m[o], SMALL_COLS).reshape(bsz, 1, SMALL_COLS)
                hh, c_new, n_new, m_new = _odd_prompt(x2, gl[0], *w_in_odd, o, gb_row, gb_col,
                                                      mlstm_norm_g[o], s_c[o], n0, m0, bsz, seq, tb)
                m_new = m_new.reshape(bsz, SMALL_COLS)
                new_c.append(c_new)
            else:
                proj, smt = _proj_in(x2, gl[0], *w_in_odd, o, 512)
                n0 = s_n[o].reshape(bsz, hq)
                m0 = _pad_cols(s_m[o], SMALL_COLS)
                hh, c_out, n_new, m_new = _mlstm_sample(proj, _chunk_lanes(smt), gb_row, gb_col,
                                                        mlstm_norm_g[o], s_c, o, c_out, n0, m0, seq)
            new_n.append(n_new.reshape(bsz, N_HEADS, HEAD_DIM))
            new_m.append(m_new[:, :N_HEADS])
            mix_acts, mix_idx, w_out_all = [hh], [(o, 0)], w_out_odd
        x2 = _mix_ffn(x2, gl[1:4], mix_acts, w_out_all, mix_idx, w_ff1, w_ff2, l, tm, 1024)
    return (x2.reshape(bsz, seq, d),
            jnp.stack(new_gdn) if is_prompt else gdn_out, jnp.stack(new_conv),
            None if is_prompt else jnp.stack(new_v),
            jnp.stack(new_c) if is_prompt else c_out, jnp.stack(new_n), jnp.stack(new_m))


def kernel(x_prompt, x_sample, state_gdn, state_gdn_conv, state_mlstm_c, state_mlstm_n, state_mlstm_m,
           norm_g, w_in_even, conv_w, a_log, dt_bias, gdn_norm_g, sgu_norm_g, sgu_w, sgu_b, w_out_even,
           w_in_odd, gate_b_odd, mlstm_norm_g, w_out_odd, w_ff1, w_ff2):
    hq = N_HEADS * HEAD_DIM
    weights = (norm_g, _split_in_weight(w_in_even, 4 * hq, 2 * N_HEADS), conv_w, a_log, dt_bias,
               gdn_norm_g, sgu_norm_g, sgu_w, sgu_b, w_out_even.astype(BF16),
               _split_in_weight(w_in_odd, 2 * hq + N_HEADS * MLSTM_DV, 2 * N_HEADS), gate_b_odd,
               mlstm_norm_g, w_out_odd.astype(BF16), w_ff1.astype(BF16), w_ff2.astype(BF16))
    bp = x_prompt.shape[0]
    n_even, n_odd = state_gdn.shape[0], state_mlstm_c.shape[0]
    y_prompt, p_gdn, p_conv, _, p_c, p_n, p_m = _trunk(
        x_prompt, True,
        jnp.zeros((n_even, bp) + state_gdn.shape[2:], F32),
        jnp.zeros((n_even, bp) + state_gdn_conv.shape[2:], x_prompt.dtype),
        jnp.zeros((n_odd, bp) + state_mlstm_c.shape[2:], F32),
        jnp.zeros((n_odd, bp) + state_mlstm_n.shape[2:], F32),
        jnp.zeros((n_odd, bp) + state_mlstm_m.shape[2:], F32),
        *weights)
    y_sample, s_gdn, s_conv, s_v, s_c, s_n, s_m = _trunk(
        x_sample, False, state_gdn, state_gdn_conv, state_mlstm_c, state_mlstm_n, state_mlstm_m,
        *weights)
    return (y_prompt, y_sample, p_gdn, s_gdn, p_conv, s_conv, s_v, p_c, s_c, p_n, s_n, p_m, s_m)
```
